```python
import math, functools
import jax, jax.numpy as jnp
from jax import lax
import numpy as np

D_MODEL = 1024
BATCH = 4
SEQ = 4096
DEPTH = 4
DEC_BATCH = 32
DEC_SEQ = 8
PAST_LEN = 8192
PAGE_SIZE = 128

HEAD_DIM = 64
MIX_DIM = D_MODEL
CONV_DIM = MIX_DIM // 4
ATTN_DIM = MIX_DIM // 2
SSD_DIM = MIX_DIM // 4
FOX_HEADS = ATTN_DIM // HEAD_DIM
SC_WIDTH = 3
SSD_HEADDIM = 64
SSD_HEADS = SSD_DIM // SSD_HEADDIM
SSD_GROUPS = 2
SSD_STATE = 64
SSD_CONV = 4
SSD_CHUNK = 128
XBC_DIM = SSD_DIM + 2 * SSD_GROUPS * SSD_STATE
PROJ_SIZES = (CONV_DIM, CONV_DIM, CONV_DIM, ATTN_DIM, ATTN_DIM, ATTN_DIM, FOX_HEADS, SSD_DIM, XBC_DIM, SSD_HEADS)
PROJ_DIM = 3 * CONV_DIM + 3 * ATTN_DIM + FOX_HEADS + SSD_DIM + XBC_DIM + SSD_HEADS
FFN_DIM = 2816
MEM_LEN = 256
MEM_HEADS = 4
MEM_HEAD_DIM = 128
MEM_DIM = MEM_HEADS * MEM_HEAD_DIM
Q_BLOCK = 128
RMS_EPS = 1e-6

kernel_name = 'hybrid_conv_fox_ssd_macaron_decoder_step'


def rms_norm(x, g):
    xf = x.astype(jnp.float32)
    y = xf * lax.rsqrt(jnp.mean(xf * xf, axis=-1, keepdims=True) + RMS_EPS)
    return (y * g.astype(jnp.float32)).astype(x.dtype)


def swiglu_half(x, g, wg, wu, wd):
    h = rms_norm(x, g)
    return x + 0.5 * ((jax.nn.silu(h @ wg) * (h @ wu)) @ wd)


def causal_dwconv(u, state, w):
    width = w.shape[0]
    L = u.shape[1]
    full = jnp.concatenate([state.astype(u.dtype), u], axis=1)
    y = full[:, 0:L] * w[0]
    for k in range(1, width):
        y = y + full[:, k:k + L] * w[k]
    return y, full[:, full.shape[1] - (width - 1):]


def split_proj(u):
    outs = []
    o = 0
    for n in PROJ_SIZES:
        outs.append(u[..., o:o + n])
        o += n
    return outs


def fox_prompt(q, k, v, logf):
    b, S, H, hd = q.shape
    c = jnp.cumsum(logf, axis=1)
    c_k = jnp.transpose(c, (0, 2, 1))[:, :, None, :]
    kpos = jnp.arange(S)
    scale = HEAD_DIM ** -0.5

    def block(i):
        start = i * Q_BLOCK
        qi = lax.dynamic_slice_in_dim(q, start, Q_BLOCK, axis=1)
        ci = lax.dynamic_slice_in_dim(c, start, Q_BLOCK, axis=1)
        s = jnp.einsum('bqhd,bkhd->bhqk', qi, k, preferred_element_type=jnp.float32) * scale
        s = s + jnp.transpose(ci, (0, 2, 1))[..., None] - c_k
        qpos = start + jnp.arange(Q_BLOCK)
        s = jnp.where(qpos[:, None] >= kpos[None, :], s, -jnp.inf)
        pr = jax.nn.softmax(s, axis=-1).astype(v.dtype)
        return jnp.einsum('bhqk,bkhd->bqhd', pr, v)

    o = lax.map(block, jnp.arange(S // Q_BLOCK))
    return jnp.transpose(o, (1, 0, 2, 3, 4)).reshape(b, S, H * hd)


def fox_sample(q, k, v, logf, k_past, v_past, logf_past):
    b, T, H, hd = q.shape
    P = k_past.shape[1]
    c_past = jnp.cumsum(logf_past.astype(jnp.float32), axis=1)
    c_new = c_past[:, -1:] + jnp.cumsum(logf, axis=1)
    c_all = jnp.concatenate([c_past, c_new], axis=1)
    k_all = jnp.concatenate([k_past.astype(k.dtype), k], axis=1)
    v_all = jnp.concatenate([v_past.astype(v.dtype), v], axis=1)
    s = jnp.einsum('bqhd,bkhd->bhqk', q, k_all, preferred_element_type=jnp.float32) * (HEAD_DIM ** -0.5)
    s = s + jnp.transpose(c_new, (0, 2, 1))[..., None] - jnp.transpose(c_all, (0, 2, 1))[:, :, None, :]
    mask = (P + jnp.arange(T))[:, None] >= jnp.arange(P + T)[None, :]
    s = jnp.where(mask, s, -jnp.inf)
    pr = jax.nn.softmax(s, axis=-1).astype(v.dtype)
    return jnp.einsum('bhqk,bkhd->bqhd', pr, v_all).reshape(b, T, H * hd)


def ssd_chunk(xh, dt, A, Bh, Ch, h0):
    L = xh.shape[1]
    acum = jnp.cumsum(dt * A, axis=1)
    causal = jnp.tril(jnp.ones((L, L), dtype=bool))
    diff = acum[:, :, None, :] - acum[:, None, :, :]
    decay = jnp.exp(jnp.where(causal[None, :, :, None], diff, -jnp.inf))
    cb = jnp.einsum('bthn,bshn->btsh', Ch, Bh)
    y = jnp.einsum('btsh,bshp->bthp', cb * decay * dt[:, None, :, :], xh)
    y = y + jnp.einsum('bthn,bhpn->bthp', Ch, h0) * jnp.exp(acum)[..., None]
    w_end = jnp.exp(acum[:, -1:] - acum) * dt
    h_end = h0 * jnp.exp(acum[:, -1])[:, :, None, None] + jnp.einsum('bshn,bsh,bshp->bhpn', Bh, w_end, xh)
    return y, h_end


def ssd_scan_prompt(xh, dt, A, Bh, Ch):
    b, S = xh.shape[:2]
    nc = S // SSD_CHUNK

    def to_chunks(t):
        return jnp.moveaxis(t.reshape((b, nc, SSD_CHUNK) + t.shape[2:]), 1, 0)

    h0 = jnp.zeros((b, SSD_HEADS, SSD_HEADDIM, SSD_STATE), jnp.float32)

    def step(h, inp):
        xc, dtc, Bc, Cc = inp
        y, h = ssd_chunk(xc, dtc, A, Bc, Cc, h)
        return h, y

    h_end, ys = lax.scan(step, h0, (to_chunks(xh), to_chunks(dt), to_chunks(Bh), to_chunks(Ch)))
    return jnp.moveaxis(ys, 0, 1).reshape(b, S, SSD_HEADS, SSD_HEADDIM), h_end


def ssd_prepare(xbc, dt_raw, conv_state, p):
    y, new_cs = causal_dwconv(xbc, conv_state, p['ssd_conv_w'])
    y = jax.nn.silu(y + p['ssd_conv_b'])
    b, L, _ = y.shape
    gn = SSD_GROUPS * SSD_STATE
    xh = y[..., :SSD_DIM].reshape(b, L, SSD_HEADS, SSD_HEADDIM).astype(jnp.float32)
    Bm = y[..., SSD_DIM:SSD_DIM + gn].reshape(b, L, SSD_GROUPS, SSD_STATE)
    Cm = y[..., SSD_DIM + gn:].reshape(b, L, SSD_GROUPS, SSD_STATE)
    rep = SSD_HEADS // SSD_GROUPS
    Bh = jnp.repeat(Bm, rep, axis=2).astype(jnp.float32)
    Ch = jnp.repeat(Cm, rep, axis=2).astype(jnp.float32)
    dt = jax.nn.softplus(dt_raw.astype(jnp.float32) + p['ssd_dt_bias'].astype(jnp.float32))
    A = -jnp.exp(p['ssd_A_log'].astype(jnp.float32))
    return xh, dt, A, Bh, Ch, new_cs


def mem_kv(mem, p):
    b = mem.shape[0]
    m = rms_norm(mem, p['mem_norm'])
    km = rms_norm((m @ p['xa_wk']).reshape(b, MEM_LEN, MEM_HEADS, MEM_HEAD_DIM), p['xa_k_norm'])
    vm = (m @ p['xa_wv']).reshape(b, MEM_LEN, MEM_HEADS, MEM_HEAD_DIM)
    return km, vm


def cross_attn(x, km, vm, p):
    b, L, _ = x.shape
    h = rms_norm(x, p['xa_norm'])
    q = rms_norm((h @ p['xa_wq']).reshape(b, L, MEM_HEADS, MEM_HEAD_DIM), p['xa_q_norm'])
    s = jnp.einsum('bqhd,bkhd->bhqk', q, km.astype(q.dtype), preferred_element_type=jnp.float32) * (MEM_HEAD_DIM ** -0.5)
    pr = jax.nn.softmax(s, axis=-1).astype(x.dtype)
    o = jnp.einsum('bhqk,bkhd->bqhd', pr, vm.astype(x.dtype)).reshape(b, L, MEM_DIM)
    return x + o @ p['xa_wo']


def token_mix(x, p, conv_state, ssd_conv_state, fox_fn, ssd_fn):
    b, L, _ = x.shape
    h = rms_norm(x, p['mix_norm'])
    sc_b, sc_c, sc_h, q, k, v, f, z, xbc, dt_raw = split_proj(h @ p['w_in'])
    conv_y, new_conv = causal_dwconv(sc_c * sc_h, conv_state, p['sc_conv_w'])
    conv_y = sc_b * conv_y
    q = rms_norm(q.reshape(b, L, FOX_HEADS, HEAD_DIM), p['fox_q_norm'])
    k = rms_norm(k.reshape(b, L, FOX_HEADS, HEAD_DIM), p['fox_k_norm'])
    v = v.reshape(b, L, FOX_HEADS, HEAD_DIM)
    logf = jax.nn.log_sigmoid(f.astype(jnp.float32) + p['fox_f_bias'].astype(jnp.float32))
    attn_y = fox_fn(q, k, v, logf)
    xh, dt, A, Bh, Ch, new_ssd_conv = ssd_prepare(xbc, dt_raw, ssd_conv_state, p)
    y_ssd, new_h = ssd_fn(xh, dt, A, Bh, Ch)
    y_ssd = (y_ssd + p['ssd_D'].astype(jnp.float32)[:, None] * xh).reshape(b, L, SSD_DIM).astype(x.dtype)
    y_ssd = y_ssd * jax.nn.silu(z)
    cat = jnp.concatenate([conv_y, attn_y.astype(x.dtype), y_ssd], axis=-1)
    cat = rms_norm(cat.reshape(b, L, MIX_DIM // HEAD_DIM, HEAD_DIM),
                   p['mix_out_norm'].reshape(MIX_DIM // HEAD_DIM, HEAD_DIM)).reshape(b, L, MIX_DIM)
    return x + cat @ p['w_out'], (k, v, logf, new_conv, new_ssd_conv, new_h)


def trunk_layer(x, p, km, vm, conv_state, ssd_conv_state, fox_fn, ssd_fn):
    x = swiglu_half(x, p['ffn1_norm'], p['ffn1_wg'], p['ffn1_wu'], p['ffn1_wd'])
    x, st = token_mix(x, p, conv_state, ssd_conv_state, fox_fn, ssd_fn)
    x = cross_attn(x, km, vm, p)
    x = swiglu_half(x, p['ffn2_norm'], p['ffn2_wg'], p['ffn2_wu'], p['ffn2_wd'])
    return x, st


def setup_inputs(seed: int = 0) -> dict:
    key = jax.random.key(seed)
    ks = iter(jax.random.split(key, 64))
    f32 = jnp.float32

    def nrm(shape, scale):
        return jax.random.normal(next(ks), shape, f32) * scale

    def gain(shape):
        return 1.0 + nrm(shape, 0.02)

    n_pages = PAST_LEN // PAGE_SIZE
    n_used = DEC_BATCH * n_pages
    n_phys = n_used + max(1, n_used // 4)
    page_table = jax.random.permutation(next(ks), n_phys)[:n_used].reshape(DEC_BATCH, n_pages).astype(jnp.int32)

    inp = {}
    inp['x_prompt'] = nrm((BATCH, SEQ, D_MODEL), 1.0)
    inp['x_sample'] = nrm((DEC_BATCH, DEC_SEQ, D_MODEL), 1.0)
    inp['cache_fox_k'] = nrm((DEPTH, n_phys, PAGE_SIZE, FOX_HEADS, HEAD_DIM), 1.0)
    inp['cache_fox_v'] = nrm((DEPTH, n_phys, PAGE_SIZE, FOX_HEADS, HEAD_DIM), 1.0)
    inp['cache_fox_logf'] = jax.nn.log_sigmoid(4.0 + nrm((DEPTH, n_phys, PAGE_SIZE, FOX_HEADS), 1.0))
    inp['cache_mem_k'] = nrm((DEPTH, DEC_BATCH, MEM_LEN, MEM_HEADS, MEM_HEAD_DIM), 1.0)
    inp['cache_mem_v'] = nrm((DEPTH, DEC_BATCH, MEM_LEN, MEM_HEADS, MEM_HEAD_DIM), 1.0)
    inp['state_conv'] = nrm((DEPTH, DEC_BATCH, SC_WIDTH - 1, CONV_DIM), 0.5)
    inp['state_ssd_conv'] = nrm((DEPTH, DEC_BATCH, SSD_CONV - 1, XBC_DIM), 0.5)
    inp['state_ssd'] = nrm((DEPTH, DEC_BATCH, SSD_HEADS, SSD_HEADDIM, SSD_STATE), 0.1)
    inp['page_table'] = page_table
    inp['mem_prompt'] = nrm((BATCH, MEM_LEN, D_MODEL), 1.0)
    inp['ffn1_norm'] = gain((DEPTH, D_MODEL))
    inp['ffn1_wg'] = nrm((DEPTH, D_MODEL, FFN_DIM), D_MODEL ** -0.5)
    inp['ffn1_wu'] = nrm((DEPTH, D_MODEL, FFN_DIM), D_MODEL ** -0.5)
    inp['ffn1_wd'] = nrm((DEPTH, FFN_DIM, D_MODEL), FFN_DIM ** -0.5)
    inp['mix_norm'] = gain((DEPTH, D_MODEL))
    inp['w_in'] = nrm((DEPTH, D_MODEL, PROJ_DIM), D_MODEL ** -0.5)
    inp['sc_conv_w'] = nrm((DEPTH, SC_WIDTH, CONV_DIM), SC_WIDTH ** -0.5)
    inp['fox_q_norm'] = gain((DEPTH, HEAD_DIM))
    inp['fox_k_norm'] = gain((DEPTH, HEAD_DIM))
    inp['fox_f_bias'] = jax.random.uniform(next(ks), (DEPTH, FOX_HEADS), f32, 2.0, 6.0)
    inp['ssd_conv_w'] = nrm((DEPTH, SSD_CONV, XBC_DIM), SSD_CONV ** -0.5)
    inp['ssd_conv_b'] = nrm((DEPTH, XBC_DIM), 0.02)
    dt0 = jnp.exp(jax.random.uniform(next(ks), (DEPTH, SSD_HEADS), f32, math.log(1e-3), math.log(1e-1)))
    inp['ssd_dt_bias'] = dt0 + jnp.log(-jnp.expm1(-dt0))
    inp['ssd_A_log'] = jnp.log(jax.random.uniform(next(ks), (DEPTH, SSD_HEADS), f32, 1.0, 16.0))
    inp['ssd_D'] = gain((DEPTH, SSD_HEADS))
    inp['mix_out_norm'] = gain((DEPTH, MIX_DIM))
    inp['w_out'] = nrm((DEPTH, MIX_DIM, D_MODEL), MIX_DIM ** -0.5)
    inp['xa_norm'] = gain((DEPTH, D_MODEL))
    inp['mem_norm'] = gain((DEPTH, D_MODEL))
    inp['xa_wq'] = nrm((DEPTH, D_MODEL, MEM_DIM), D_MODEL ** -0.5)
    inp['xa_wk'] = nrm((DEPTH, D_MODEL, MEM_DIM), D_MODEL ** -0.5)
    inp['xa_wv'] = nrm((DEPTH, D_MODEL, MEM_DIM), D_MODEL ** -0.5)
    inp['xa_q_norm'] = gain((DEPTH, MEM_HEAD_DIM))
    inp['xa_k_norm'] = gain((DEPTH, MEM_HEAD_DIM))
    inp['xa_wo'] = nrm((DEPTH, MEM_DIM, D_MODEL), MEM_DIM ** -0.5)
    inp['ffn2_norm'] = gain((DEPTH, D_MODEL))
    inp['ffn2_wg'] = nrm((DEPTH, D_MODEL, FFN_DIM), D_MODEL ** -0.5)
    inp['ffn2_wu'] = nrm((DEPTH, D_MODEL, FFN_DIM), D_MODEL ** -0.5)
    inp['ffn2_wd'] = nrm((DEPTH, FFN_DIM, D_MODEL), FFN_DIM ** -0.5)
    inp['final_norm'] = gain((D_MODEL,))
    return inp


def reference(x_prompt, x_sample, cache_fox_k, cache_fox_v, cache_fox_logf, cache_mem_k, cache_mem_v,
              state_conv, state_ssd_conv, state_ssd, page_table, mem_prompt,
              ffn1_norm, ffn1_wg, ffn1_wu, ffn1_wd, mix_norm, w_in, sc_conv_w, fox_q_norm, fox_k_norm,
              fox_f_bias, ssd_conv_w, ssd_conv_b, ssd_dt_bias, ssd_A_log, ssd_D, mix_out_norm, w_out,
              xa_norm, mem_norm, xa_wq, xa_wk, xa_wv, xa_q_norm, xa_k_norm, xa_wo,
              ffn2_norm, ffn2_wg, ffn2_wu, ffn2_wd, final_norm):
    b_p = x_prompt.shape[0]
    b_s = x_sample.shape[0]
    n_pages = page_table.shape[1]
    past = n_pages * PAGE_SIZE
    xp = x_prompt
    xs = x_sample
    fk_p, fv_p, fl_p, mk_p, mv_p, cv_p, sc_p, ss_p = [], [], [], [], [], [], [], []
    fk_s, fv_s, fl_s, cv_s, sc_s, ss_s = [], [], [], [], [], []
    for l in range(DEPTH):
        p = {
            'ffn1_norm': ffn1_norm[l], 'ffn1_wg': ffn1_wg[l], 'ffn1_wu': ffn1_wu[l], 'ffn1_wd': ffn1_wd[l],
            'mix_norm': mix_norm[l], 'w_in': w_in[l], 'sc_conv_w': sc_conv_w[l],
            'fox_q_norm': fox_q_norm[l], 'fox_k_norm': fox_k_norm[l], 'fox_f_bias': fox_f_bias[l],
            'ssd_conv_w': ssd_conv_w[l], 'ssd_conv_b': ssd_conv_b[l], 'ssd_dt_bias': ssd_dt_bias[l],
            'ssd_A_log': ssd_A_log[l], 'ssd_D': ssd_D[l], 'mix_out_norm': mix_out_norm[l], 'w_out': w_out[l],
            'xa_norm': xa_norm[l], 'mem_norm': mem_norm[l], 'xa_wq': xa_wq[l], 'xa_wk': xa_wk[l],
            'xa_wv': xa_wv[l], 'xa_q_norm': xa_q_norm[l], 'xa_k_norm': xa_k_norm[l], 'xa_wo': xa_wo[l],
            'ffn2_norm': ffn2_norm[l], 'ffn2_wg': ffn2_wg[l], 'ffn2_wu': ffn2_wu[l], 'ffn2_wd': ffn2_wd[l],
        }
        km_p, vm_p = mem_kv(mem_prompt, p)
        zc = jnp.zeros((b_p, SC_WIDTH - 1, CONV_DIM), xp.dtype)
        zs = jnp.zeros((b_p, SSD_CONV - 1, XBC_DIM), xp.dtype)
        xp, st_p = trunk_layer(xp, p, km_p, vm_p, zc, zs, fox_prompt, ssd_scan_prompt)
        fk_p.append(st_p[0]); fv_p.append(st_p[1]); fl_p.append(st_p[2])
        mk_p.append(km_p); mv_p.append(vm_p)
        cv_p.append(st_p[3]); sc_p.append(st_p[4]); ss_p.append(st_p[5])
        k_past = cache_fox_k[l][page_table].reshape(b_s, past, FOX_HEADS, HEAD_DIM)
        v_past = cache_fox_v[l][page_table].reshape(b_s, past, FOX_HEADS, HEAD_DIM)
        lf_past = cache_fox_logf[l][page_table].reshape(b_s, past, FOX_HEADS)
        fox_fn = functools.partial(fox_sample, k_past=k_past, v_past=v_past, logf_past=lf_past)
        ssd_fn = functools.partial(ssd_chunk, h0=state_ssd[l].astype(jnp.float32))
        xs, st_s = trunk_layer(xs, p, cache_mem_k[l], cache_mem_v[l], state_conv[l], state_ssd_conv[l], fox_fn, ssd_fn)
        fk_s.append(st_s[0]); fv_s.append(st_s[1]); fl_s.append(st_s[2])
        cv_s.append(st_s[3]); sc_s.append(st_s[4]); ss_s.append(st_s[5])
    y_prompt = rms_norm(xp, final_norm)
    y_sample = rms_norm(xs, final_norm)
    return (y_prompt, y_sample,
            jnp.stack(fk_p), jnp.stack(fv_p), jnp.stack(fl_p), jnp.stack(mk_p), jnp.stack(mv_p),
            jnp.stack(cv_p), jnp.stack(sc_p), jnp.stack(ss_p),
            jnp.stack(fk_s), jnp.stack(fv_s), jnp.stack(fl_s),
            jnp.stack(cv_s), jnp.stack(sc_s), jnp.stack(ss_s))
```

```python
import functools
import math

import jax
import jax.numpy as jnp
from jax import lax
from jax.experimental import pallas as pl
from jax.experimental.pallas import tpu as pltpu

F32 = jnp.float32
BF16 = jnp.bfloat16

D_MODEL = 1024
DEPTH = 4
PAGE_SIZE = 128
HEAD_DIM = 64
CONV_DIM = 256
ATTN_DIM = 512
SSD_DIM = 256
FOX_HEADS = 8
SC_WIDTH = 3
SSD_HEADS = 4
SSD_STATE = 64
SSD_CONV = 4
SSD_CHUNK = 128
XBC_DIM = 512
FFN_DIM = 2816
MEM_LEN = 256
MEM_HEADS = 4
MEM_HEAD_DIM = 128
MEM_DIM = 512
RMS_EPS = 1e-6

LANES = 128
SUBLANES = 8
VMEM_LIMIT = 56 * 1024 * 1024
ROW_TILE = 512
PAGES_PER_STEP = 8
PROJ_MAIN = 3 * CONV_DIM + 3 * ATTN_DIM + SSD_DIM + XBC_DIM
PROJ_PAD = PROJ_MAIN + LANES
F_LANE0 = 0
DT_LANE0 = FOX_HEADS


def _params(sem):
    return pltpu.CompilerParams(dimension_semantics=sem, vmem_limit_bytes=VMEM_LIMIT)


def _resident(shape):
    nd = len(shape)
    return pl.BlockSpec(shape, lambda *_: (0,) * nd, pipeline_mode=pl.Buffered(1))


def _rms(x, g):
    ms = jnp.mean(x * x, axis=-1, keepdims=True)
    return x * lax.rsqrt(ms + RMS_EPS) * g


def _head_rms(x, g, head_dim):
    width = x.shape[-1]
    pieces = []
    for c in range(width // LANES):
        blk = x[:, c * LANES:(c + 1) * LANES]
        sq = blk * blk
        s_all = jnp.sum(sq, axis=-1, keepdims=True)
        if head_dim == LANES:
            ms = s_all * (1.0 / LANES)
        else:
            lo = lax.broadcasted_iota(jnp.int32, sq.shape, 1) < head_dim
            s_lo = jnp.sum(jnp.where(lo, sq, 0.0), axis=-1, keepdims=True)
            ms = jnp.where(lo, s_lo, s_all - s_lo) * (1.0 / head_dim)
        pieces.append(blk * lax.rsqrt(ms + RMS_EPS))
    return jnp.concatenate(pieces, axis=-1) * g


def _silu(x):
    return x * jax.nn.sigmoid(x)


def _softplus(x):
    return jnp.maximum(x, 0.0) + jnp.log1p(jnp.exp(-jnp.abs(x)))


def _split3(x):
    hi = x.astype(BF16)
    r1 = x - hi.astype(F32)
    mid = r1.astype(BF16)
    lo = (r1 - mid.astype(F32)).astype(BF16)
    return hi, mid, lo


def _dot_exact_rhs(x, m_bf16):
    hi, mid, lo = _split3(x)
    out = jnp.dot(hi, m_bf16, preferred_element_type=F32)
    out = out + jnp.dot(mid, m_bf16, preferred_element_type=F32)
    return out + jnp.dot(lo, m_bf16, preferred_element_type=F32)


def _dot_exact_lhs(m_bf16, x):
    hi, mid, lo = _split3(x)
    out = jnp.dot(m_bf16, hi, preferred_element_type=F32)
    out = out + jnp.dot(m_bf16, mid, preferred_element_type=F32)
    return out + jnp.dot(m_bf16, lo, preferred_element_type=F32)


def _lower_tri(n):
    r = lax.broadcasted_iota(jnp.int32, (n, n), 0)
    c = lax.broadcasted_iota(jnp.int32, (n, n), 1)
    return (c <= r).astype(BF16)


def _upper_tri(n):
    r = lax.broadcasted_iota(jnp.int32, (n, n), 0)
    c = lax.broadcasted_iota(jnp.int32, (n, n), 1)
    return (r <= c).astype(BF16)


def _nt(a, b):
    return lax.dot_general(a, b, (((1,), (1,)), ((), ())), preferred_element_type=F32)


def _tn(a, b):
    return lax.dot_general(a, b, (((0,), (0,)), ((), ())), preferred_element_type=F32)


def _ffn_kernel(x_ref, g_ref, wg_ref, wu_ref, wd_ref, fg_ref, o_ref, *, final):
    x = x_ref[...]
    h = _rms(x, g_ref[...]).astype(BF16)
    a = jnp.dot(h, wg_ref[...], preferred_element_type=F32)
    b = jnp.dot(h, wu_ref[...], preferred_element_type=F32)
    t = (_silu(a) * b).astype(BF16)
    out = x + 0.5 * jnp.dot(t, wd_ref[...], preferred_element_type=F32)
    if final:
        out = _rms(out, fg_ref[...])
    o_ref[...] = out


def _ffn(x, g, wg, wu, wd, fg, final):
    n = x.shape[0]
    tm = min(ROW_TILE // 2, n)
    return pl.pallas_call(
        functools.partial(_ffn_kernel, final=final),
        out_shape=jax.ShapeDtypeStruct((n, D_MODEL), F32),
        grid=(n // tm,),
        in_specs=[
            pl.BlockSpec((tm, D_MODEL), lambda i: (i, 0)),
            _resident((1, D_MODEL)),
            _resident((D_MODEL, FFN_DIM)),
            _resident((D_MODEL, FFN_DIM)),
            _resident((FFN_DIM, D_MODEL)),
            _resident((1, D_MODEL)),
        ],
        out_specs=pl.BlockSpec((tm, D_MODEL), lambda i: (i, 0)),
        compiler_params=_params(("arbitrary",)),
        name="ffn",
    )(x, g, wg, wu, wd, fg)


def _inproj_kernel(x_ref, g_ref, w_ref, qg_ref, kg_ref, fb_ref,
                   scb_ref, cin_ref, qbf_ref, kn_ref, kbf_ref, v_ref, vbf_ref,
                   z_ref, xbc_ref, fdt_ref):
    h = _rms(x_ref[...], g_ref[...]).astype(BF16)

    def proj(a, b):
        return jnp.dot(h, w_ref[:, a:b], preferred_element_type=F32)

    o = 0
    scb_ref[...] = proj(o, o + CONV_DIM)
    o += CONV_DIM
    cin_ref[...] = proj(o, o + CONV_DIM) * proj(o + CONV_DIM, o + 2 * CONV_DIM)
    o += 2 * CONV_DIM
    q = _head_rms(proj(o, o + ATTN_DIM), qg_ref[...], HEAD_DIM)
    qbf_ref[...] = (q * (HEAD_DIM ** -0.5)).astype(BF16)
    o += ATTN_DIM
    k = _head_rms(proj(o, o + ATTN_DIM), kg_ref[...], HEAD_DIM)
    kn_ref[...] = k
    kbf_ref[...] = k.astype(BF16)
    o += ATTN_DIM
    v = proj(o, o + ATTN_DIM)
    v_ref[...] = v
    vbf_ref[...] = v.astype(BF16)
    o += ATTN_DIM
    z_ref[...] = proj(o, o + SSD_DIM)
    o += SSD_DIM
    xbc_ref[...] = proj(o, o + XBC_DIM)
    o += XBC_DIM
    u = proj(o, o + LANES) + fb_ref[...]
    lane = lax.broadcasted_iota(jnp.int32, u.shape, 1)
    fdt_ref[...] = jnp.where(lane < DT_LANE0, -_softplus(-u), _softplus(u))


def _inproj(x, g, w, qg, kg, fb):
    n = x.shape[0]
    tm = min(ROW_TILE // 2, n)
    widths = [(CONV_DIM, F32), (CONV_DIM, F32), (ATTN_DIM, BF16), (ATTN_DIM, F32), (ATTN_DIM, BF16),
              (ATTN_DIM, F32), (ATTN_DIM, BF16), (SSD_DIM, F32), (XBC_DIM, F32), (LANES, F32)]
    return pl.pallas_call(
        _inproj_kernel,
        out_shape=[jax.ShapeDtypeStruct((n, w_), dt) for w_, dt in widths],
        grid=(n // tm,),
        in_specs=[
            pl.BlockSpec((tm, D_MODEL), lambda i: (i, 0)),
            _resident((1, D_MODEL)),
            _resident((D_MODEL, PROJ_PAD)),
            _resident((1, ATTN_DIM)),
            _resident((1, ATTN_DIM)),
            _resident((1, LANES)),
        ],
        out_specs=[pl.BlockSpec((tm, w_), lambda i: (i, 0)) for w_, _ in widths],
        compiler_params=_params(("arbitrary",)),
        name="inproj",
    )(x, g, w, qg, kg, fb)


def _cumsum_kernel(f_ref, o_ref, carry_ref, *, tl):
    @pl.when(pl.program_id(1) == 0)
    def _():
        carry_ref[...] = jnp.zeros_like(carry_ref)

    c = _dot_exact_lhs(_lower_tri(tl), f_ref[...]) + carry_ref[0:1, :]
    carry_ref[...] = jnp.broadcast_to(c[tl - 1:tl, :], carry_ref.shape)
    for hp in range(FOX_HEADS // 2):
        o_ref[hp] = c if hp == 0 else pltpu.roll(c, LANES - 2 * hp, axis=1)


def _cumsum_pairs(fdt, nseq, seq):
    tl = min(ROW_TILE, seq)
    nt = seq // tl
    return pl.pallas_call(
        functools.partial(_cumsum_kernel, tl=tl),
        out_shape=jax.ShapeDtypeStruct((FOX_HEADS // 2, nseq * seq, LANES), F32),
        grid=(nseq, nt),
        in_specs=[pl.BlockSpec((tl, LANES), lambda b, j: (b * nt + j, 0))],
        out_specs=pl.BlockSpec((FOX_HEADS // 2, tl, LANES), lambda b, j: (0, b * nt + j, 0)),
        scratch_shapes=[pltpu.VMEM((SUBLANES, LANES), F32)],
        compiler_params=_params(("arbitrary", "arbitrary")),
        name="logf_cumsum",
    )(fdt)


def _fox_kernel(q_ref, k_ref, v_ref, c_ref, ct_ref, o_ref, m_ref, l_ref, acc_ref, *, tq):
    i = pl.program_id(2)
    j = pl.program_id(3)

    @pl.when(j == 0)
    def _():
        m_ref[...] = jnp.full(m_ref.shape, -jnp.inf, F32)
        l_ref[...] = jnp.zeros_like(l_ref)
        acc_ref[...] = jnp.zeros_like(acc_ref)

    def step(masked):
        for hh in range(2):
            sl = slice(hh * HEAD_DIM, (hh + 1) * HEAD_DIM)
            s = _nt(q_ref[:, sl], k_ref[:, sl])
            s = s + c_ref[:, hh:hh + 1] - ct_ref[hh:hh + 1, :]
            if masked:
                row = lax.broadcasted_iota(jnp.int32, s.shape, 0)
                col = lax.broadcasted_iota(jnp.int32, s.shape, 1)
                s = jnp.where(col <= row, s, -jnp.inf)
            m_old = m_ref[hh]
            m_new = jnp.maximum(m_old, jnp.max(s, axis=-1, keepdims=True))
            alpha = jnp.exp(m_old - m_new)
            p = jnp.exp(s - m_new)
            l_ref[hh] = alpha * l_ref[hh] + jnp.sum(p, axis=-1, keepdims=True)
            acc_ref[hh] = alpha * acc_ref[hh] + jnp.dot(p.astype(BF16), v_ref[:, sl],
                                                        preferred_element_type=F32)
            m_ref[hh] = m_new

    @pl.when(j < i)
    def _():
        step(False)

    @pl.when(j == i)
    def _():
        step(True)
        o_ref[...] = jnp.concatenate([acc_ref[0] / l_ref[0], acc_ref[1] / l_ref[1]], axis=-1)


def _fox_prompt(qbf, kbf, vbf, c4, ctp, nseq, seq):
    tq = min(ROW_TILE, seq)
    nq = seq // tq
    hp_n = FOX_HEADS // 2
    return pl.pallas_call(
        functools.partial(_fox_kernel, tq=tq),
        out_shape=jax.ShapeDtypeStruct((nseq * seq, ATTN_DIM), F32),
        grid=(nseq, hp_n, nq, nq),
        in_specs=[
            pl.BlockSpec((tq, LANES), lambda b, h, i, j: (b * nq + i, h)),
            pl.BlockSpec((tq, LANES), lambda b, h, i, j: (b * nq + jnp.minimum(i, j), h)),
            pl.BlockSpec((tq, LANES), lambda b, h, i, j: (b * nq + jnp.minimum(i, j), h)),
            pl.BlockSpec((None, tq, LANES), lambda b, h, i, j: (h, b * nq + i, 0)),
            pl.BlockSpec((None, None, 2, tq), lambda b, h, i, j: (b, h, 0, jnp.minimum(i, j))),
        ],
        out_specs=pl.BlockSpec((tq, LANES), lambda b, h, i, j: (b * nq + i, h)),
        scratch_shapes=[pltpu.VMEM((2, tq, 1), F32), pltpu.VMEM((2, tq, 1), F32),
                        pltpu.VMEM((2, tq, HEAD_DIM), F32)],
        compiler_params=_params(("arbitrary",) * 4),
        name="fox_prompt",
    )(qbf, kbf, vbf, c4, ctp)


def _fox_paged_kernel(pt_ref, q_ref, kn_ref, vn_ref, lfn_ref, *rest, tokens):
    npg = PAGES_PER_STEP
    k_refs = rest[:npg]
    v_refs = rest[npg:2 * npg]
    lf_refs = rest[2 * npg:3 * npg]
    o_ref, m_ref, l_ref, acc_ref, carry_ref = rest[3 * npg:]
    g = pl.program_id(1)
    rows = FOX_HEADS * tokens

    @pl.when(g == 0)
    def _():
        m_ref[...] = jnp.full(m_ref.shape, -jnp.inf, F32)
        l_ref[...] = jnp.zeros_like(l_ref)
        acc_ref[...] = jnp.zeros_like(acc_ref)
        carry_ref[...] = jnp.zeros_like(carry_ref)

    q = q_ref[...]
    tri = _upper_tri(PAGE_SIZE)

    def page(k, v, lf_t, masked):
        c = _dot_exact_rhs(lf_t, tri) + carry_ref[:, 0:1]
        carry_ref[...] = jnp.broadcast_to(c[:, PAGE_SIZE - 1:PAGE_SIZE], carry_ref.shape)
        c_rows = jnp.broadcast_to(c[:, None, :], (FOX_HEADS, tokens, PAGE_SIZE)).reshape(rows, PAGE_SIZE)
        s = _nt(q, k.astype(BF16)) - c_rows
        if masked:
            t_idx = lax.broadcasted_iota(jnp.int32, (FOX_HEADS, tokens, PAGE_SIZE), 1).reshape(rows, PAGE_SIZE)
            col = lax.broadcasted_iota(jnp.int32, s.shape, 1)
            s = jnp.where(col <= t_idx, s, -jnp.inf)
        m_old = m_ref[...]
        m_new = jnp.maximum(m_old, jnp.max(s, axis=-1, keepdims=True))
        alpha = jnp.exp(m_old - m_new)
        p = jnp.exp(s - m_new)
        l_ref[...] = alpha * l_ref[...] + jnp.sum(p, axis=-1, keepdims=True)
        acc_ref[...] = alpha * acc_ref[...] + jnp.dot(p.astype(BF16), v.astype(BF16),
                                                      preferred_element_type=F32)
        m_ref[...] = m_new

    for n in range(npg):
        page(k_refs[n][...], v_refs[n][...], lf_refs[n][...], False)

    @pl.when(g == pl.num_programs(1) - 1)
    def _():
        page(kn_ref[...], vn_ref[...], lfn_ref[...], True)
        full = acc_ref[...] / l_ref[...]
        lane_head = lax.broadcasted_iota(jnp.int32, (tokens, ATTN_DIM), 1) // HEAD_DIM
        out = jnp.zeros((tokens, ATTN_DIM), F32)
        for h in range(FOX_HEADS):
            out = out + jnp.where(lane_head == h, full[h * tokens:(h + 1) * tokens, :], 0.0)
        o_ref[...] = out


def _fox_paged(layer, page_table, qbd, kn_pad, vn_pad, lfn_t, cache_k, cache_v, cache_lf_t, nseq, tokens):
    npg = PAGES_PER_STEP
    n_groups = page_table.shape[1] // npg
    rows = FOX_HEADS * tokens

    def page_spec(width2, width1, n):
        return pl.BlockSpec((None, None, width2, width1),
                            lambda b, g, pt: (layer, pt[b, g * npg + n], 0, 0))

    in_specs = [
        pl.BlockSpec((None, rows, ATTN_DIM), lambda b, g, pt: (b, 0, 0)),
        pl.BlockSpec((None, PAGE_SIZE, ATTN_DIM), lambda b, g, pt: (b, 0, 0)),
        pl.BlockSpec((None, PAGE_SIZE, ATTN_DIM), lambda b, g, pt: (b, 0, 0)),
        pl.BlockSpec((None, FOX_HEADS, PAGE_SIZE), lambda b, g, pt: (b, 0, 0)),
    ]
    in_specs += [page_spec(PAGE_SIZE, ATTN_DIM, n) for n in range(npg)]
    in_specs += [page_spec(PAGE_SIZE, ATTN_DIM, n) for n in range(npg)]
    in_specs += [page_spec(FOX_HEADS, PAGE_SIZE, n) for n in range(npg)]
    grid_spec = pltpu.PrefetchScalarGridSpec(
        num_scalar_prefetch=1,
        grid=(nseq, n_groups),
        in_specs=in_specs,
        out_specs=pl.BlockSpec((None, tokens, ATTN_DIM), lambda b, g, pt: (b, 0, 0)),
        scratch_shapes=[pltpu.VMEM((rows, 1), F32), pltpu.VMEM((rows, 1), F32),
                        pltpu.VMEM((rows, ATTN_DIM), F32), pltpu.VMEM((FOX_HEADS, LANES), F32)],
    )
    return pl.pallas_call(
        functools.partial(_fox_paged_kernel, tokens=tokens),
        out_shape=jax.ShapeDtypeStruct((nseq, tokens, ATTN_DIM), F32),
        grid_spec=grid_spec,
        compiler_params=_params(("arbitrary", "arbitrary")),
        name="fox_paged",
    )(page_table, qbd, kn_pad, vn_pad, lfn_t,
      *([cache_k] * npg), *([cache_v] * npg), *([cache_lf_t] * npg))


def _ssd_kernel(xbc_ref, halo_ref, st_ref, fdt_ref, dtt_ref, z_ref, h0_ref,
                cw_ref, cb_ref, av_ref, ac_ref, dv_ref,
                y_ref, hout_ref, buf_ref, h_ref, *, lc):
    j = pl.program_id(1)

    @pl.when(j == 0)
    def _():
        h_ref[...] = h0_ref[...]
        buf_ref[0:SUBLANES, :] = st_ref[...]

    @pl.when(j > 0)
    def _():
        buf_ref[0:SUBLANES, :] = halo_ref[...]

    buf_ref[SUBLANES:SUBLANES + lc, :] = xbc_ref[...]
    base = SUBLANES - (SSD_CONV - 1)
    u = buf_ref[base:base + lc, :] * cw_ref[0:1, :]
    for k in range(1, SSD_CONV):
        u = u + buf_ref[base + k:base + k + lc, :] * cw_ref[k:k + 1, :]
    u = _silu(u + cb_ref[...])

    fdt = fdt_ref[...]
    acum = _dot_exact_lhs(_lower_tri(lc), fdt * av_ref[...])
    dtt = dtt_ref[...]
    acum_t = _dot_exact_rhs(dtt * ac_ref[...], _upper_tri(lc))
    row = lax.broadcasted_iota(jnp.int32, (lc, lc), 0)
    col = lax.broadcasted_iota(jnp.int32, (lc, lc), 1)
    causal = col <= row
    z = z_ref[...]
    dvec = dv_ref[...]

    outs = []
    for h in range(SSD_HEADS):
        grp = h // (SSD_HEADS // 2)
        xh = u[:, h * HEAD_DIM:(h + 1) * HEAD_DIM]
        bh = u[:, SSD_DIM + grp * SSD_STATE:SSD_DIM + (grp + 1) * SSD_STATE]
        ch = u[:, SSD_DIM + 2 * SSD_STATE + grp * SSD_STATE:SSD_DIM + 2 * SSD_STATE + (grp + 1) * SSD_STATE]
        lane = DT_LANE0 + h
        a_col = acum[:, lane:lane + 1]
        dt_col = fdt[:, lane:lane + 1]
        a_row = acum_t[h:h + 1, :]
        dt_row = dtt[h:h + 1, :]
        a_last = acum[lc - 1:lc, lane:lane + 1]
        decay = jnp.exp(jnp.where(causal, a_col - a_row, -jnp.inf))
        xb = xh.astype(BF16)
        bb = bh.astype(BF16)
        cb16 = ch.astype(BF16)
        w = _nt(cb16, bb) * decay * dt_row
        h_prev = h_ref[h]
        y = jnp.dot(w.astype(BF16), xb, preferred_element_type=F32)
        y = y + _nt(cb16, h_prev.astype(BF16)) * jnp.exp(a_col)
        w_end = jnp.exp(a_last - a_col) * dt_col
        h_ref[h] = h_prev * jnp.exp(a_last) + _tn((xh * w_end).astype(BF16), bb)
        sl = slice(h * HEAD_DIM, (h + 1) * HEAD_DIM)
        outs.append((y + dvec[:, sl] * xh) * _silu(z[:, sl]))
    y_ref[...] = jnp.concatenate(outs, axis=-1)

    @pl.when(j == pl.num_programs(1) - 1)
    def _():
        hout_ref[...] = h_ref[...]


def _ssd(xbc, st_pad, fdt, dtt, z, h0, cw, cb, av, ac, dv, nseq, seq):
    lc = min(SSD_CHUNK, seq)
    nc = seq // lc
    hb = lc // SUBLANES
    return pl.pallas_call(
        functools.partial(_ssd_kernel, lc=lc),
        out_shape=[jax.ShapeDtypeStruct((nseq * seq, SSD_DIM), F32),
                   jax.ShapeDtypeStruct((nseq, SSD_HEADS, HEAD_DIM, SSD_STATE), F32)],
        grid=(nseq, nc),
        in_specs=[
            pl.BlockSpec((lc, XBC_DIM), lambda b, j: (b * nc + j, 0)),
            pl.BlockSpec((SUBLANES, XBC_DIM), lambda b, j: (jnp.maximum((b * nc + j) * hb - 1, 0), 0)),
            pl.BlockSpec((None, SUBLANES, XBC_DIM), lambda b, j: (b, 0, 0)),
            pl.BlockSpec((lc, LANES), lambda b, j: (b * nc + j, 0)),
            pl.BlockSpec((None, SSD_HEADS, lc), lambda b, j: (b, 0, j)),
            pl.BlockSpec((lc, SSD_DIM), lambda b, j: (b * nc + j, 0)),
            pl.BlockSpec((None, SSD_HEADS, HEAD_DIM, SSD_STATE), lambda b, j: (b, 0, 0, 0)),
            _resident((SSD_CONV, XBC_DIM)),
            _resident((1, XBC_DIM)),
            _resident((1, LANES)),
            _resident((SSD_HEADS, 1)),
            _resident((1, SSD_DIM)),
        ],
        out_specs=[pl.BlockSpec((lc, SSD_DIM), lambda b, j: (b * nc + j, 0)),
                   pl.BlockSpec((None, SSD_HEADS, HEAD_DIM, SSD_STATE), lambda b, j: (b, 0, 0, 0))],
        scratch_shapes=[pltpu.VMEM((SUBLANES + lc, XBC_DIM), F32),
                        pltpu.VMEM((SSD_HEADS, HEAD_DIM, SSD_STATE), F32)],
        compiler_params=_params(("arbitrary", "arbitrary")),
        name="ssd",
    )(xbc, xbc, st_pad, fdt, dtt, z, h0, cw, cb, av, ac, dv)


def _merge_kernel(x_ref, scb_ref, cin_ref, halo_ref, st_ref, attn_ref, yssd_ref,
                  cw_ref, g_ref, w_ref, o_ref, buf_ref, *, tl):
    j = pl.program_id(1)

    @pl.when(j == 0)
    def _():
        buf_ref[0:SUBLANES, :] = st_ref[...]

    @pl.when(j > 0)
    def _():
        buf_ref[0:SUBLANES, :] = halo_ref[...]

    buf_ref[SUBLANES:SUBLANES + tl, :] = cin_ref[...]
    base = SUBLANES - (SC_WIDTH - 1)
    y = buf_ref[base:base + tl, :] * cw_ref[0:1, :]
    for k in range(1, SC_WIDTH):
        y = y + buf_ref[base + k:base + k + tl, :] * cw_ref[k:k + 1, :]
    cat = jnp.concatenate([scb_ref[...] * y, attn_ref[...], yssd_ref[...]], axis=-1)
    cat = _head_rms(cat, g_ref[...], HEAD_DIM).astype(BF16)
    o_ref[...] = x_ref[...] + jnp.dot(cat, w_ref[...], preferred_element_type=F32)


def _merge(x, scb, cin, st_pad, attn, yssd, cw, g, w, nseq, seq):
    tl = min(ROW_TILE, seq)
    nt = seq // tl
    hb = tl // SUBLANES

    def rows(width):
        return pl.BlockSpec((tl, width), lambda b, j: (b * nt + j, 0))

    return pl.pallas_call(
        functools.partial(_merge_kernel, tl=tl),
        out_shape=jax.ShapeDtypeStruct((nseq * seq, D_MODEL), F32),
        grid=(nseq, nt),
        in_specs=[
            rows(D_MODEL), rows(CONV_DIM), rows(CONV_DIM),
            pl.BlockSpec((SUBLANES, CONV_DIM), lambda b, j: (jnp.maximum((b * nt + j) * hb - 1, 0), 0)),
            pl.BlockSpec((None, SUBLANES, CONV_DIM), lambda b, j: (b, 0, 0)),
            rows(ATTN_DIM), rows(SSD_DIM),
            _resident((SC_WIDTH, CONV_DIM)),
            _resident((1, D_MODEL)),
            _resident((D_MODEL, D_MODEL)),
        ],
        out_specs=rows(D_MODEL),
        scratch_shapes=[pltpu.VMEM((SUBLANES + tl, CONV_DIM), F32)],
        compiler_params=_params(("arbitrary", "arbitrary")),
        name="merge",
    )(x, scb, cin, cin, st_pad, attn, yssd, cw, g, w)


def _memkv_kernel(mem_ref, g_ref, wk_ref, wv_ref, kg_ref, km_ref, vm_ref):
    m = _rms(mem_ref[...], g_ref[...]).astype(BF16)
    km_ref[...] = _head_rms(jnp.dot(m, wk_ref[...], preferred_element_type=F32), kg_ref[...], MEM_HEAD_DIM)
    vm_ref[...] = jnp.dot(m, wv_ref[...], preferred_element_type=F32)


def _memkv(mem, g, wk, wv, kg):
    nb = mem.shape[0]
    out = jax.ShapeDtypeStruct((DEPTH, nb, MEM_LEN, MEM_DIM), F32)
    return pl.pallas_call(
        _memkv_kernel,
        out_shape=[out, out],
        grid=(DEPTH, nb),
        in_specs=[
            pl.BlockSpec((None, MEM_LEN, D_MODEL), lambda l, b: (b, 0, 0)),
            pl.BlockSpec((None, 1, D_MODEL), lambda l, b: (l, 0, 0)),
            pl.BlockSpec((None, D_MODEL, MEM_DIM), lambda l, b: (l, 0, 0)),
            pl.BlockSpec((None, D_MODEL, MEM_DIM), lambda l, b: (l, 0, 0)),
            pl.BlockSpec((None, 1, MEM_DIM), lambda l, b: (l, 0, 0)),
        ],
        out_specs=[pl.BlockSpec((None, None, MEM_LEN, MEM_DIM), lambda l, b: (l, b, 0, 0))] * 2,
        compiler_params=_params(("arbitrary", "arbitrary")),
        name="mem_kv",
    )(mem, g, wk, wv, kg)


def _xattn_kernel(x_ref, km_ref, vm_ref, g_ref, wq_ref, qg_ref, wo_ref, o_ref):
    x = x_ref[...]
    h = _rms(x, g_ref[...]).astype(BF16)
    q = _head_rms(jnp.dot(h, wq_ref[...], preferred_element_type=F32), qg_ref[...], MEM_HEAD_DIM)
    q = (q * (MEM_HEAD_DIM ** -0.5)).astype(BF16)
    km = km_ref[...].astype(BF16)
    vm = vm_ref[...].astype(BF16)
    outs = []
    for hh in range(MEM_HEADS):
        sl = slice(hh * MEM_HEAD_DIM, (hh + 1) * MEM_HEAD_DIM)
        s = _nt(q[:, sl], km[:, sl])
        p = jnp.exp(s - jnp.max(s, axis=-1, keepdims=True))
        p = p / jnp.sum(p, axis=-1, keepdims=True)
        outs.append(jnp.dot(p.astype(BF16), vm[:, sl], preferred_element_type=F32))
    o = jnp.concatenate(outs, axis=-1).astype(BF16)
    o_ref[...] = x + jnp.dot(o, wo_ref[...], preferred_element_type=F32)


def _xattn(x, km, vm, layer, g, wq, qg, wo, nseq, seq):
    tl = min(ROW_TILE, seq)
    nt = seq // tl
    return pl.pallas_call(
        _xattn_kernel,
        out_shape=jax.ShapeDtypeStruct((nseq * seq, D_MODEL), F32),
        grid=(nseq, nt),
        in_specs=[
            pl.BlockSpec((tl, D_MODEL), lambda b, j: (b * nt + j, 0)),
            pl.BlockSpec((None, None, MEM_LEN, MEM_DIM), lambda b, j: (layer, b, 0, 0)),
            pl.BlockSpec((None, None, MEM_LEN, MEM_DIM), lambda b, j: (layer, b, 0, 0)),
            _resident((1, D_MODEL)),
            _resident((D_MODEL, MEM_DIM)),
            _resident((1, MEM_DIM)),
            _resident((MEM_DIM, D_MODEL)),
        ],
        out_specs=pl.BlockSpec((tl, D_MODEL), lambda b, j: (b * nt + j, 0)),
        compiler_params=_params(("arbitrary", "arbitrary")),
        name="xattn",
    )(x, km, vm, g, wq, qg, wo)


def _pad_state(state, width):
    nseq, _, c = state.shape
    return jnp.concatenate([jnp.zeros((nseq, SUBLANES - (width - 1), c), F32), state], axis=1)


def _layer_params(l, p):
    w_in = p['w_in'][l]
    n_f0 = 3 * CONV_DIM + 3 * ATTN_DIM
    n_z0 = n_f0 + FOX_HEADS
    n_dt0 = n_z0 + SSD_DIM + XBC_DIM
    w_in_r = jnp.concatenate(
        [w_in[:, :n_f0], w_in[:, n_z0:n_dt0], w_in[:, n_f0:n_z0], w_in[:, n_dt0:],
         jnp.zeros((D_MODEL, PROJ_PAD - PROJ_MAIN - FOX_HEADS - SSD_HEADS), F32)], axis=1).astype(BF16)
    pad = jnp.zeros((LANES - FOX_HEADS - SSD_HEADS,), F32)
    a_neg = -jnp.exp(p['ssd_A_log'][l])
    return dict(
        w_in=w_in_r,
        fb=jnp.concatenate([p['fox_f_bias'][l], p['ssd_dt_bias'][l], pad])[None, :],
        qg=jnp.tile(p['fox_q_norm'][l], FOX_HEADS)[None, :],
        kg=jnp.tile(p['fox_k_norm'][l], FOX_HEADS)[None, :],
        av=jnp.concatenate([jnp.zeros((FOX_HEADS,), F32), a_neg, pad])[None, :],
        ac=a_neg[:, None],
        dv=jnp.repeat(p['ssd_D'][l], HEAD_DIM)[None, :],
        xa_qg=jnp.tile(p['xa_q_norm'][l], MEM_HEADS)[None, :],
    )


def _token_mix(l, x, p, lp, nseq, seq, conv_state, ssd_conv_state, ssd_state, paged):
    n = nseq * seq
    scb, cin, qbf, kn, kbf, v, vbf, z, xbc, fdt = _inproj(
        x, p['mix_norm'][l][None, :], lp['w_in'], lp['qg'], lp['kg'], lp['fb'])
    logf = fdt[:, F_LANE0:F_LANE0 + FOX_HEADS].reshape(nseq, seq, FOX_HEADS)
    dt = fdt[:, DT_LANE0:DT_LANE0 + SSD_HEADS].reshape(nseq, seq, SSD_HEADS)

    if paged is None:
        c4 = _cumsum_pairs(fdt, nseq, seq)
        ctp = jnp.transpose(c4[0][:, :FOX_HEADS].reshape(nseq, seq, FOX_HEADS // 2, 2), (0, 2, 3, 1))
        attn = _fox_prompt(qbf, kbf, vbf, c4, ctp, nseq, seq)
    else:
        page_table, cache_k, cache_v, cache_lf_t = paged
        eye = jnp.eye(FOX_HEADS, dtype=BF16)
        q4 = jnp.transpose(qbf.reshape(nseq, seq, FOX_HEADS, HEAD_DIM), (0, 2, 1, 3))
        qbd = (q4[:, :, :, None, :] * eye[None, :, None, :, None]).reshape(nseq, FOX_HEADS * seq, ATTN_DIM)
        zpad = jnp.zeros((nseq, PAGE_SIZE - seq, ATTN_DIM), F32)
        kn_pad = jnp.concatenate([kn.reshape(nseq, seq, ATTN_DIM), zpad], axis=1)
        vn_pad = jnp.concatenate([v.reshape(nseq, seq, ATTN_DIM), zpad], axis=1)
        lfn_t = jnp.concatenate([jnp.transpose(logf, (0, 2, 1)),
                                 jnp.zeros((nseq, FOX_HEADS, PAGE_SIZE - seq), F32)], axis=2)
        attn = _fox_paged(l, page_table, qbd, kn_pad, vn_pad, lfn_t, cache_k, cache_v, cache_lf_t,
                          nseq, seq).reshape(n, ATTN_DIM)

    yssd, h_end = _ssd(xbc, _pad_state(ssd_conv_state, SSD_CONV), fdt, jnp.transpose(dt, (0, 2, 1)), z,
                       ssd_state, p['ssd_conv_w'][l], p['ssd_conv_b'][l][None, :], lp['av'], lp['ac'],
                       lp['dv'], nseq, seq)
    x = _merge(x, scb, cin, _pad_state(conv_state, SC_WIDTH), attn, yssd, p['sc_conv_w'][l],
               p['mix_out_norm'][l][None, :], p['w_out_bf'][l], nseq, seq)

    cin3 = cin.reshape(nseq, seq, CONV_DIM)
    xbc3 = xbc.reshape(nseq, seq, XBC_DIM)
    new_conv = jnp.concatenate([conv_state, cin3], axis=1)[:, -(SC_WIDTH - 1):]
    new_ssd_conv = jnp.concatenate([ssd_conv_state, xbc3], axis=1)[:, -(SSD_CONV - 1):]
    state = (kn.reshape(nseq, seq, FOX_HEADS, HEAD_DIM), v.reshape(nseq, seq, FOX_HEADS, HEAD_DIM),
             logf, new_conv, new_ssd_conv, h_end)
    return x, state


def kernel(x_prompt, x_sample, cache_fox_k, cache_fox_v, cache_fox_logf, cache_mem_k, cache_mem_v,
           state_conv, state_ssd_conv, state_ssd, page_table, mem_prompt,
           ffn1_norm, ffn1_wg, ffn1_wu, ffn1_wd, mix_norm, w_in, sc_conv_w, fox_q_norm, fox_k_norm,
           fox_f_bias, ssd_conv_w, ssd_conv_b, ssd_dt_bias, ssd_A_log, ssd_D, mix_out_norm, w_out,
           xa_norm, mem_norm, xa_wq, xa_wk, xa_wv, xa_q_norm, xa_k_norm, xa_wo,
           ffn2_norm, ffn2_wg, ffn2_wu, ffn2_wd, final_norm):
    b_p, s_p, _ = x_prompt.shape
    b_s, s_s, _ = x_sample.shape
    n_phys = cache_fox_k.shape[1]
    p = dict(mix_norm=mix_norm, w_in=w_in, sc_conv_w=sc_conv_w, fox_q_norm=fox_q_norm, fox_k_norm=fox_k_norm,
             fox_f_bias=fox_f_bias, ssd_conv_w=ssd_conv_w, ssd_conv_b=ssd_conv_b, ssd_dt_bias=ssd_dt_bias,
             ssd_A_log=ssd_A_log, ssd_D=ssd_D, mix_out_norm=mix_out_norm, xa_q_norm=xa_q_norm,
             w_out_bf=w_out.astype(BF16))
    wg1, wu1, wd1 = ffn1_wg.astype(BF16), ffn1_wu.astype(BF16), ffn1_wd.astype(BF16)
    wg2, wu2, wd2 = ffn2_wg.astype(BF16), ffn2_wu.astype(BF16), ffn2_wd.astype(BF16)
    wq, wo = xa_wq.astype(BF16), xa_wo.astype(BF16)
    fg = final_norm[None, :]

    km_p, vm_p = _memkv(mem_prompt, mem_norm[:, None, :], xa_wk.astype(BF16), xa_wv.astype(BF16),
                        jnp.tile(xa_k_norm, (1, MEM_HEADS))[:, None, :])
    km_s = cache_mem_k.reshape(DEPTH, b_s, MEM_LEN, MEM_DIM)
    vm_s = cache_mem_v.reshape(DEPTH, b_s, MEM_LEN, MEM_DIM)
    cache_k = cache_fox_k.reshape(DEPTH, n_phys, PAGE_SIZE, ATTN_DIM)
    cache_v = cache_fox_v.reshape(DEPTH, n_phys, PAGE_SIZE, ATTN_DIM)
    cache_lf_t = jnp.transpose(cache_fox_logf, (0, 1, 3, 2))

    zero_conv = jnp.zeros((b_p, SC_WIDTH - 1, CONV_DIM), F32)
    zero_ssd_conv = jnp.zeros((b_p, SSD_CONV - 1, XBC_DIM), F32)
    zero_ssd = jnp.zeros((b_p, SSD_HEADS, HEAD_DIM, SSD_STATE), F32)

    xp = x_prompt.reshape(b_p * s_p, D_MODEL)
    xs = x_sample.reshape(b_s * s_s, D_MODEL)
    st_p, st_s = [], []
    for l in range(DEPTH):
        lp = _layer_params(l, p)
        last = l == DEPTH - 1
        groups = []
        for x, nseq, seq, km, vm, cs, scs, ss, paged in (
                (xp, b_p, s_p, km_p, vm_p, zero_conv, zero_ssd_conv, zero_ssd, None),
                (xs, b_s, s_s, km_s, vm_s, state_conv[l], state_ssd_conv[l], state_ssd[l],
                 (page_table, cache_k, cache_v, cache_lf_t))):
            x = _ffn(x, ffn1_norm[l][None, :], wg1[l], wu1[l], wd1[l], fg, False)
            x, st = _token_mix(l, x, p, lp, nseq, seq, cs, scs, ss, paged)
            x = _xattn(x, km, vm, l, xa_norm[l][None, :], wq[l], lp['xa_qg'], wo[l], nseq, seq)
            x = _ffn(x, ffn2_norm[l][None, :], wg2[l], wu2[l], wd2[l], fg, last)
            groups.append((x, st))
        (xp, sp), (xs, ss_) = groups
        st_p.append(sp)
        st_s.append(ss_)

    def stack(states, idx):
        return jnp.stack([s[idx] for s in states])

    return (xp.reshape(b_p, s_p, D_MODEL), xs.reshape(b_s, s_s, D_MODEL),
            stack(st_p, 0), stack(st_p, 1), stack(st_p, 2), km_p.reshape(DEPTH, b_p, MEM_LEN, MEM_HEADS, MEM_HEAD_DIM),
            vm_p.reshape(DEPTH, b_p, MEM_LEN, MEM_HEADS, MEM_HEAD_DIM),
            stack(st_p, 3), stack(st_p, 4), stack(st_p, 5),
            stack(st_s, 0), stack(st_s, 1), stack(st_s, 2), stack(st_s, 3), stack(st_s, 4), stack(st_s, 5))
```

```python
import functools

import numpy as np
import jax
import jax.numpy as jnp
from jax import lax
from jax.experimental import pallas as pl
from jax.experimental.pallas import tpu as pltpu

F32 = jnp.float32
BF16 = jnp.bfloat16

D_MODEL = 1024
DEPTH = 4
PAGE_SIZE = 128
HEAD_DIM = 64
CONV_DIM = 256
ATTN_DIM = 512
SSD_DIM = 256
FOX_HEADS = 8
SC_WIDTH = 3
SSD_HEADS = 4
SSD_STATE = 64
SSD_CONV = 4
SSD_CHUNK = 128
XBC_DIM = 512
FFN_DIM = 2816
MEM_LEN = 256
MEM_HEADS = 4
MEM_HEAD_DIM = 128
MEM_DIM = 512
RMS_EPS = 1e-6
LOG2E = 1.4426950408889634

LANES = 128
SUBLANES = 8
VMEM_LIMIT = 56 * 1024 * 1024
ROW_TILE = 512
PAGES_PER_STEP = 16
PROJ_MAIN = 3 * CONV_DIM + 3 * ATTN_DIM + SSD_DIM + XBC_DIM
PROJ_PAD = PROJ_MAIN + LANES
F_LANE0 = 0
DT_LANE0 = FOX_HEADS
PAD_DIM = FOX_HEADS * LANES
BIAS_LANE0 = HEAD_DIM
N_SPLIT = 3


def _params(sem):
    return pltpu.CompilerParams(dimension_semantics=sem, vmem_limit_bytes=VMEM_LIMIT)


def _resident(shape):
    nd = len(shape)
    return pl.BlockSpec(shape, lambda *_: (0,) * nd, pipeline_mode=pl.Buffered(1))


def _rms(x, g):
    ms = jnp.mean(x * x, axis=-1, keepdims=True)
    return x * lax.rsqrt(ms + RMS_EPS) * g


def _head_rms(x, g, head_dim):
    width = x.shape[-1]
    pieces = []
    for c in range(width // LANES):
        blk = x[:, c * LANES:(c + 1) * LANES]
        sq = blk * blk
        s_all = jnp.sum(sq, axis=-1, keepdims=True)
        if head_dim == LANES:
            ms = s_all * (1.0 / LANES)
        else:
            lo = lax.broadcasted_iota(jnp.int32, sq.shape, 1) < head_dim
            s_lo = jnp.sum(jnp.where(lo, sq, 0.0), axis=-1, keepdims=True)
            ms = jnp.where(lo, s_lo, s_all - s_lo) * (1.0 / head_dim)
        pieces.append(blk * lax.rsqrt(ms + RMS_EPS))
    return jnp.concatenate(pieces, axis=-1) * g


def _silu(x):
    return x * jax.nn.sigmoid(x)


def _softplus(x):
    return jnp.maximum(x, 0.0) + jnp.log1p(jnp.exp(-jnp.abs(x)))


def _split3(x):
    hi = x.astype(BF16)
    r1 = x - hi.astype(F32)
    mid = r1.astype(BF16)
    lo = (r1 - mid.astype(F32)).astype(BF16)
    return hi, mid, lo


def _dot_exact_rhs(x, m_bf16):
    hi, mid, lo = _split3(x)
    out = jnp.dot(hi, m_bf16, preferred_element_type=F32)
    out = out + jnp.dot(mid, m_bf16, preferred_element_type=F32)
    return out + jnp.dot(lo, m_bf16, preferred_element_type=F32)


def _dot_exact_lhs(m_bf16, x):
    hi, mid, lo = _split3(x)
    out = jnp.dot(m_bf16, hi, preferred_element_type=F32)
    out = out + jnp.dot(m_bf16, mid, preferred_element_type=F32)
    return out + jnp.dot(m_bf16, lo, preferred_element_type=F32)


def _lower_tri(n, seq=None):
    r = lax.broadcasted_iota(jnp.int32, (n, n), 0)
    c = lax.broadcasted_iota(jnp.int32, (n, n), 1)
    keep = c <= r
    if seq is not None and seq < n:
        shift = jnp.int32(seq.bit_length() - 1)
        keep = keep & (lax.shift_right_logical(r, shift) == lax.shift_right_logical(c, shift))
    return keep.astype(BF16)


def _upper_tri(n):
    r = lax.broadcasted_iota(jnp.int32, (n, n), 0)
    c = lax.broadcasted_iota(jnp.int32, (n, n), 1)
    return (r <= c).astype(BF16)


def _nt(a, b):
    return lax.dot_general(a, b, (((1,), (1,)), ((), ())), preferred_element_type=F32)


def _tn(a, b):
    return lax.dot_general(a, b, (((0,), (0,)), ((), ())), preferred_element_type=F32)


def _lane_tile(x, reps):
    return x if reps == 1 else jnp.concatenate([x] * reps, axis=-1)


def _ffn_kernel(x_ref, g_ref, wg_ref, wu_ref, wd_ref, fg_ref, o_ref, *, final):
    x = x_ref[...]
    h = _rms(x, g_ref[...]).astype(BF16)
    a = jnp.dot(h, wg_ref[...], preferred_element_type=F32)
    b = jnp.dot(h, wu_ref[...], preferred_element_type=F32)
    t = (_silu(a) * b).astype(BF16)
    out = x + 0.5 * jnp.dot(t, wd_ref[...], preferred_element_type=F32)
    if final:
        out = _rms(out, fg_ref[...])
    o_ref[...] = out


def _ffn(x, g, wg, wu, wd, fg, final):
    n = x.shape[0]
    tm = min(ROW_TILE // 2, n)
    return pl.pallas_call(
        functools.partial(_ffn_kernel, final=final),
        out_shape=jax.ShapeDtypeStruct((n, D_MODEL), F32),
        grid=(n // tm,),
        in_specs=[
            pl.BlockSpec((tm, D_MODEL), lambda i: (i, 0)),
            _resident((1, D_MODEL)),
            _resident((D_MODEL, FFN_DIM)),
            _resident((D_MODEL, FFN_DIM)),
            _resident((FFN_DIM, D_MODEL)),
            _resident((1, D_MODEL)),
        ],
        out_specs=pl.BlockSpec((tm, D_MODEL), lambda i: (i, 0)),
        compiler_params=_params(("arbitrary",)),
        name="ffn",
    )(x, g, wg, wu, wd, fg)


def _expand_heads(x, fill):
    lo = lax.broadcasted_iota(jnp.int32, (x.shape[0], LANES), 1) < HEAD_DIM
    blocks = []
    for c in range(ATTN_DIM // LANES):
        blk = x[:, c * LANES:(c + 1) * LANES]
        for half, src in enumerate((blk, pltpu.roll(blk, HEAD_DIM, axis=1))):
            h = 2 * c + half
            blocks.append(jnp.where(lo, src, fill[:, h * LANES:(h + 1) * LANES]))
    return jnp.concatenate(blocks, axis=-1).astype(BF16)


def _inproj_kernel(x_ref, g_ref, w_ref, qg_ref, kg_ref, fb_ref, pm_ref, pc_ref,
                   scb_ref, cin_ref, qa_ref, kn_ref, ka_ref, v_ref, va_ref,
                   z_ref, xbc_ref, fdt_ref, carry_ref, *, tm, seq):
    h = _rms(x_ref[...], g_ref[...]).astype(BF16)

    def proj(a, b):
        return _nt(h, w_ref[a:b, :])

    o = 0
    scb_ref[...] = proj(o, o + CONV_DIM)
    o += CONV_DIM
    cin_ref[...] = proj(o, o + CONV_DIM) * proj(o + CONV_DIM, o + 2 * CONV_DIM)
    o += 2 * CONV_DIM
    q = _head_rms(proj(o, o + ATTN_DIM), qg_ref[...], HEAD_DIM) * (HEAD_DIM ** -0.5 * LOG2E)
    o += ATTN_DIM
    k = _head_rms(proj(o, o + ATTN_DIM), kg_ref[...], HEAD_DIM)
    kn_ref[...] = k
    o += ATTN_DIM
    v = proj(o, o + ATTN_DIM)
    v_ref[...] = v
    o += ATTN_DIM
    z_ref[...] = proj(o, o + SSD_DIM)
    o += SSD_DIM
    xbc_ref[...] = proj(o, o + XBC_DIM)
    o += XBC_DIM
    u = proj(o, o + LANES) + fb_ref[...]
    lane = lax.broadcasted_iota(jnp.int32, u.shape, 1)
    fdt = jnp.where(lane < DT_LANE0, -_softplus(-u), _softplus(u))
    fdt_ref[...] = fdt

    @pl.when(pl.program_id(0) % max(seq // tm, 1) == 0)
    def _():
        carry_ref[...] = jnp.zeros_like(carry_ref)

    lf = jnp.where(lane < FOX_HEADS, fdt, 0.0) * LOG2E
    c = _dot_exact_lhs(_lower_tri(tm, seq), lf) + carry_ref[0:1, :]
    carry_ref[...] = jnp.broadcast_to(c[tm - 1:tm, :], carry_ref.shape)
    hi, mid, lo = _split3(c)
    packed = (hi.astype(F32) + pltpu.roll(mid.astype(F32), FOX_HEADS, axis=1)
              + pltpu.roll(lo.astype(F32), 2 * FOX_HEADS, axis=1)).astype(BF16)
    bias = jnp.dot(packed, pm_ref[...], preferred_element_type=F32) + pc_ref[:, 0:2 * PAD_DIM]
    qa_ref[...] = _expand_heads(q, bias[:, 0:PAD_DIM])
    ka_ref[...] = _expand_heads(k, bias[:, PAD_DIM:2 * PAD_DIM])
    va_ref[...] = _expand_heads(v, pc_ref[:, 2 * PAD_DIM:3 * PAD_DIM])


def _bias_placement():
    pm = np.zeros((LANES, 2 * PAD_DIM), np.float32)
    pc = np.zeros((1, 3 * PAD_DIM), np.float32)
    for h in range(FOX_HEADS):
        for part in range(N_SPLIT):
            src = part * FOX_HEADS + h
            pm[src, h * LANES + BIAS_LANE0 + part] = 1.0
            pm[src, PAD_DIM + h * LANES + BIAS_LANE0 + N_SPLIT + part] = -1.0
            pc[0, h * LANES + BIAS_LANE0 + N_SPLIT + part] = 1.0
            pc[0, PAD_DIM + h * LANES + BIAS_LANE0 + part] = 1.0
        pc[0, 2 * PAD_DIM + h * LANES + HEAD_DIM] = 1.0
    return jnp.asarray(pm, BF16), jnp.asarray(pc, F32)


def _inproj(x, g, w, qg, kg, fb, pm, pc, seq):
    n = x.shape[0]
    tm = min(ROW_TILE // 2, n)
    widths = [(CONV_DIM, F32), (CONV_DIM, F32), (PAD_DIM, BF16), (ATTN_DIM, F32), (PAD_DIM, BF16),
              (ATTN_DIM, F32), (PAD_DIM, BF16), (SSD_DIM, F32), (XBC_DIM, F32), (LANES, F32)]
    return pl.pallas_call(
        functools.partial(_inproj_kernel, tm=tm, seq=seq),
        out_shape=[jax.ShapeDtypeStruct((n, w_), dt) for w_, dt in widths],
        grid=(n // tm,),
        in_specs=[
            pl.BlockSpec((tm, D_MODEL), lambda i: (i, 0)),
            _resident((1, D_MODEL)),
            _resident((PROJ_PAD, D_MODEL)),
            _resident((1, ATTN_DIM)),
            _resident((1, ATTN_DIM)),
            _resident((1, LANES)),
            _resident((LANES, 2 * PAD_DIM)),
            _resident((1, 3 * PAD_DIM)),
        ],
        out_specs=[pl.BlockSpec((tm, w_), lambda i: (i, 0)) for w_, _ in widths],
        scratch_shapes=[pltpu.VMEM((SUBLANES, LANES), F32)],
        compiler_params=_params(("arbitrary",)),
        name="inproj",
    )(x, g, w, qg, kg, fb, pm, pc)


def _fox_kernel(q_ref, k_ref, v_ref, o_ref, m_ref, acc_ref, *, tq):
    i = pl.program_id(2)
    m_ref[...] = jnp.full(m_ref.shape, -jnp.inf, F32)
    acc_ref[...] = jnp.zeros_like(acc_ref)
    reps = tq // LANES

    def tile(j, masked):
        rows = pl.ds(pl.multiple_of(j * tq, tq), tq)
        for hh in range(2):
            sl = slice(hh * LANES, (hh + 1) * LANES)
            s = _nt(q_ref[:, sl], k_ref[rows, sl])
            if masked:
                row = lax.broadcasted_iota(jnp.int32, s.shape, 0)
                col = lax.broadcasted_iota(jnp.int32, s.shape, 1)
                s = jnp.where(col <= row, s, -jnp.inf)
            m_old = m_ref[hh]
            m_new = jnp.maximum(m_old, jnp.max(s, axis=-1, keepdims=True))
            p = jnp.exp2(s - _lane_tile(m_new, reps))
            acc_ref[hh] = jnp.exp2(m_old - m_new) * acc_ref[hh] + jnp.dot(
                p.astype(BF16), v_ref[rows, sl], preferred_element_type=F32)
            m_ref[hh] = m_new

    def body(j, carry):
        tile(j, False)
        return carry

    lax.fori_loop(0, i, body, 0)
    tile(i, True)
    outs = []
    for hh in range(2):
        a = acc_ref[hh]
        outs.append(a[:, 0:HEAD_DIM] / a[:, HEAD_DIM:HEAD_DIM + 1])
    o_ref[...] = jnp.concatenate(outs, axis=-1)


def _fox_prompt(qa, ka, va, nseq, seq):
    tq = min(ROW_TILE, seq)
    nq = seq // tq
    pair = 2 * LANES
    return pl.pallas_call(
        functools.partial(_fox_kernel, tq=tq),
        out_shape=jax.ShapeDtypeStruct((nseq * seq, ATTN_DIM), F32),
        grid=(nseq, FOX_HEADS // 2, nq),
        in_specs=[
            pl.BlockSpec((tq, pair), lambda b, h, i: (b * nq + i, h)),
            pl.BlockSpec((seq, pair), lambda b, h, i: (b, h)),
            pl.BlockSpec((seq, pair), lambda b, h, i: (b, h)),
        ],
        out_specs=pl.BlockSpec((tq, LANES), lambda b, h, i: (b * nq + i, h)),
        scratch_shapes=[pltpu.VMEM((2, tq, LANES), F32), pltpu.VMEM((2, tq, LANES), F32)],
        compiler_params=_params(("arbitrary",) * 3),
        name="fox_prompt",
    )(qa, ka, va)


def _fox_paged_kernel(pt_ref, q_ref, kn_ref, vn_ref, lfn_ref, *rest, tokens):
    npg = PAGES_PER_STEP
    k_refs = rest[:npg]
    v_refs = rest[npg:2 * npg]
    lf_refs = rest[2 * npg:3 * npg]
    o_ref, m_ref, l_ref, acc_ref, carry_ref = rest[3 * npg:]
    g = pl.program_id(1)
    rows = FOX_HEADS * tokens

    @pl.when(g == 0)
    def _():
        m_ref[...] = jnp.full(m_ref.shape, -jnp.inf, F32)
        l_ref[...] = jnp.zeros_like(l_ref)
        acc_ref[...] = jnp.zeros_like(acc_ref)
        carry_ref[...] = jnp.zeros_like(carry_ref)

    q = q_ref[...]
    tri = _upper_tri(PAGE_SIZE)

    def update(k_list, v_list, lf_list, masked):
        n = len(k_list)
        cw = _dot_exact_rhs(jnp.concatenate(lf_list, axis=0) * LOG2E, tri)
        off = carry_ref[:, 0:1]
        s_list = []
        for idx in range(n):
            c = cw[idx * FOX_HEADS:(idx + 1) * FOX_HEADS, :] + off
            off = c[:, PAGE_SIZE - 1:PAGE_SIZE]
            c_rows = jnp.broadcast_to(c[:, None, :], (FOX_HEADS, tokens, PAGE_SIZE)).reshape(rows, PAGE_SIZE)
            s_list.append(jnp.dot(q, k_list[idx].astype(BF16), preferred_element_type=F32) - c_rows)
        carry_ref[...] = jnp.broadcast_to(off, carry_ref.shape)
        s = _lane_tile(s_list[0], 1) if n == 1 else jnp.concatenate(s_list, axis=-1)
        if masked:
            t_idx = lax.broadcasted_iota(jnp.int32, (FOX_HEADS, tokens, PAGE_SIZE), 1).reshape(rows, PAGE_SIZE)
            col = lax.broadcasted_iota(jnp.int32, s.shape, 1)
            s = jnp.where(col <= t_idx, s, -jnp.inf)
        m_old = m_ref[...]
        m_new = jnp.maximum(m_old, jnp.max(s, axis=-1, keepdims=True))
        alpha = jnp.exp2(m_old - m_new)
        p = jnp.exp2(s - _lane_tile(m_new, n))
        l_ref[...] = alpha * l_ref[...] + jnp.sum(p, axis=-1, keepdims=True)
        pv = _nt(p[:, 0:PAGE_SIZE].astype(BF16), v_list[0].astype(BF16))
        for idx in range(1, n):
            pv = pv + _nt(p[:, idx * PAGE_SIZE:(idx + 1) * PAGE_SIZE].astype(BF16), v_list[idx].astype(BF16))
        acc_ref[...] = _lane_tile(alpha, ATTN_DIM // LANES) * acc_ref[...] + pv
        m_ref[...] = m_new

    update([r[...] for r in k_refs], [r[...] for r in v_refs], [r[...] for r in lf_refs], False)

    @pl.when(g == pl.num_programs(1) - 1)
    def _():
        update([kn_ref[...]], [vn_ref[...]], [lfn_ref[...]], True)
        full = acc_ref[...] / _lane_tile(l_ref[...], ATTN_DIM // LANES)
        lane_head = lax.shift_right_logical(
            lax.broadcasted_iota(jnp.int32, (tokens, ATTN_DIM), 1), jnp.int32(HEAD_DIM.bit_length() - 1))
        out = jnp.zeros((tokens, ATTN_DIM), F32)
        for h in range(FOX_HEADS):
            out = out + jnp.where(lane_head == h, full[h * tokens:(h + 1) * tokens, :], 0.0)
        o_ref[...] = out


def _fox_paged(layer, page_table, qbd, kn_t, vn_t, lfn_t, cache_kt, cache_vt, cache_lf_t, nseq, tokens):
    npg = PAGES_PER_STEP
    n_groups = page_table.shape[1] // npg
    rows = FOX_HEADS * tokens

    def page_spec(width2, n):
        return pl.BlockSpec((None, None, width2, PAGE_SIZE),
                            lambda b, g, pt: (layer, pt[b, g * npg + n], 0, 0))

    in_specs = [
        pl.BlockSpec((None, rows, ATTN_DIM), lambda b, g, pt: (b, 0, 0)),
        pl.BlockSpec((None, ATTN_DIM, PAGE_SIZE), lambda b, g, pt: (b, 0, 0)),
        pl.BlockSpec((None, ATTN_DIM, PAGE_SIZE), lambda b, g, pt: (b, 0, 0)),
        pl.BlockSpec((None, FOX_HEADS, PAGE_SIZE), lambda b, g, pt: (b, 0, 0)),
    ]
    in_specs += [page_spec(ATTN_DIM, n) for n in range(npg)]
    in_specs += [page_spec(ATTN_DIM, n) for n in range(npg)]
    in_specs += [page_spec(FOX_HEADS, n) for n in range(npg)]
    grid_spec = pltpu.PrefetchScalarGridSpec(
        num_scalar_prefetch=1,
        grid=(nseq, n_groups),
        in_specs=in_specs,
        out_specs=pl.BlockSpec((None, tokens, ATTN_DIM), lambda b, g, pt: (b, 0, 0)),
        scratch_shapes=[pltpu.VMEM((rows, LANES), F32), pltpu.VMEM((rows, LANES), F32),
                        pltpu.VMEM((rows, ATTN_DIM), F32), pltpu.VMEM((FOX_HEADS, LANES), F32)],
    )
    return pl.pallas_call(
        functools.partial(_fox_paged_kernel, tokens=tokens),
        out_shape=jax.ShapeDtypeStruct((nseq, tokens, ATTN_DIM), F32),
        grid_spec=grid_spec,
        compiler_params=_params(("arbitrary", "arbitrary")),
        name="fox_paged",
    )(page_table, qbd, kn_t, vn_t, lfn_t,
      *([cache_kt] * npg), *([cache_vt] * npg), *([cache_lf_t] * npg))


def _ssd_kernel(xbc_ref, halo_ref, st_ref, fdt_ref, dtt_ref, z_ref, h0_ref,
                cw_ref, cb_ref, av_ref, ac_ref, dv_ref,
                y_ref, hout_ref, buf_ref, h_ref, *, lc):
    j = pl.program_id(1)

    @pl.when(j == 0)
    def _():
        h_ref[...] = h0_ref[...]
        buf_ref[0:SUBLANES, :] = st_ref[...]

    @pl.when(j > 0)
    def _():
        buf_ref[0:SUBLANES, :] = halo_ref[...]

    buf_ref[SUBLANES:SUBLANES + lc, :] = xbc_ref[...]
    base = SUBLANES - (SSD_CONV - 1)
    u = buf_ref[base:base + lc, :] * cw_ref[0:1, :]
    for k in range(1, SSD_CONV):
        u = u + buf_ref[base + k:base + k + lc, :] * cw_ref[k:k + 1, :]
    u = _silu(u + cb_ref[...])

    fdt = fdt_ref[...]
    acum = _dot_exact_lhs(_lower_tri(lc), fdt * av_ref[...])
    dtt = dtt_ref[...]
    acum_t = _dot_exact_rhs(dtt * ac_ref[...], _upper_tri(lc))
    row = lax.broadcasted_iota(jnp.int32, (lc, lc), 0)
    col = lax.broadcasted_iota(jnp.int32, (lc, lc), 1)
    causal = col <= row
    z = z_ref[...]
    dvec = dv_ref[...]

    outs = []
    for h in range(SSD_HEADS):
        grp = h // (SSD_HEADS // 2)
        xh = u[:, h * HEAD_DIM:(h + 1) * HEAD_DIM]
        bh = u[:, SSD_DIM + grp * SSD_STATE:SSD_DIM + (grp + 1) * SSD_STATE]
        ch = u[:, SSD_DIM + 2 * SSD_STATE + grp * SSD_STATE:SSD_DIM + 2 * SSD_STATE + (grp + 1) * SSD_STATE]
        lane = DT_LANE0 + h
        a_col = acum[:, lane:lane + 1]
        dt_col = fdt[:, lane:lane + 1]
        a_row = acum_t[h:h + 1, :]
        dt_row = dtt[h:h + 1, :]
        a_last = acum[lc - 1:lc, lane:lane + 1]
        decay = jnp.exp(jnp.where(causal, a_col - a_row, -jnp.inf))
        xb = xh.astype(BF16)
        bb = bh.astype(BF16)
        cb16 = ch.astype(BF16)
        w = _nt(cb16, bb) * decay * dt_row
        h_prev = h_ref[h]
        y = jnp.dot(w.astype(BF16), xb, preferred_element_type=F32)
        y = y + _nt(cb16, h_prev.astype(BF16)) * jnp.exp(a_col)
        w_end = jnp.exp(a_last - a_col) * dt_col
        h_ref[h] = h_prev * jnp.exp(a_last) + _tn((xh * w_end).astype(BF16), bb)
        sl = slice(h * HEAD_DIM, (h + 1) * HEAD_DIM)
        outs.append((y + dvec[:, sl] * xh) * _silu(z[:, sl]))
    y_ref[...] = jnp.concatenate(outs, axis=-1)

    @pl.when(j == pl.num_programs(1) - 1)
    def _():
        hout_ref[...] = h_ref[...]


def _ssd(xbc, st_pad, fdt, dtt, z, h0, cw, cb, av, ac, dv, nseq, seq):
    lc = min(SSD_CHUNK, seq)
    nc = seq // lc
    hb = lc // SUBLANES
    return pl.pallas_call(
        functools.partial(_ssd_kernel, lc=lc),
        out_shape=[jax.ShapeDtypeStruct((nseq * seq, SSD_DIM), F32),
                   jax.ShapeDtypeStruct((nseq, SSD_HEADS, HEAD_DIM, SSD_STATE), F32)],
        grid=(nseq, nc),
        in_specs=[
            pl.BlockSpec((lc, XBC_DIM), lambda b, j: (b * nc + j, 0)),
            pl.BlockSpec((SUBLANES, XBC_DIM), lambda b, j: (jnp.maximum((b * nc + j) * hb - 1, 0), 0)),
            pl.BlockSpec((None, SUBLANES, XBC_DIM), lambda b, j: (b, 0, 0)),
            pl.BlockSpec((lc, LANES), lambda b, j: (b * nc + j, 0)),
            pl.BlockSpec((None, SSD_HEADS, lc), lambda b, j: (b, 0, j)),
            pl.BlockSpec((lc, SSD_DIM), lambda b, j: (b * nc + j, 0)),
            pl.BlockSpec((None, SSD_HEADS, HEAD_DIM, SSD_STATE), lambda b, j: (b, 0, 0, 0)),
            _resident((SSD_CONV, XBC_DIM)),
            _resident((1, XBC_DIM)),
            _resident((1, LANES)),
            _resident((SSD_HEADS, 1)),
            _resident((1, SSD_DIM)),
        ],
        out_specs=[pl.BlockSpec((lc, SSD_DIM), lambda b, j: (b * nc + j, 0)),
                   pl.BlockSpec((None, SSD_HEADS, HEAD_DIM, SSD_STATE), lambda b, j: (b, 0, 0, 0))],
        scratch_shapes=[pltpu.VMEM((SUBLANES + lc, XBC_DIM), F32),
                        pltpu.VMEM((SSD_HEADS, HEAD_DIM, SSD_STATE), F32)],
        compiler_params=_params(("arbitrary", "arbitrary")),
        name="ssd",
    )(xbc, xbc, st_pad, fdt, dtt, z, h0, cw, cb, av, ac, dv)


def _merge_kernel(x_ref, scb_ref, cin_ref, halo_ref, st_ref, attn_ref, yssd_ref,
                  cw_ref, g_ref, w_ref, o_ref, buf_ref, *, tl):
    j = pl.program_id(1)

    @pl.when(j == 0)
    def _():
        buf_ref[0:SUBLANES, :] = st_ref[...]

    @pl.when(j > 0)
    def _():
        buf_ref[0:SUBLANES, :] = halo_ref[...]

    buf_ref[SUBLANES:SUBLANES + tl, :] = cin_ref[...]
    base = SUBLANES - (SC_WIDTH - 1)
    y = buf_ref[base:base + tl, :] * cw_ref[0:1, :]
    for k in range(1, SC_WIDTH):
        y = y + buf_ref[base + k:base + k + tl, :] * cw_ref[k:k + 1, :]
    cat = jnp.concatenate([scb_ref[...] * y, attn_ref[...], yssd_ref[...]], axis=-1)
    cat = _head_rms(cat, g_ref[...], HEAD_DIM).astype(BF16)
    o_ref[...] = x_ref[...] + jnp.dot(cat, w_ref[...], preferred_element_type=F32)


def _merge(x, scb, cin, st_pad, attn, yssd, cw, g, w, nseq, seq):
    tl = min(ROW_TILE, seq)
    nt = seq // tl
    hb = tl // SUBLANES

    def rows(width):
        return pl.BlockSpec((tl, width), lambda b, j: (b * nt + j, 0))

    return pl.pallas_call(
        functools.partial(_merge_kernel, tl=tl),
        out_shape=jax.ShapeDtypeStruct((nseq * seq, D_MODEL), F32),
        grid=(nseq, nt),
        in_specs=[
            rows(D_MODEL), rows(CONV_DIM), rows(CONV_DIM),
            pl.BlockSpec((SUBLANES, CONV_DIM), lambda b, j: (jnp.maximum((b * nt + j) * hb - 1, 0), 0)),
            pl.BlockSpec((None, SUBLANES, CONV_DIM), lambda b, j: (b, 0, 0)),
            rows(ATTN_DIM), rows(SSD_DIM),
            _resident((SC_WIDTH, CONV_DIM)),
            _resident((1, D_MODEL)),
            _resident((D_MODEL, D_MODEL)),
        ],
        out_specs=rows(D_MODEL),
        scratch_shapes=[pltpu.VMEM((SUBLANES + tl, CONV_DIM), F32)],
        compiler_params=_params(("arbitrary", "arbitrary")),
        name="merge",
    )(x, scb, cin, cin, st_pad, attn, yssd, cw, g, w)


def _memkv_kernel(mem_ref, g_ref, wk_ref, wv_ref, kg_ref, km_ref, vm_ref):
    m = _rms(mem_ref[...], g_ref[...]).astype(BF16)
    km_ref[...] = _head_rms(jnp.dot(m, wk_ref[...], preferred_element_type=F32), kg_ref[...], MEM_HEAD_DIM)
    vm_ref[...] = jnp.dot(m, wv_ref[...], preferred_element_type=F32)


def _memkv(mem, g, wk, wv, kg):
    nb = mem.shape[0]
    out = jax.ShapeDtypeStruct((DEPTH, nb, MEM_LEN, MEM_DIM), F32)
    return pl.pallas_call(
        _memkv_kernel,
        out_shape=[out, out],
        grid=(DEPTH, nb),
        in_specs=[
            pl.BlockSpec((None, MEM_LEN, D_MODEL), lambda l, b: (b, 0, 0)),
            pl.BlockSpec((None, 1, D_MODEL), lambda l, b: (l, 0, 0)),
            pl.BlockSpec((None, D_MODEL, MEM_DIM), lambda l, b: (l, 0, 0)),
            pl.BlockSpec((None, D_MODEL, MEM_DIM), lambda l, b: (l, 0, 0)),
            pl.BlockSpec((None, 1, MEM_DIM), lambda l, b: (l, 0, 0)),
        ],
        out_specs=[pl.BlockSpec((None, None, MEM_LEN, MEM_DIM), lambda l, b: (l, b, 0, 0))] * 2,
        compiler_params=_params(("arbitrary", "arbitrary")),
        name="mem_kv",
    )(mem, g, wk, wv, kg)


def _xattn_kernel(x_ref, km_ref, vm_ref, g_ref, wq_ref, qg_ref, wo_ref, o_ref):
    x = x_ref[...]
    h = _rms(x, g_ref[...]).astype(BF16)
    q = _head_rms(jnp.dot(h, wq_ref[...], preferred_element_type=F32), qg_ref[...], MEM_HEAD_DIM)
    q = (q * (MEM_HEAD_DIM ** -0.5)).astype(BF16)
    km = km_ref[...].astype(BF16)
    vm = vm_ref[...].astype(BF16)
    outs = []
    for hh in range(MEM_HEADS):
        sl = slice(hh * MEM_HEAD_DIM, (hh + 1) * MEM_HEAD_DIM)
        s = _nt(q[:, sl], km[:, sl])
        p = jnp.exp(s - jnp.max(s, axis=-1, keepdims=True))
        p = p / jnp.sum(p, axis=-1, keepdims=True)
        outs.append(jnp.dot(p.astype(BF16), vm[:, sl], preferred_element_type=F32))
    o = jnp.concatenate(outs, axis=-1).astype(BF16)
    o_ref[...] = x + jnp.dot(o, wo_ref[...], preferred_element_type=F32)


def _xattn(x, km, vm, layer, g, wq, qg, wo, nseq, seq):
    tl = min(ROW_TILE, seq)
    nt = seq // tl
    return pl.pallas_call(
        _xattn_kernel,
        out_shape=jax.ShapeDtypeStruct((nseq * seq, D_MODEL), F32),
        grid=(nseq, nt),
        in_specs=[
            pl.BlockSpec((tl, D_MODEL), lambda b, j: (b * nt + j, 0)),
            pl.BlockSpec((None, None, MEM_LEN, MEM_DIM), lambda b, j: (layer, b, 0, 0)),
            pl.BlockSpec((None, None, MEM_LEN, MEM_DIM), lambda b, j: (layer, b, 0, 0)),
            _resident((1, D_MODEL)),
            _resident((D_MODEL, MEM_DIM)),
            _resident((1, MEM_DIM)),
            _resident((MEM_DIM, D_MODEL)),
        ],
        out_specs=pl.BlockSpec((tl, D_MODEL), lambda b, j: (b * nt + j, 0)),
        compiler_params=_params(("arbitrary", "arbitrary")),
        name="xattn",
    )(x, km, vm, g, wq, qg, wo)


def _pad_state(state, width):
    nseq, _, c = state.shape
    return jnp.concatenate([jnp.zeros((nseq, SUBLANES - (width - 1), c), F32), state], axis=1)


def _prep_w_in(w_in):
    w_t = jnp.transpose(w_in, (0, 2, 1))
    n_f0 = 3 * CONV_DIM + 3 * ATTN_DIM
    n_z0 = n_f0 + FOX_HEADS
    n_dt0 = n_z0 + SSD_DIM + XBC_DIM
    pad = jnp.zeros((DEPTH, PROJ_PAD - PROJ_MAIN - FOX_HEADS - SSD_HEADS, D_MODEL), F32)
    return jnp.concatenate([w_t[:, :n_f0], w_t[:, n_z0:n_dt0], w_t[:, n_f0:n_z0], w_t[:, n_dt0:], pad],
                           axis=1).astype(BF16)


def _layer_params(l, p):
    pad = jnp.zeros((LANES - FOX_HEADS - SSD_HEADS,), F32)
    a_neg = -jnp.exp(p['ssd_A_log'][l])
    return dict(
        fb=jnp.concatenate([p['fox_f_bias'][l], p['ssd_dt_bias'][l], pad])[None, :],
        qg=jnp.tile(p['fox_q_norm'][l], FOX_HEADS)[None, :],
        kg=jnp.tile(p['fox_k_norm'][l], FOX_HEADS)[None, :],
        av=jnp.concatenate([jnp.zeros((FOX_HEADS,), F32), a_neg, pad])[None, :],
        ac=a_neg[:, None],
        dv=jnp.repeat(p['ssd_D'][l], HEAD_DIM)[None, :],
        xa_qg=jnp.tile(p['xa_q_norm'][l], MEM_HEADS)[None, :],
    )


def _token_mix(l, x, p, lp, nseq, seq, conv_state, ssd_conv_state, ssd_state, paged):
    n = nseq * seq
    scb, cin, qa, kn, ka, v, va, z, xbc, fdt = _inproj(
        x, p['mix_norm'][l][None, :], p['w_in_t'][l], lp['qg'], lp['kg'], lp['fb'], p['pm'], p['pc'], seq)
    logf = fdt[:, F_LANE0:F_LANE0 + FOX_HEADS].reshape(nseq, seq, FOX_HEADS)
    dt = fdt[:, DT_LANE0:DT_LANE0 + SSD_HEADS].reshape(nseq, seq, SSD_HEADS)

    if paged is None:
        attn = _fox_prompt(qa, ka, va, nseq, seq)
    else:
        page_table, cache_kt, cache_vt, cache_lf_t = paged
        eye = jnp.eye(FOX_HEADS, dtype=BF16)
        q4 = jnp.transpose(qa.reshape(nseq, seq, FOX_HEADS, LANES)[..., :HEAD_DIM], (0, 2, 1, 3))
        qbd = (q4[:, :, :, None, :] * eye[None, :, None, :, None]).reshape(nseq, FOX_HEADS * seq, ATTN_DIM)
        tpad = ((0, 0), (0, 0), (0, PAGE_SIZE - seq))
        kn_t = jnp.pad(jnp.transpose(kn.reshape(nseq, seq, ATTN_DIM), (0, 2, 1)), tpad)
        vn_t = jnp.pad(jnp.transpose(v.reshape(nseq, seq, ATTN_DIM), (0, 2, 1)), tpad)
        lfn_t = jnp.pad(jnp.transpose(logf, (0, 2, 1)), tpad)
        attn = _fox_paged(l, page_table, qbd, kn_t, vn_t, lfn_t, cache_kt, cache_vt, cache_lf_t,
                          nseq, seq).reshape(n, ATTN_DIM)

    yssd, h_end = _ssd(xbc, _pad_state(ssd_conv_state, SSD_CONV), fdt, jnp.transpose(dt, (0, 2, 1)), z,
                       ssd_state, p['ssd_conv_w'][l], p['ssd_conv_b'][l][None, :], lp['av'], lp['ac'],
                       lp['dv'], nseq, seq)
    x = _merge(x, scb, cin, _pad_state(conv_state, SC_WIDTH), attn, yssd, p['sc_conv_w'][l],
               p['mix_out_norm'][l][None, :], p['w_out_bf'][l], nseq, seq)

    cin3 = cin.reshape(nseq, seq, CONV_DIM)
    xbc3 = xbc.reshape(nseq, seq, XBC_DIM)
    new_conv = jnp.concatenate([conv_state, cin3], axis=1)[:, -(SC_WIDTH - 1):]
    new_ssd_conv = jnp.concatenate([ssd_conv_state, xbc3], axis=1)[:, -(SSD_CONV - 1):]
    state = (kn.reshape(nseq, seq, FOX_HEADS, HEAD_DIM), v.reshape(nseq, seq, FOX_HEADS, HEAD_DIM),
             logf, new_conv, new_ssd_conv, h_end)
    return x, state


def kernel(x_prompt, x_sample, cache_fox_k, cache_fox_v, cache_fox_logf, cache_mem_k, cache_mem_v,
           state_conv, state_ssd_conv, state_ssd, page_table, mem_prompt,
           ffn1_norm, ffn1_wg, ffn1_wu, ffn1_wd, mix_norm, w_in, sc_conv_w, fox_q_norm, fox_k_norm,
           fox_f_bias, ssd_conv_w, ssd_conv_b, ssd_dt_bias, ssd_A_log, ssd_D, mix_out_norm, w_out,
           xa_norm, mem_norm, xa_wq, xa_wk, xa_wv, xa_q_norm, xa_k_norm, xa_wo,
           ffn2_norm, ffn2_wg, ffn2_wu, ffn2_wd, final_norm):
    b_p, s_p, _ = x_prompt.shape
    b_s, s_s, _ = x_sample.shape
    n_phys = cache_fox_k.shape[1]
    pm, pc = _bias_placement()
    p = dict(mix_norm=mix_norm, w_in_t=_prep_w_in(w_in), sc_conv_w=sc_conv_w, fox_q_norm=fox_q_norm,
             fox_k_norm=fox_k_norm, fox_f_bias=fox_f_bias, ssd_conv_w=ssd_conv_w, ssd_conv_b=ssd_conv_b,
             ssd_dt_bias=ssd_dt_bias, ssd_A_log=ssd_A_log, ssd_D=ssd_D, mix_out_norm=mix_out_norm,
             xa_q_norm=xa_q_norm, w_out_bf=w_out.astype(BF16), pm=pm, pc=pc)
    wg1, wu1, wd1 = ffn1_wg.astype(BF16), ffn1_wu.astype(BF16), ffn1_wd.astype(BF16)
    wg2, wu2, wd2 = ffn2_wg.astype(BF16), ffn2_wu.astype(BF16), ffn2_wd.astype(BF16)
    wq, wo = xa_wq.astype(BF16), xa_wo.astype(BF16)
    fg = final_norm[None, :]

    km_p, vm_p = _memkv(mem_prompt, mem_norm[:, None, :], xa_wk.astype(BF16), xa_wv.astype(BF16),
                        jnp.tile(xa_k_norm, (1, MEM_HEADS))[:, None, :])
    km_s = cache_mem_k.reshape(DEPTH, b_s, MEM_LEN, MEM_DIM)
    vm_s = cache_mem_v.reshape(DEPTH, b_s, MEM_LEN, MEM_DIM)
    cache_kt = jnp.transpose(cache_fox_k, (0, 1, 3, 4, 2)).reshape(DEPTH, n_phys, ATTN_DIM, PAGE_SIZE)
    cache_vt = jnp.transpose(cache_fox_v, (0, 1, 3, 4, 2)).reshape(DEPTH, n_phys, ATTN_DIM, PAGE_SIZE)
    cache_lf_t = jnp.transpose(cache_fox_logf, (0, 1, 3, 2))

    zero_conv = jnp.zeros((b_p, SC_WIDTH - 1, CONV_DIM), F32)
    zero_ssd_conv = jnp.zeros((b_p, SSD_CONV - 1, XBC_DIM), F32)
    zero_ssd = jnp.zeros((b_p, SSD_HEADS, HEAD_DIM, SSD_STATE), F32)

    xp = x_prompt.reshape(b_p * s_p, D_MODEL)
    xs = x_sample.reshape(b_s * s_s, D_MODEL)
    st_p, st_s = [], []
    for l in range(DEPTH):
        lp = _layer_params(l, p)
        last = l == DEPTH - 1
        groups = []
        for x, nseq, seq, km, vm, cs, scs, ss, paged in (
                (xp, b_p, s_p, km_p, vm_p, zero_conv, zero_ssd_conv, zero_ssd, None),
                (xs, b_s, s_s, km_s, vm_s, state_conv[l], state_ssd_conv[l], state_ssd[l],
                 (page_table, cache_kt, cache_vt, cache_lf_t))):
            x = _ffn(x, ffn1_norm[l][None, :], wg1[l], wu1[l], wd1[l], fg, False)
            x, st = _token_mix(l, x, p, lp, nseq, seq, cs, scs, ss, paged)
            x = _xattn(x, km, vm, l, xa_norm[l][None, :], wq[l], lp['xa_qg'], wo[l], nseq, seq)
            x = _ffn(x, ffn2_norm[l][None, :], wg2[l], wu2[l], wd2[l], fg, last)
            groups.append((x, st))
        (xp, sp), (xs, ss_) = groups
        st_p.append(sp)
        st_s.append(ss_)

    def stack(states, idx):
        return jnp.stack([s[idx] for s in states])

    return (xp.reshape(b_p, s_p, D_MODEL), xs.reshape(b_s, s_s, D_MODEL),
            stack(st_p, 0), stack(st_p, 1), stack(st_p, 2), km_p.reshape(DEPTH, b_p, MEM_LEN, MEM_HEADS, MEM_HEAD_DIM),
            vm_p.reshape(DEPTH, b_p, MEM_LEN, MEM_HEADS, MEM_HEAD_DIM),
            stack(st_p, 3), stack(st_p, 4), stack(st_p, 5),
            stack(st_s, 0), stack(st_s, 1), stack(st_s, 2), stack(st_s, 3), stack(st_s, 4), stack(st_s, 5))
```

```python
import functools

import numpy as np
import jax
import jax.numpy as jnp
from jax import lax
from jax.experimental import pallas as pl
from jax.experimental.pallas import tpu as pltpu

F32 = jnp.float32
BF16 = jnp.bfloat16

D_MODEL = 1024
DEPTH = 4
PAGE_SIZE = 128
HEAD_DIM = 64
CONV_DIM = 256
ATTN_DIM = 512
SSD_DIM = 256
FOX_HEADS = 8
SC_WIDTH = 3
SSD_HEADS = 4
SSD_STATE = 64
SSD_CONV = 4
SSD_CHUNK = 128
XBC_DIM = 512
FFN_DIM = 2816
MEM_LEN = 256
MEM_HEADS = 4
MEM_HEAD_DIM = 128
MEM_DIM = 512
RMS_EPS = 1e-6
LOG2E = 1.4426950408889634

LANES = 128
SUBLANES = 8
VMEM_LIMIT = 56 * 1024 * 1024
ROW_TILE = 512
PAGES_PER_STEP = 16
XATTN_SEQS_PER_STEP = 8
PROJ_MAIN = 3 * CONV_DIM + 3 * ATTN_DIM + SSD_DIM + XBC_DIM
PROJ_PAD = PROJ_MAIN + LANES
F_LANE0 = 0
DT_LANE0 = FOX_HEADS
PAD_DIM = FOX_HEADS * LANES
BIAS_LANE0 = HEAD_DIM
N_SPLIT = 3


def _params(sem):
    return pltpu.CompilerParams(dimension_semantics=sem, vmem_limit_bytes=VMEM_LIMIT)


def _resident(shape):
    nd = len(shape)
    return pl.BlockSpec(shape, lambda *_: (0,) * nd, pipeline_mode=pl.Buffered(1))


def _resident_layer(shape, layer):
    nd = len(shape)
    return pl.BlockSpec((None,) + tuple(shape), lambda *_: (layer,) + (0,) * nd, pipeline_mode=pl.Buffered(1))


def _rms(x, g):
    ms = jnp.mean(x * x, axis=-1, keepdims=True)
    return x * lax.rsqrt(ms + RMS_EPS) * g


def _head_rms(x, g, head_dim):
    width = x.shape[-1]
    pieces = []
    for c in range(width // LANES):
        blk = x[:, c * LANES:(c + 1) * LANES]
        sq = blk * blk
        s_all = jnp.sum(sq, axis=-1, keepdims=True)
        if head_dim == LANES:
            ms = s_all * (1.0 / LANES)
        else:
            lo = lax.broadcasted_iota(jnp.int32, sq.shape, 1) < head_dim
            s_lo = jnp.sum(jnp.where(lo, sq, 0.0), axis=-1, keepdims=True)
            ms = jnp.where(lo, s_lo, s_all - s_lo) * (1.0 / head_dim)
        pieces.append(blk * lax.rsqrt(ms + RMS_EPS))
    return jnp.concatenate(pieces, axis=-1) * g


def _silu(x):
    return x * jax.nn.sigmoid(x)


def _softplus(x):
    return jnp.maximum(x, 0.0) + jnp.log1p(jnp.exp(-jnp.abs(x)))


def _split3(x):
    hi = x.astype(BF16)
    r1 = x - hi.astype(F32)
    mid = r1.astype(BF16)
    lo = (r1 - mid.astype(F32)).astype(BF16)
    return hi, mid, lo


def _dot_exact_rhs(x, m_bf16):
    hi, mid, lo = _split3(x)
    out = jnp.dot(hi, m_bf16, preferred_element_type=F32)
    out = out + jnp.dot(mid, m_bf16, preferred_element_type=F32)
    return out + jnp.dot(lo, m_bf16, preferred_element_type=F32)


def _dot_exact_lhs(m_bf16, x):
    hi, mid, lo = _split3(x)
    out = jnp.dot(m_bf16, hi, preferred_element_type=F32)
    out = out + jnp.dot(m_bf16, mid, preferred_element_type=F32)
    return out + jnp.dot(m_bf16, lo, preferred_element_type=F32)


def _lower_tri(n, seq=None):
    r = lax.broadcasted_iota(jnp.int32, (n, n), 0)
    c = lax.broadcasted_iota(jnp.int32, (n, n), 1)
    keep = c <= r
    if seq is not None and seq < n:
        shift = jnp.int32(seq.bit_length() - 1)
        keep = keep & (lax.shift_right_logical(r, shift) == lax.shift_right_logical(c, shift))
    return keep.astype(BF16)


def _upper_tri(n):
    r = lax.broadcasted_iota(jnp.int32, (n, n), 0)
    c = lax.broadcasted_iota(jnp.int32, (n, n), 1)
    return (r <= c).astype(BF16)


def _nt(a, b):
    return lax.dot_general(a, b, (((1,), (1,)), ((), ())), preferred_element_type=F32)


def _tn(a, b):
    return lax.dot_general(a, b, (((0,), (0,)), ((), ())), preferred_element_type=F32)


def _lane_tile(x, reps):
    return x if reps == 1 else jnp.concatenate([x] * reps, axis=-1)


def _ffn_kernel(x_ref, g_ref, wg_ref, wu_ref, wd_ref, fg_ref, o_ref, *, final):
    x = x_ref[...]
    h = _rms(x, g_ref[...]).astype(BF16)
    a = jnp.dot(h, wg_ref[...], preferred_element_type=F32)
    b = jnp.dot(h, wu_ref[...], preferred_element_type=F32)
    t = (_silu(a) * b).astype(BF16)
    out = x + 0.5 * jnp.dot(t, wd_ref[...], preferred_element_type=F32)
    if final:
        out = _rms(out, fg_ref[...])
    o_ref[...] = out


def _ffn(x, g, wg, wu, wd, fg, layer, final):
    n = x.shape[0]
    tm = min(ROW_TILE // 2, n)
    return pl.pallas_call(
        functools.partial(_ffn_kernel, final=final),
        out_shape=jax.ShapeDtypeStruct((n, D_MODEL), F32),
        grid=(n // tm,),
        in_specs=[
            pl.BlockSpec((tm, D_MODEL), lambda i: (i, 0)),
            _resident((1, D_MODEL)),
            _resident_layer((D_MODEL, FFN_DIM), layer),
            _resident_layer((D_MODEL, FFN_DIM), layer),
            _resident_layer((FFN_DIM, D_MODEL), layer),
            _resident((1, D_MODEL)),
        ],
        out_specs=pl.BlockSpec((tm, D_MODEL), lambda i: (i, 0)),
        compiler_params=_params(("arbitrary",)),
        name="ffn",
    )(x, g, wg, wu, wd, fg)


def _expand_heads(x, fill):
    lo = lax.broadcasted_iota(jnp.int32, (x.shape[0], LANES), 1) < HEAD_DIM
    blocks = []
    for c in range(ATTN_DIM // LANES):
        blk = x[:, c * LANES:(c + 1) * LANES]
        for half, src in enumerate((blk, pltpu.roll(blk, HEAD_DIM, axis=1))):
            h = 2 * c + half
            blocks.append(jnp.where(lo, src, fill[:, h * LANES:(h + 1) * LANES]))
    return jnp.concatenate(blocks, axis=-1).astype(BF16)


def _inproj_kernel(x_ref, g_ref, w_ref, qg_ref, kg_ref, fb_ref, pm_ref, pc_ref,
                   scb_ref, cin_ref, qa_ref, kn_ref, ka_ref, v_ref, va_ref,
                   z_ref, xbc_ref, fdt_ref, carry_ref, *, tm, seq):
    h = _rms(x_ref[...], g_ref[...]).astype(BF16)

    def proj(a, b):
        return _nt(h, w_ref[a:b, :])

    o = 0
    scb_ref[...] = proj(o, o + CONV_DIM)
    o += CONV_DIM
    cin_ref[...] = proj(o, o + CONV_DIM) * proj(o + CONV_DIM, o + 2 * CONV_DIM)
    o += 2 * CONV_DIM
    q = _head_rms(proj(o, o + ATTN_DIM), qg_ref[...], HEAD_DIM) * (HEAD_DIM ** -0.5 * LOG2E)
    o += ATTN_DIM
    k = _head_rms(proj(o, o + ATTN_DIM), kg_ref[...], HEAD_DIM)
    kn_ref[...] = k
    o += ATTN_DIM
    v = proj(o, o + ATTN_DIM)
    v_ref[...] = v
    o += ATTN_DIM
    z_ref[...] = proj(o, o + SSD_DIM)
    o += SSD_DIM
    xbc_ref[...] = proj(o, o + XBC_DIM)
    o += XBC_DIM
    u = proj(o, o + LANES) + fb_ref[...]
    lane = lax.broadcasted_iota(jnp.int32, u.shape, 1)
    fdt = jnp.where(lane < DT_LANE0, -_softplus(-u), _softplus(u))
    fdt_ref[...] = fdt

    @pl.when(pl.program_id(0) % max(seq // tm, 1) == 0)
    def _():
        carry_ref[...] = jnp.zeros_like(carry_ref)

    lf = jnp.where(lane < FOX_HEADS, fdt, 0.0) * LOG2E
    c = _dot_exact_lhs(_lower_tri(tm, seq), lf) + carry_ref[0:1, :]
    carry_ref[...] = jnp.broadcast_to(c[tm - 1:tm, :], carry_ref.shape)
    hi, mid, lo = _split3(c)
    packed = (hi.astype(F32) + pltpu.roll(mid.astype(F32), FOX_HEADS, axis=1)
              + pltpu.roll(lo.astype(F32), 2 * FOX_HEADS, axis=1)).astype(BF16)
    bias = jnp.dot(packed, pm_ref[...], preferred_element_type=F32) + pc_ref[:, 0:2 * PAD_DIM]
    qa_ref[...] = _expand_heads(q, bias[:, 0:PAD_DIM])
    ka_ref[...] = _expand_heads(k, bias[:, PAD_DIM:2 * PAD_DIM])
    va_ref[...] = _expand_heads(v, pc_ref[:, 2 * PAD_DIM:3 * PAD_DIM])


def _bias_placement():
    pm = np.zeros((LANES, 2 * PAD_DIM), np.float32)
    pc = np.zeros((1, 3 * PAD_DIM), np.float32)
    for h in range(FOX_HEADS):
        for part in range(N_SPLIT):
            src = part * FOX_HEADS + h
            pm[src, h * LANES + BIAS_LANE0 + part] = 1.0
            pm[src, PAD_DIM + h * LANES + BIAS_LANE0 + N_SPLIT + part] = -1.0
            pc[0, h * LANES + BIAS_LANE0 + N_SPLIT + part] = 1.0
            pc[0, PAD_DIM + h * LANES + BIAS_LANE0 + part] = 1.0
        pc[0, 2 * PAD_DIM + h * LANES + HEAD_DIM] = 1.0
    return jnp.asarray(pm, BF16), jnp.asarray(pc, F32)


def _inproj(x, g, w, qg, kg, fb, pm, pc, layer, seq):
    n = x.shape[0]
    tm = min(ROW_TILE, n)
    widths = [(CONV_DIM, F32), (CONV_DIM, F32), (PAD_DIM, BF16), (ATTN_DIM, F32), (PAD_DIM, BF16),
              (ATTN_DIM, F32), (PAD_DIM, BF16), (SSD_DIM, F32), (XBC_DIM, F32), (LANES, F32)]
    return pl.pallas_call(
        functools.partial(_inproj_kernel, tm=tm, seq=seq),
        out_shape=[jax.ShapeDtypeStruct((n, w_), dt) for w_, dt in widths],
        grid=(n // tm,),
        in_specs=[
            pl.BlockSpec((tm, D_MODEL), lambda i: (i, 0)),
            _resident((1, D_MODEL)),
            _resident_layer((PROJ_PAD, D_MODEL), layer),
            _resident((1, ATTN_DIM)),
            _resident((1, ATTN_DIM)),
            _resident((1, LANES)),
            _resident((LANES, 2 * PAD_DIM)),
            _resident((1, 3 * PAD_DIM)),
        ],
        out_specs=[pl.BlockSpec((tm, w_), lambda i: (i, 0)) for w_, _ in widths],
        scratch_shapes=[pltpu.VMEM((SUBLANES, LANES), F32)],
        compiler_params=_params(("arbitrary",)),
        name="inproj",
    )(x, g, w, qg, kg, fb, pm, pc)


def _fox_kernel(q_ref, k_ref, v_ref, o_ref, m_ref, acc_ref, *, tq):
    i = pl.program_id(2)
    m_ref[...] = jnp.full(m_ref.shape, -jnp.inf, F32)
    acc_ref[...] = jnp.zeros_like(acc_ref)

    def tile(start, width, masked):
        rows = pl.ds(pl.multiple_of(start, width), width)
        for hh in range(2):
            sl = slice(hh * LANES, (hh + 1) * LANES)
            s = _nt(q_ref[:, sl], k_ref[rows, sl])
            if masked:
                row = lax.broadcasted_iota(jnp.int32, s.shape, 0)
                col = lax.broadcasted_iota(jnp.int32, s.shape, 1)
                s = jnp.where(col <= row, s, -jnp.inf)
            m_old = m_ref[hh]
            m_new = jnp.maximum(m_old, jnp.max(s, axis=-1, keepdims=True))
            p = jnp.exp2(s - _lane_tile(m_new, width // LANES))
            acc_ref[hh] = jnp.exp2(m_old - m_new) * acc_ref[hh] + jnp.dot(
                p.astype(BF16), v_ref[rows, sl], preferred_element_type=F32)
            m_ref[hh] = m_new

    def body(j, carry):
        tile(j * (2 * tq), 2 * tq, False)
        return carry

    lax.fori_loop(0, lax.shift_right_logical(i, 1), body, 0)

    @pl.when(lax.bitwise_and(i, 1) == 1)
    def _():
        tile((i - 1) * tq, tq, False)

    tile(i * tq, tq, True)
    outs = []
    for hh in range(2):
        a = acc_ref[hh]
        outs.append(a[:, 0:HEAD_DIM] / a[:, HEAD_DIM:HEAD_DIM + 1])
    o_ref[...] = jnp.concatenate(outs, axis=-1)


def _fox_prompt(qa, ka, va, nseq, seq):
    tq = min(ROW_TILE, seq)
    nq = seq // tq
    pair = 2 * LANES
    return pl.pallas_call(
        functools.partial(_fox_kernel, tq=tq),
        out_shape=jax.ShapeDtypeStruct((nseq * seq, ATTN_DIM), F32),
        grid=(nseq, FOX_HEADS // 2, nq),
        in_specs=[
            pl.BlockSpec((tq, pair), lambda b, h, i: (b * nq + i, h)),
            pl.BlockSpec((seq, pair), lambda b, h, i: (b, h)),
            pl.BlockSpec((seq, pair), lambda b, h, i: (b, h)),
        ],
        out_specs=pl.BlockSpec((tq, LANES), lambda b, h, i: (b * nq + i, h)),
        scratch_shapes=[pltpu.VMEM((2, tq, LANES), F32), pltpu.VMEM((2, tq, LANES), F32)],
        compiler_params=_params(("arbitrary",) * 3),
        name="fox_prompt",
    )(qa, ka, va)


def _fox_paged_kernel(pt_ref, q_ref, kn_ref, vn_ref, lfn_ref, *rest, tokens):
    npg = PAGES_PER_STEP
    k_refs = rest[:npg]
    v_refs = rest[npg:2 * npg]
    lf_refs = rest[2 * npg:3 * npg]
    o_ref, m_ref, l_ref, acc_ref, carry_ref = rest[3 * npg:]
    g = pl.program_id(1)
    rows = FOX_HEADS * tokens

    @pl.when(g == 0)
    def _():
        m_ref[...] = jnp.full(m_ref.shape, -jnp.inf, F32)
        l_ref[...] = jnp.zeros_like(l_ref)
        acc_ref[...] = jnp.zeros_like(acc_ref)
        carry_ref[...] = jnp.zeros_like(carry_ref)

    q = q_ref[...]
    tri = _upper_tri(PAGE_SIZE)

    def update(k_list, v_list, lf_list, masked):
        n = len(k_list)
        cw = _dot_exact_rhs(jnp.concatenate(lf_list, axis=0) * LOG2E, tri)
        off = carry_ref[:, 0:1]
        s_list = []
        for idx in range(n):
            c = cw[idx * FOX_HEADS:(idx + 1) * FOX_HEADS, :] + off
            off = c[:, PAGE_SIZE - 1:PAGE_SIZE]
            c_rows = jnp.broadcast_to(c[:, None, :], (FOX_HEADS, tokens, PAGE_SIZE)).reshape(rows, PAGE_SIZE)
            s_list.append(jnp.dot(q, k_list[idx].astype(BF16), preferred_element_type=F32) - c_rows)
        carry_ref[...] = jnp.broadcast_to(off, carry_ref.shape)
        s = _lane_tile(s_list[0], 1) if n == 1 else jnp.concatenate(s_list, axis=-1)
        if masked:
            t_idx = lax.broadcasted_iota(jnp.int32, (FOX_HEADS, tokens, PAGE_SIZE), 1).reshape(rows, PAGE_SIZE)
            col = lax.broadcasted_iota(jnp.int32, s.shape, 1)
            s = jnp.where(col <= t_idx, s, -jnp.inf)
        m_old = m_ref[...]
        m_new = jnp.maximum(m_old, jnp.max(s, axis=-1, keepdims=True))
        alpha = jnp.exp2(m_old - m_new)
        p = jnp.exp2(s - _lane_tile(m_new, n))
        l_ref[...] = alpha * l_ref[...] + jnp.sum(p, axis=-1, keepdims=True)
        pv = _nt(p[:, 0:PAGE_SIZE].astype(BF16), v_list[0].astype(BF16))
        for idx in range(1, n):
            pv = pv + _nt(p[:, idx * PAGE_SIZE:(idx + 1) * PAGE_SIZE].astype(BF16), v_list[idx].astype(BF16))
        acc_ref[...] = _lane_tile(alpha, ATTN_DIM // LANES) * acc_ref[...] + pv
        m_ref[...] = m_new

    update([r[...] for r in k_refs], [r[...] for r in v_refs], [r[...] for r in lf_refs], False)

    @pl.when(g == pl.num_programs(1) - 1)
    def _():
        update([kn_ref[...]], [vn_ref[...]], [lfn_ref[...]], True)
        full = acc_ref[...] / _lane_tile(l_ref[...], ATTN_DIM // LANES)
        lane_head = lax.shift_right_logical(
            lax.broadcasted_iota(jnp.int32, (tokens, ATTN_DIM), 1), jnp.int32(HEAD_DIM.bit_length() - 1))
        out = jnp.zeros((tokens, ATTN_DIM), F32)
        for h in range(FOX_HEADS):
            out = out + jnp.where(lane_head == h, full[h * tokens:(h + 1) * tokens, :], 0.0)
        o_ref[...] = out


def _fox_paged(layer, page_table, qbd, kn_t, vn_t, lfn_t, cache_kt, cache_vt, cache_lf_t, nseq, tokens):
    npg = PAGES_PER_STEP
    n_groups = page_table.shape[1] // npg
    rows = FOX_HEADS * tokens

    def page_spec(width2, n):
        return pl.BlockSpec((None, None, width2, PAGE_SIZE),
                            lambda b, g, pt: (layer, pt[b, g * npg + n], 0, 0))

    in_specs = [
        pl.BlockSpec((None, rows, ATTN_DIM), lambda b, g, pt: (b, 0, 0)),
        pl.BlockSpec((None, ATTN_DIM, PAGE_SIZE), lambda b, g, pt: (b, 0, 0)),
        pl.BlockSpec((None, ATTN_DIM, PAGE_SIZE), lambda b, g, pt: (b, 0, 0)),
        pl.BlockSpec((None, FOX_HEADS, PAGE_SIZE), lambda b, g, pt: (b, 0, 0)),
    ]
    in_specs += [page_spec(ATTN_DIM, n) for n in range(npg)]
    in_specs += [page_spec(ATTN_DIM, n) for n in range(npg)]
    in_specs += [page_spec(FOX_HEADS, n) for n in range(npg)]
    grid_spec = pltpu.PrefetchScalarGridSpec(
        num_scalar_prefetch=1,
        grid=(nseq, n_groups),
        in_specs=in_specs,
        out_specs=pl.BlockSpec((None, tokens, ATTN_DIM), lambda b, g, pt: (b, 0, 0)),
        scratch_shapes=[pltpu.VMEM((rows, LANES), F32), pltpu.VMEM((rows, LANES), F32),
                        pltpu.VMEM((rows, ATTN_DIM), F32), pltpu.VMEM((FOX_HEADS, LANES), F32)],
    )
    return pl.pallas_call(
        functools.partial(_fox_paged_kernel, tokens=tokens),
        out_shape=jax.ShapeDtypeStruct((nseq, tokens, ATTN_DIM), F32),
        grid_spec=grid_spec,
        compiler_params=_params(("arbitrary", "arbitrary")),
        name="fox_paged",
    )(page_table, qbd, kn_t, vn_t, lfn_t,
      *([cache_kt] * npg), *([cache_vt] * npg), *([cache_lf_t] * npg))


def _ssd_kernel(xbc_ref, halo_ref, st_ref, fdt_ref, dtt_ref, z_ref, h0_ref,
                cw_ref, cb_ref, av_ref, ac_ref, dv_ref,
                y_ref, hout_ref, buf_ref, h_ref, *, lc):
    j = pl.program_id(1)

    @pl.when(j == 0)
    def _():
        h_ref[...] = h0_ref[...]
        buf_ref[0:SUBLANES, :] = st_ref[...]

    @pl.when(j > 0)
    def _():
        buf_ref[0:SUBLANES, :] = halo_ref[...]

    buf_ref[SUBLANES:SUBLANES + lc, :] = xbc_ref[...]
    base = SUBLANES - (SSD_CONV - 1)
    u = buf_ref[base:base + lc, :] * cw_ref[0:1, :]
    for k in range(1, SSD_CONV):
        u = u + buf_ref[base + k:base + k + lc, :] * cw_ref[k:k + 1, :]
    u = _silu(u + cb_ref[...])

    fdt = fdt_ref[...]
    acum = _dot_exact_lhs(_lower_tri(lc), fdt * av_ref[...])
    dtt = dtt_ref[...]
    acum_t = _dot_exact_rhs(dtt * ac_ref[...], _upper_tri(lc))
    row = lax.broadcasted_iota(jnp.int32, (lc, lc), 0)
    col = lax.broadcasted_iota(jnp.int32, (lc, lc), 1)
    causal = col <= row
    z = z_ref[...]
    dvec = dv_ref[...]

    outs = []
    for h in range(SSD_HEADS):
        grp = h // (SSD_HEADS // 2)
        xh = u[:, h * HEAD_DIM:(h + 1) * HEAD_DIM]
        bh = u[:, SSD_DIM + grp * SSD_STATE:SSD_DIM + (grp + 1) * SSD_STATE]
        ch = u[:, SSD_DIM + 2 * SSD_STATE + grp * SSD_STATE:SSD_DIM + 2 * SSD_STATE + (grp + 1) * SSD_STATE]
        lane = DT_LANE0 + h
        a_col = acum[:, lane:lane + 1]
        dt_col = fdt[:, lane:lane + 1]
        a_row = acum_t[h:h + 1, :]
        dt_row = dtt[h:h + 1, :]
        a_last = acum[lc - 1:lc, lane:lane + 1]
        decay = jnp.exp(jnp.where(causal, a_col - a_row, -jnp.inf))
        xb = xh.astype(BF16)
        bb = bh.astype(BF16)
        cb16 = ch.astype(BF16)
        w = _nt(cb16, bb) * decay * dt_row
        h_prev = h_ref[h]
        y = jnp.dot(w.astype(BF16), xb, preferred_element_type=F32)
        y = y + _nt(cb16, h_prev.astype(BF16)) * jnp.exp(a_col)
        w_end = jnp.exp(a_last - a_col) * dt_col
        h_ref[h] = h_prev * jnp.exp(a_last) + _tn((xh * w_end).astype(BF16), bb)
        sl = slice(h * HEAD_DIM, (h + 1) * HEAD_DIM)
        outs.append((y + dvec[:, sl] * xh) * _silu(z[:, sl]))
    y_ref[...] = jnp.concatenate(outs, axis=-1)

    @pl.when(j == pl.num_programs(1) - 1)
    def _():
        hout_ref[...] = h_ref[...]


def _ssd(xbc, st_pad, fdt, dtt, z, h0, cw, cb, av, ac, dv, nseq, seq):
    lc = min(SSD_CHUNK, seq)
    nc = seq // lc
    hb = lc // SUBLANES
    return pl.pallas_call(
        functools.partial(_ssd_kernel, lc=lc),
        out_shape=[jax.ShapeDtypeStruct((nseq * seq, SSD_DIM), F32),
                   jax.ShapeDtypeStruct((nseq, SSD_HEADS, HEAD_DIM, SSD_STATE), F32)],
        grid=(nseq, nc),
        in_specs=[
            pl.BlockSpec((lc, XBC_DIM), lambda b, j: (b * nc + j, 0)),
            pl.BlockSpec((SUBLANES, XBC_DIM), lambda b, j: (jnp.maximum((b * nc + j) * hb - 1, 0), 0)),
            pl.BlockSpec((None, SUBLANES, XBC_DIM), lambda b, j: (b, 0, 0)),
            pl.BlockSpec((lc, LANES), lambda b, j: (b * nc + j, 0)),
            pl.BlockSpec((None, SSD_HEADS, lc), lambda b, j: (b, 0, j)),
            pl.BlockSpec((lc, SSD_DIM), lambda b, j: (b * nc + j, 0)),
            pl.BlockSpec((None, SSD_HEADS, HEAD_DIM, SSD_STATE), lambda b, j: (b, 0, 0, 0)),
            _resident((SSD_CONV, XBC_DIM)),
            _resident((1, XBC_DIM)),
            _resident((1, LANES)),
            _resident((SSD_HEADS, 1)),
            _resident((1, SSD_DIM)),
        ],
        out_specs=[pl.BlockSpec((lc, SSD_DIM), lambda b, j: (b * nc + j, 0)),
                   pl.BlockSpec((None, SSD_HEADS, HEAD_DIM, SSD_STATE), lambda b, j: (b, 0, 0, 0))],
        scratch_shapes=[pltpu.VMEM((SUBLANES + lc, XBC_DIM), F32),
                        pltpu.VMEM((SSD_HEADS, HEAD_DIM, SSD_STATE), F32)],
        compiler_params=_params(("arbitrary", "arbitrary")),
        name="ssd",
    )(xbc, xbc, st_pad, fdt, dtt, z, h0, cw, cb, av, ac, dv)


def _merge_kernel(x_ref, scb_ref, cin_ref, halo_ref, st_ref, attn_ref, yssd_ref,
                  cw_ref, g_ref, w_ref, o_ref, buf_ref, *, tl):
    j = pl.program_id(1)

    @pl.when(j == 0)
    def _():
        buf_ref[0:SUBLANES, :] = st_ref[...]

    @pl.when(j > 0)
    def _():
        buf_ref[0:SUBLANES, :] = halo_ref[...]

    buf_ref[SUBLANES:SUBLANES + tl, :] = cin_ref[...]
    base = SUBLANES - (SC_WIDTH - 1)
    y = buf_ref[base:base + tl, :] * cw_ref[0:1, :]
    for k in range(1, SC_WIDTH):
        y = y + buf_ref[base + k:base + k + tl, :] * cw_ref[k:k + 1, :]
    cat = jnp.concatenate([scb_ref[...] * y, attn_ref[...], yssd_ref[...]], axis=-1)
    cat = _head_rms(cat, g_ref[...], HEAD_DIM).astype(BF16)
    o_ref[...] = x_ref[...] + jnp.dot(cat, w_ref[...], preferred_element_type=F32)


def _merge(x, scb, cin, st_pad, attn, yssd, cw, g, w, layer, nseq, seq):
    tl = min(ROW_TILE, seq)
    nt = seq // tl
    hb = tl // SUBLANES

    def rows(width):
        return pl.BlockSpec((tl, width), lambda b, j: (b * nt + j, 0))

    return pl.pallas_call(
        functools.partial(_merge_kernel, tl=tl),
        out_shape=jax.ShapeDtypeStruct((nseq * seq, D_MODEL), F32),
        grid=(nseq, nt),
        in_specs=[
            rows(D_MODEL), rows(CONV_DIM), rows(CONV_DIM),
            pl.BlockSpec((SUBLANES, CONV_DIM), lambda b, j: (jnp.maximum((b * nt + j) * hb - 1, 0), 0)),
            pl.BlockSpec((None, SUBLANES, CONV_DIM), lambda b, j: (b, 0, 0)),
            rows(ATTN_DIM), rows(SSD_DIM),
            _resident((SC_WIDTH, CONV_DIM)),
            _resident((1, D_MODEL)),
            _resident_layer((D_MODEL, D_MODEL), layer),
        ],
        out_specs=rows(D_MODEL),
        scratch_shapes=[pltpu.VMEM((SUBLANES + tl, CONV_DIM), F32)],
        compiler_params=_params(("arbitrary", "arbitrary")),
        name="merge",
    )(x, scb, cin, cin, st_pad, attn, yssd, cw, g, w)


def _memkv_kernel(mem_ref, g_ref, wk_ref, wv_ref, kg_ref, km_ref, vm_ref):
    m = _rms(mem_ref[...], g_ref[...]).astype(BF16)
    km_ref[...] = _head_rms(jnp.dot(m, wk_ref[...], preferred_element_type=F32), kg_ref[...], MEM_HEAD_DIM)
    vm_ref[...] = jnp.dot(m, wv_ref[...], preferred_element_type=F32)


def _memkv(mem, g, wk, wv, kg):
    nb = mem.shape[0]
    out = jax.ShapeDtypeStruct((DEPTH, nb, MEM_LEN, MEM_DIM), F32)
    return pl.pallas_call(
        _memkv_kernel,
        out_shape=[out, out],
        grid=(DEPTH, nb),
        in_specs=[
            pl.BlockSpec((None, MEM_LEN, D_MODEL), lambda l, b: (b, 0, 0)),
            pl.BlockSpec((None, 1, D_MODEL), lambda l, b: (l, 0, 0)),
            pl.BlockSpec((None, D_MODEL, MEM_DIM), lambda l, b: (l, 0, 0)),
            pl.BlockSpec((None, D_MODEL, MEM_DIM), lambda l, b: (l, 0, 0)),
            pl.BlockSpec((None, 1, MEM_DIM), lambda l, b: (l, 0, 0)),
        ],
        out_specs=[pl.BlockSpec((None, None, MEM_LEN, MEM_DIM), lambda l, b: (l, b, 0, 0))] * 2,
        compiler_params=_params(("arbitrary", "arbitrary")),
        name="mem_kv",
    )(mem, g, wk, wv, kg)


def _xattn_kernel(x_ref, km_ref, vm_ref, g_ref, wq_ref, qg_ref, wo_ref, o_ref, q_ref, att_ref, *, bs, tl):
    x = x_ref[...]
    h = _rms(x, g_ref[...]).astype(BF16)
    q = _head_rms(jnp.dot(h, wq_ref[...], preferred_element_type=F32), qg_ref[...], MEM_HEAD_DIM)
    q_ref[...] = q * (MEM_HEAD_DIM ** -0.5)

    def one_sequence(s, carry):
        rows = pl.ds(pl.multiple_of(s * tl, tl), tl)
        for hh in range(MEM_HEADS):
            sl = slice(hh * MEM_HEAD_DIM, (hh + 1) * MEM_HEAD_DIM)
            sc = _nt(q_ref[rows, sl].astype(BF16), km_ref[s, :, hh, :].astype(BF16))
            p = jnp.exp(sc - jnp.max(sc, axis=-1, keepdims=True))
            p = p / jnp.sum(p, axis=-1, keepdims=True)
            att_ref[rows, sl] = jnp.dot(p.astype(BF16), vm_ref[s, :, hh, :].astype(BF16),
                                        preferred_element_type=F32)
        return carry

    lax.fori_loop(0, bs, one_sequence, 0)
    o_ref[...] = x + jnp.dot(att_ref[...].astype(BF16), wo_ref[...], preferred_element_type=F32)


def _xattn(x, km, vm, layer, g, wq, qg, wo, nseq, seq):
    tl = min(ROW_TILE, seq)
    nt = seq // tl
    bs = 1 if nt > 1 else min(nseq, XATTN_SEQS_PER_STEP)
    rows = bs * tl
    mem_block = (None, bs, MEM_LEN, MEM_HEADS, MEM_HEAD_DIM)
    return pl.pallas_call(
        functools.partial(_xattn_kernel, bs=bs, tl=tl),
        out_shape=jax.ShapeDtypeStruct((nseq * seq, D_MODEL), F32),
        grid=(nseq // bs, nt),
        in_specs=[
            pl.BlockSpec((rows, D_MODEL), lambda b, j: (b * nt + j, 0)),
            pl.BlockSpec(mem_block, lambda b, j: (layer, b, 0, 0, 0)),
            pl.BlockSpec(mem_block, lambda b, j: (layer, b, 0, 0, 0)),
            _resident((1, D_MODEL)),
            _resident_layer((D_MODEL, MEM_DIM), layer),
            _resident((1, MEM_DIM)),
            _resident_layer((MEM_DIM, D_MODEL), layer),
        ],
        out_specs=pl.BlockSpec((rows, D_MODEL), lambda b, j: (b * nt + j, 0)),
        scratch_shapes=[pltpu.VMEM((rows, MEM_DIM), F32), pltpu.VMEM((rows, MEM_DIM), F32)],
        compiler_params=_params(("arbitrary", "arbitrary")),
        name="xattn",
    )(x, km, vm, g, wq, qg, wo)


def _pad_state(state, width):
    nseq, _, c = state.shape
    return jnp.concatenate([jnp.zeros((nseq, SUBLANES - (width - 1), c), F32), state], axis=1)


def _prep_w_in(w_in):
    w_t = jnp.transpose(w_in, (0, 2, 1))
    n_f0 = 3 * CONV_DIM + 3 * ATTN_DIM
    n_z0 = n_f0 + FOX_HEADS
    n_dt0 = n_z0 + SSD_DIM + XBC_DIM
    pad = jnp.zeros((DEPTH, PROJ_PAD - PROJ_MAIN - FOX_HEADS - SSD_HEADS, D_MODEL), F32)
    return jnp.concatenate([w_t[:, :n_f0], w_t[:, n_z0:n_dt0], w_t[:, n_f0:n_z0], w_t[:, n_dt0:], pad],
                           axis=1).astype(BF16)


def _layer_params(l, p):
    pad = jnp.zeros((LANES - FOX_HEADS - SSD_HEADS,), F32)
    a_neg = -jnp.exp(p['ssd_A_log'][l])
    return dict(
        fb=jnp.concatenate([p['fox_f_bias'][l], p['ssd_dt_bias'][l], pad])[None, :],
        qg=jnp.tile(p['fox_q_norm'][l], FOX_HEADS)[None, :],
        kg=jnp.tile(p['fox_k_norm'][l], FOX_HEADS)[None, :],
        av=jnp.concatenate([jnp.zeros((FOX_HEADS,), F32), a_neg, pad])[None, :],
        ac=a_neg[:, None],
        dv=jnp.repeat(p['ssd_D'][l], HEAD_DIM)[None, :],
        xa_qg=jnp.tile(p['xa_q_norm'][l], MEM_HEADS)[None, :],
    )


def _token_mix(l, x, p, lp, nseq, seq, conv_state, ssd_conv_state, ssd_state, paged):
    n = nseq * seq
    scb, cin, qa, kn, ka, v, va, z, xbc, fdt = _inproj(
        x, p['mix_norm'][l][None, :], p['w_in_t'], lp['qg'], lp['kg'], lp['fb'], p['pm'], p['pc'], l, seq)
    logf = fdt[:, F_LANE0:F_LANE0 + FOX_HEADS].reshape(nseq, seq, FOX_HEADS)
    dt = fdt[:, DT_LANE0:DT_LANE0 + SSD_HEADS].reshape(nseq, seq, SSD_HEADS)

    if paged is None:
        attn = _fox_prompt(qa, ka, va, nseq, seq)
    else:
        page_table, cache_kt, cache_vt, cache_lf_t = paged
        eye = jnp.eye(FOX_HEADS, dtype=BF16)
        q4 = jnp.transpose(qa.reshape(nseq, seq, FOX_HEADS, LANES)[..., :HEAD_DIM], (0, 2, 1, 3))
        qbd = (q4[:, :, :, None, :] * eye[None, :, None, :, None]).reshape(nseq, FOX_HEADS * seq, ATTN_DIM)
        tpad = ((0, 0), (0, 0), (0, PAGE_SIZE - seq))
        kn_t = jnp.pad(jnp.transpose(kn.reshape(nseq, seq, ATTN_DIM), (0, 2, 1)), tpad)
        vn_t = jnp.pad(jnp.transpose(v.reshape(nseq, seq, ATTN_DIM), (0, 2, 1)), tpad)
        lfn_t = jnp.pad(jnp.transpose(logf, (0, 2, 1)), tpad)
        attn = _fox_paged(l, page_table, qbd, kn_t, vn_t, lfn_t, cache_kt, cache_vt, cache_lf_t,
                          nseq, seq).reshape(n, ATTN_DIM)

    yssd, h_end = _ssd(xbc, _pad_state(ssd_conv_state, SSD_CONV), fdt, jnp.transpose(dt, (0, 2, 1)), z,
                       ssd_state, p['ssd_conv_w'][l], p['ssd_conv_b'][l][None, :], lp['av'], lp['ac'],
                       lp['dv'], nseq, seq)
    x = _merge(x, scb, cin, _pad_state(conv_state, SC_WIDTH), attn, yssd, p['sc_conv_w'][l],
               p['mix_out_norm'][l][None, :], p['w_out_bf'], l, nseq, seq)

    cin3 = cin.reshape(nseq, seq, CONV_DIM)
    xbc3 = xbc.reshape(nseq, seq, XBC_DIM)
    new_conv = jnp.concatenate([conv_state, cin3], axis=1)[:, -(SC_WIDTH - 1):]
    new_ssd_conv = jnp.concatenate([ssd_conv_state, xbc3], axis=1)[:, -(SSD_CONV - 1):]
    state = (kn.reshape(nseq, seq, FOX_HEADS, HEAD_DIM), v.reshape(nseq, seq, FOX_HEADS, HEAD_DIM),
             logf, new_conv, new_ssd_conv, h_end)
    return x, state


def kernel(x_prompt, x_sample, cache_fox_k, cache_fox_v, cache_fox_logf, cache_mem_k, cache_mem_v,
           state_conv, state_ssd_conv, state_ssd, page_table, mem_prompt,
           ffn1_norm, ffn1_wg, ffn1_wu, ffn1_wd, mix_norm, w_in, sc_conv_w, fox_q_norm, fox_k_norm,
           fox_f_bias, ssd_conv_w, ssd_conv_b, ssd_dt_bias, ssd_A_log, ssd_D, mix_out_norm, w_out,
           xa_norm, mem_norm, xa_wq, xa_wk, xa_wv, xa_q_norm, xa_k_norm, xa_wo,
           ffn2_norm, ffn2_wg, ffn2_wu, ffn2_wd, final_norm):
    b_p, s_p, _ = x_prompt.shape
    b_s, s_s, _ = x_sample.shape
    n_phys = cache_fox_k.shape[1]
    pm, pc = _bias_placement()
    p = dict(mix_norm=mix_norm, w_in_t=_prep_w_in(w_in), sc_conv_w=sc_conv_w, fox_q_norm=fox_q_norm,
             fox_k_norm=fox_k_norm, fox_f_bias=fox_f_bias, ssd_conv_w=ssd_conv_w, ssd_conv_b=ssd_conv_b,
             ssd_dt_bias=ssd_dt_bias, ssd_A_log=ssd_A_log, ssd_D=ssd_D, mix_out_norm=mix_out_norm,
             xa_q_norm=xa_q_norm, w_out_bf=w_out.astype(BF16), pm=pm, pc=pc)
    wg1, wu1, wd1 = ffn1_wg.astype(BF16), ffn1_wu.astype(BF16), ffn1_wd.astype(BF16)
    wg2, wu2, wd2 = ffn2_wg.astype(BF16), ffn2_wu.astype(BF16), ffn2_wd.astype(BF16)
    wq, wo = xa_wq.astype(BF16), xa_wo.astype(BF16)
    fg = final_norm[None, :]

    km_p, vm_p = _memkv(mem_prompt, mem_norm[:, None, :], xa_wk.astype(BF16), xa_wv.astype(BF16),
                        jnp.tile(xa_k_norm, (1, MEM_HEADS))[:, None, :])
    km_p = km_p.reshape(DEPTH, b_p, MEM_LEN, MEM_HEADS, MEM_HEAD_DIM)
    vm_p = vm_p.reshape(DEPTH, b_p, MEM_LEN, MEM_HEADS, MEM_HEAD_DIM)
    km_s, vm_s = cache_mem_k, cache_mem_v
    cache_kt = jnp.transpose(cache_fox_k, (0, 1, 3, 4, 2)).reshape(DEPTH, n_phys, ATTN_DIM, PAGE_SIZE)
    cache_vt = jnp.transpose(cache_fox_v, (0, 1, 3, 4, 2)).reshape(DEPTH, n_phys, ATTN_DIM, PAGE_SIZE)
    cache_lf_t = jnp.transpose(cache_fox_logf, (0, 1, 3, 2))

    zero_conv = jnp.zeros((b_p, SC_WIDTH - 1, CONV_DIM), F32)
    zero_ssd_conv = jnp.zeros((b_p, SSD_CONV - 1, XBC_DIM), F32)
    zero_ssd = jnp.zeros((b_p, SSD_HEADS, HEAD_DIM, SSD_STATE), F32)

    xp = x_prompt.reshape(b_p * s_p, D_MODEL)
    xs = x_sample.reshape(b_s * s_s, D_MODEL)
    st_p, st_s = [], []
    for l in range(DEPTH):
        lp = _layer_params(l, p)
        last = l == DEPTH - 1
        groups = []
        for x, nseq, seq, km, vm, cs, scs, ss, paged in (
                (xp, b_p, s_p, km_p, vm_p, zero_conv, zero_ssd_conv, zero_ssd, None),
                (xs, b_s, s_s, km_s, vm_s, state_conv[l], state_ssd_conv[l], state_ssd[l],
                 (page_table, cache_kt, cache_vt, cache_lf_t))):
            x = _ffn(x, ffn1_norm[l][None, :], wg1, wu1, wd1, fg, l, False)
            x, st = _token_mix(l, x, p, lp, nseq, seq, cs, scs, ss, paged)
            x = _xattn(x, km, vm, l, xa_norm[l][None, :], wq, lp['xa_qg'], wo, nseq, seq)
            x = _ffn(x, ffn2_norm[l][None, :], wg2, wu2, wd2, fg, l, last)
            groups.append((x, st))
        (xp, sp), (xs, ss_) = groups
        st_p.append(sp)
        st_s.append(ss_)

    def stack(states, idx):
        return jnp.stack([s[idx] for s in states])

    return (xp.reshape(b_p, s_p, D_MODEL), xs.reshape(b_s, s_s, D_MODEL),
            stack(st_p, 0), stack(st_p, 1), stack(st_p, 2), km_p, vm_p,
            stack(st_p, 3), stack(st_p, 4), stack(st_p, 5),
            stack(st_s, 0), stack(st_s, 1), stack(st_s, 2), stack(st_s, 3), stack(st_s, 4), stack(st_s, 5))
```

```python
import functools

import numpy as np
import jax
import jax.numpy as jnp
from jax import lax
from jax.experimental import pallas as pl
from jax.experimental.pallas import tpu as pltpu

F32 = jnp.float32
BF16 = jnp.bfloat16

D_MODEL = 1024
DEPTH = 4
PAGE_SIZE = 128
HEAD_DIM = 64
CONV_DIM = 256
ATTN_DIM = 512
SSD_DIM = 256
FOX_HEADS = 8
SC_WIDTH = 3
SSD_HEADS = 4
SSD_STATE = 64
SSD_CONV = 4
SSD_CHUNK = 128
XBC_DIM = 512
FFN_DIM = 2816
MEM_LEN = 256
MEM_HEADS = 4
MEM_HEAD_DIM = 128
MEM_DIM = 512
RMS_EPS = 1e-6
LOG2E = 1.4426950408889634

LANES = 128
SUBLANES = 8
VMEM_LIMIT = 56 * 1024 * 1024
ROW_TILE = 512
PAGES_PER_STEP = 16
XATTN_SEQS_PER_STEP = 8
FOX_HEADS_PER_STEP = 4
PROJ_MAIN = 3 * CONV_DIM + 3 * ATTN_DIM + SSD_DIM + XBC_DIM
PROJ_PAD = PROJ_MAIN + LANES
F_LANE0 = 0
DT_LANE0 = FOX_HEADS
PAD_DIM = FOX_HEADS * LANES
BIAS_LANE0 = HEAD_DIM
N_SPLIT = 3


def _params(sem):
    return pltpu.CompilerParams(dimension_semantics=sem, vmem_limit_bytes=VMEM_LIMIT)


def _resident(shape):
    nd = len(shape)
    return pl.BlockSpec(shape, lambda *_: (0,) * nd, pipeline_mode=pl.Buffered(1))


def _resident_layer(shape, layer):
    nd = len(shape)
    return pl.BlockSpec((None,) + tuple(shape), lambda *_: (layer,) + (0,) * nd, pipeline_mode=pl.Buffered(1))


def _rms(x, g):
    ms = jnp.mean(x * x, axis=-1, keepdims=True)
    return x * lax.rsqrt(ms + RMS_EPS) * g


def _head_rms(x, g, head_dim):
    width = x.shape[-1]
    pieces = []
    for c in range(width // LANES):
        blk = x[:, c * LANES:(c + 1) * LANES]
        sq = blk * blk
        s_all = jnp.sum(sq, axis=-1, keepdims=True)
        if head_dim == LANES:
            ms = s_all * (1.0 / LANES)
        else:
            lo = lax.broadcasted_iota(jnp.int32, sq.shape, 1) < head_dim
            s_lo = jnp.sum(jnp.where(lo, sq, 0.0), axis=-1, keepdims=True)
            ms = jnp.where(lo, s_lo, s_all - s_lo) * (1.0 / head_dim)
        pieces.append(blk * lax.rsqrt(ms + RMS_EPS))
    return jnp.concatenate(pieces, axis=-1) * g


def _silu(x):
    return x * jax.nn.sigmoid(x)


def _softplus(x):
    return jnp.maximum(x, 0.0) + jnp.log1p(jnp.exp(-jnp.abs(x)))


def _split3(x):
    hi = x.astype(BF16)
    r1 = x - hi.astype(F32)
    mid = r1.astype(BF16)
    lo = (r1 - mid.astype(F32)).astype(BF16)
    return hi, mid, lo


def _dot_exact_rhs(x, m_bf16):
    hi, mid, lo = _split3(x)
    out = jnp.dot(hi, m_bf16, preferred_element_type=F32)
    out = out + jnp.dot(mid, m_bf16, preferred_element_type=F32)
    return out + jnp.dot(lo, m_bf16, preferred_element_type=F32)


def _dot_exact_lhs(m_bf16, x):
    hi, mid, lo = _split3(x)
    out = jnp.dot(m_bf16, hi, preferred_element_type=F32)
    out = out + jnp.dot(m_bf16, mid, preferred_element_type=F32)
    return out + jnp.dot(m_bf16, lo, preferred_element_type=F32)


def _lower_tri(n, seq=None):
    r = lax.broadcasted_iota(jnp.int32, (n, n), 0)
    c = lax.broadcasted_iota(jnp.int32, (n, n), 1)
    keep = c <= r
    if seq is not None and seq < n:
        shift = jnp.int32(seq.bit_length() - 1)
        keep = keep & (lax.shift_right_logical(r, shift) == lax.shift_right_logical(c, shift))
    return keep.astype(BF16)


def _upper_tri(n):
    r = lax.broadcasted_iota(jnp.int32, (n, n), 0)
    c = lax.broadcasted_iota(jnp.int32, (n, n), 1)
    return (r <= c).astype(BF16)


def _nt(a, b):
    return lax.dot_general(a, b, (((1,), (1,)), ((), ())), preferred_element_type=F32)


def _tn(a, b):
    return lax.dot_general(a, b, (((0,), (0,)), ((), ())), preferred_element_type=F32)


def _row_tile(x, reps):
    return x if reps == 1 else jnp.concatenate([x] * reps, axis=0)


def _lane_tile(x, reps):
    return x if reps == 1 else jnp.concatenate([x] * reps, axis=-1)


def _ffn_kernel(x_ref, g_ref, wg_ref, wu_ref, wd_ref, fg_ref, o_ref, *, final):
    x = x_ref[...]
    h = _rms(x, g_ref[...]).astype(BF16)
    a = jnp.dot(h, wg_ref[...], preferred_element_type=F32)
    b = jnp.dot(h, wu_ref[...], preferred_element_type=F32)
    t = (_silu(a) * b).astype(BF16)
    out = x + 0.5 * jnp.dot(t, wd_ref[...], preferred_element_type=F32)
    if final:
        out = _rms(out, fg_ref[...])
    o_ref[...] = out


def _ffn(x, g, wg, wu, wd, fg, layer, final):
    n = x.shape[0]
    tm = min(ROW_TILE // 2, n)
    return pl.pallas_call(
        functools.partial(_ffn_kernel, final=final),
        out_shape=jax.ShapeDtypeStruct((n, D_MODEL), F32),
        grid=(n // tm,),
        in_specs=[
            pl.BlockSpec((tm, D_MODEL), lambda i: (i, 0)),
            _resident((1, D_MODEL)),
            _resident_layer((D_MODEL, FFN_DIM), layer),
            _resident_layer((D_MODEL, FFN_DIM), layer),
            _resident_layer((FFN_DIM, D_MODEL), layer),
            _resident((1, D_MODEL)),
        ],
        out_specs=pl.BlockSpec((tm, D_MODEL), lambda i: (i, 0)),
        compiler_params=_params(("arbitrary",)),
        name="ffn",
    )(x, g, wg, wu, wd, fg)


def _expand_heads(x, fill):
    lo = lax.broadcasted_iota(jnp.int32, (x.shape[0], LANES), 1) < HEAD_DIM
    blocks = []
    for c in range(ATTN_DIM // LANES):
        blk = x[:, c * LANES:(c + 1) * LANES]
        for half, src in enumerate((blk, pltpu.roll(blk, HEAD_DIM, axis=1))):
            h = 2 * c + half
            blocks.append(jnp.where(lo, src, fill[:, h * LANES:(h + 1) * LANES]))
    return jnp.concatenate(blocks, axis=-1).astype(BF16)


def _inproj_kernel(x_ref, g_ref, w_ref, qg_ref, kg_ref, fb_ref, pm_ref, pc_ref,
                   scb_ref, cin_ref, qa_ref, kn_ref, ka_ref, v_ref, va_ref,
                   z_ref, xbc_ref, fdt_ref, carry_ref, *, tm, seq):
    h = _rms(x_ref[...], g_ref[...]).astype(BF16)

    def proj(a, b):
        return _nt(h, w_ref[a:b, :])

    o = 0
    scb_ref[...] = proj(o, o + CONV_DIM)
    o += CONV_DIM
    cin_ref[...] = proj(o, o + CONV_DIM) * proj(o + CONV_DIM, o + 2 * CONV_DIM)
    o += 2 * CONV_DIM
    q = _head_rms(proj(o, o + ATTN_DIM), qg_ref[...], HEAD_DIM) * (HEAD_DIM ** -0.5 * LOG2E)
    o += ATTN_DIM
    k = _head_rms(proj(o, o + ATTN_DIM), kg_ref[...], HEAD_DIM)
    kn_ref[...] = k
    o += ATTN_DIM
    v = proj(o, o + ATTN_DIM)
    v_ref[...] = v
    o += ATTN_DIM
    z_ref[...] = proj(o, o + SSD_DIM)
    o += SSD_DIM
    xbc_ref[...] = proj(o, o + XBC_DIM)
    o += XBC_DIM
    u = proj(o, o + LANES) + fb_ref[...]
    lane = lax.broadcasted_iota(jnp.int32, u.shape, 1)
    fdt = jnp.where(lane < DT_LANE0, -_softplus(-u), _softplus(u))
    fdt_ref[...] = fdt

    @pl.when(pl.program_id(0) % max(seq // tm, 1) == 0)
    def _():
        carry_ref[...] = jnp.zeros_like(carry_ref)

    lf = jnp.where(lane < FOX_HEADS, fdt, 0.0) * LOG2E
    c = _dot_exact_lhs(_lower_tri(tm, seq), lf) + carry_ref[0:1, :]
    carry_ref[...] = jnp.broadcast_to(c[tm - 1:tm, :], carry_ref.shape)
    hi, mid, lo = _split3(c)
    packed = (hi.astype(F32) + pltpu.roll(mid.astype(F32), FOX_HEADS, axis=1)
              + pltpu.roll(lo.astype(F32), 2 * FOX_HEADS, axis=1)).astype(BF16)
    bias = jnp.dot(packed, pm_ref[...], preferred_element_type=F32) + pc_ref[:, 0:2 * PAD_DIM]
    qa_ref[...] = _expand_heads(q, bias[:, 0:PAD_DIM])
    ka_ref[...] = _expand_heads(k, bias[:, PAD_DIM:2 * PAD_DIM])
    va_ref[...] = _expand_heads(v, pc_ref[:, 2 * PAD_DIM:3 * PAD_DIM])


def _bias_placement():
    pm = np.zeros((LANES, 2 * PAD_DIM), np.float32)
    pc = np.zeros((1, 3 * PAD_DIM), np.float32)
    for h in range(FOX_HEADS):
        for part in range(N_SPLIT):
            src = part * FOX_HEADS + h
            pm[src, h * LANES + BIAS_LANE0 + part] = 1.0
            pm[src, PAD_DIM + h * LANES + BIAS_LANE0 + N_SPLIT + part] = -1.0
            pc[0, h * LANES + BIAS_LANE0 + N_SPLIT + part] = 1.0
            pc[0, PAD_DIM + h * LANES + BIAS_LANE0 + part] = 1.0
        pc[0, 2 * PAD_DIM + h * LANES + HEAD_DIM] = 1.0
    return jnp.asarray(pm, BF16), jnp.asarray(pc, F32)


def _inproj(x, g, w, qg, kg, fb, pm, pc, layer, seq):
    n = x.shape[0]
    tm = min(ROW_TILE, n)
    widths = [(CONV_DIM, F32), (CONV_DIM, F32), (PAD_DIM, BF16), (ATTN_DIM, F32), (PAD_DIM, BF16),
              (ATTN_DIM, F32), (PAD_DIM, BF16), (SSD_DIM, F32), (XBC_DIM, F32), (LANES, F32)]
    return pl.pallas_call(
        functools.partial(_inproj_kernel, tm=tm, seq=seq),
        out_shape=[jax.ShapeDtypeStruct((n, w_), dt) for w_, dt in widths],
        grid=(n // tm,),
        in_specs=[
            pl.BlockSpec((tm, D_MODEL), lambda i: (i, 0)),
            _resident((1, D_MODEL)),
            _resident_layer((PROJ_PAD, D_MODEL), layer),
            _resident((1, ATTN_DIM)),
            _resident((1, ATTN_DIM)),
            _resident((1, LANES)),
            _resident((LANES, 2 * PAD_DIM)),
            _resident((1, 3 * PAD_DIM)),
        ],
        out_specs=[pl.BlockSpec((tm, w_), lambda i: (i, 0)) for w_, _ in widths],
        scratch_shapes=[pltpu.VMEM((SUBLANES, LANES), F32)],
        compiler_params=_params(("arbitrary",)),
        name="inproj",
    )(x, g, w, qg, kg, fb, pm, pc)


def _fox_kernel(q_ref, k_ref, v_ref, o_ref, m_ref, acc_ref, *, tq):
    i = pl.program_id(2)
    m_ref[...] = jnp.full(m_ref.shape, -jnp.inf, F32)
    acc_ref[...] = jnp.zeros_like(acc_ref)

    def tile(start, width, masked):
        rows = pl.ds(pl.multiple_of(start, width), width)
        for hh in range(FOX_HEADS_PER_STEP):
            sl = slice(hh * LANES, (hh + 1) * LANES)
            s = _nt(q_ref[:, sl], k_ref[rows, sl])
            if masked:
                row = lax.broadcasted_iota(jnp.int32, s.shape, 0)
                col = lax.broadcasted_iota(jnp.int32, s.shape, 1)
                s = jnp.where(col <= row, s, -jnp.inf)
            m_old = m_ref[hh]
            m_new = jnp.maximum(m_old, jnp.max(s, axis=-1, keepdims=True))
            p = jnp.exp2(s - _lane_tile(m_new, width // LANES))
            acc_ref[hh] = jnp.exp2(m_old - m_new) * acc_ref[hh] + jnp.dot(
                p.astype(BF16), v_ref[rows, sl], preferred_element_type=F32)
            m_ref[hh] = m_new

    def body(j, carry):
        tile(j * (2 * tq), 2 * tq, False)
        return carry

    lax.fori_loop(0, lax.shift_right_logical(i, 1), body, 0)

    @pl.when(lax.bitwise_and(i, 1) == 1)
    def _():
        tile((i - 1) * tq, tq, False)

    tile(i * tq, tq, True)
    outs = []
    for hh in range(FOX_HEADS_PER_STEP):
        a = acc_ref[hh]
        outs.append(a[:, 0:HEAD_DIM] / a[:, HEAD_DIM:HEAD_DIM + 1])
    o_ref[...] = jnp.concatenate(outs, axis=-1)


def _fox_prompt(qa, ka, va, nseq, seq):
    tq = min(ROW_TILE, seq)
    nq = seq // tq
    nh = FOX_HEADS_PER_STEP
    return pl.pallas_call(
        functools.partial(_fox_kernel, tq=tq),
        out_shape=jax.ShapeDtypeStruct((nseq * seq, ATTN_DIM), F32),
        grid=(nseq, FOX_HEADS // nh, nq),
        in_specs=[
            pl.BlockSpec((tq, nh * LANES), lambda b, h, i: (b * nq + i, h)),
            pl.BlockSpec((seq, nh * LANES), lambda b, h, i: (b, h)),
            pl.BlockSpec((seq, nh * LANES), lambda b, h, i: (b, h)),
        ],
        out_specs=pl.BlockSpec((tq, nh * HEAD_DIM), lambda b, h, i: (b * nq + i, h)),
        scratch_shapes=[pltpu.VMEM((nh, tq, LANES), F32), pltpu.VMEM((nh, tq, LANES), F32)],
        compiler_params=_params(("arbitrary",) * 3),
        name="fox_prompt",
    )(qa, ka, va)


def _fox_paged_kernel(pt_ref, q_ref, kn_ref, vn_ref, lfn_ref, *rest, tokens):
    npg = PAGES_PER_STEP
    k_refs = rest[:npg]
    v_refs = rest[npg:2 * npg]
    lf_refs = rest[2 * npg:3 * npg]
    o_ref, m_ref, l_ref, acc_ref, carry_ref, c_ref = rest[3 * npg:]
    g = pl.program_id(1)
    rows = FOX_HEADS * tokens

    @pl.when(g == 0)
    def _():
        m_ref[...] = jnp.full(m_ref.shape, -jnp.inf, F32)
        l_ref[...] = jnp.zeros_like(l_ref)
        acc_ref[...] = jnp.zeros_like(acc_ref)
        carry_ref[...] = jnp.zeros_like(carry_ref)

    q = q_ref[...]
    tri = _upper_tri(PAGE_SIZE)

    def update(k_list, v_list, lf_list):
        n = len(k_list)
        nr = n * FOX_HEADS
        parts = _split3(jnp.concatenate(lf_list, axis=0) * LOG2E)

        def times(m_bf16):
            return sum(jnp.dot(x, m_bf16, preferred_element_type=F32) for x in parts)

        c = times(tri) + _row_tile(carry_ref[...], n)
        if n > 1:
            tot = times(jnp.ones((PAGE_SIZE, PAGE_SIZE), BF16))
            r = lax.broadcasted_iota(jnp.int32, (nr, nr), 0)
            col = lax.broadcasted_iota(jnp.int32, (nr, nr), 1)
            same_head = lax.bitwise_and(r, FOX_HEADS - 1) == lax.bitwise_and(col, FOX_HEADS - 1)
            earlier = jnp.where(col < r, jnp.where(same_head, 1.0, 0.0), 0.0).astype(BF16)
            off = _dot_exact_lhs(earlier, tot)
            c = c + off
            carry_ref[...] = carry_ref[...] + off[nr - FOX_HEADS:nr, :] + tot[nr - FOX_HEADS:nr, :]
        c_ref[0:nr, :] = c
        s_list = []
        for idx in range(n):
            c_rows = jnp.concatenate(
                [jnp.broadcast_to(c_ref[idx * FOX_HEADS + h:idx * FOX_HEADS + h + 1, :], (tokens, PAGE_SIZE))
                 for h in range(FOX_HEADS)], axis=0)
            s_list.append(jnp.dot(q, k_list[idx].astype(BF16), preferred_element_type=F32) - c_rows)
        s = jnp.concatenate(s_list, axis=-1)
        m_old = m_ref[...]
        m_new = jnp.maximum(m_old, jnp.max(s, axis=-1, keepdims=True))
        alpha = jnp.exp2(m_old - m_new)
        p = jnp.exp2(s - _lane_tile(m_new, n))
        l_ref[...] = alpha * l_ref[...] + jnp.sum(p, axis=-1, keepdims=True)
        pv = _nt(p[:, 0:PAGE_SIZE].astype(BF16), v_list[0].astype(BF16))
        for idx in range(1, n):
            pv = pv + _nt(p[:, idx * PAGE_SIZE:(idx + 1) * PAGE_SIZE].astype(BF16), v_list[idx].astype(BF16))
        acc_ref[...] = _lane_tile(alpha, ATTN_DIM // LANES) * acc_ref[...] + pv
        m_ref[...] = m_new

    update([r[...] for r in k_refs], [r[...] for r in v_refs], [r[...] for r in lf_refs])

    @pl.when(g == pl.num_programs(1) - 1)
    def _():
        c_new = _dot_exact_rhs(lfn_ref[...] * LOG2E, _upper_tri(tokens)) + carry_ref[:, 0:tokens]
        c_ref[0:FOX_HEADS, 0:tokens] = c_new
        c_rows = jnp.concatenate(
            [jnp.broadcast_to(c_ref[h:h + 1, 0:tokens], (tokens, tokens)) for h in range(FOX_HEADS)], axis=0)
        s = _nt(q, kn_ref[...].astype(BF16)) - c_rows
        t_idx = lax.bitwise_and(lax.broadcasted_iota(jnp.int32, s.shape, 0), tokens - 1)
        s = jnp.where(lax.broadcasted_iota(jnp.int32, s.shape, 1) <= t_idx, s, -jnp.inf)
        m_old = m_ref[...]
        m_new = jnp.maximum(m_old, jnp.max(s, axis=-1, keepdims=True))
        alpha = jnp.exp2(m_old - m_new)
        p = jnp.exp2(s - m_new[:, 0:tokens])
        l_new = alpha * l_ref[...] + jnp.sum(p, axis=-1, keepdims=True)
        acc = _lane_tile(alpha, ATTN_DIM // LANES) * acc_ref[...] + jnp.dot(
            p.astype(BF16), vn_ref[...].astype(BF16), preferred_element_type=F32)
        full = acc / _lane_tile(l_new, ATTN_DIM // LANES)
        lane_head = lax.shift_right_logical(
            lax.broadcasted_iota(jnp.int32, (tokens, ATTN_DIM), 1), jnp.int32(HEAD_DIM.bit_length() - 1))
        out = jnp.zeros((tokens, ATTN_DIM), F32)
        for h in range(FOX_HEADS):
            out = out + jnp.where(lane_head == h, full[h * tokens:(h + 1) * tokens, :], 0.0)
        o_ref[...] = out


def _fox_paged(layer, page_table, qbd, kn_t, vn_t, lfn_t, cache_kt, cache_vt, cache_lf_t, nseq, tokens):
    npg = PAGES_PER_STEP
    n_groups = page_table.shape[1] // npg
    rows = FOX_HEADS * tokens

    def page_spec(width2, n):
        return pl.BlockSpec((None, None, width2, PAGE_SIZE),
                            lambda b, g, pt: (layer, pt[b, g * npg + n], 0, 0))

    in_specs = [
        pl.BlockSpec((None, rows, ATTN_DIM), lambda b, g, pt: (b, 0, 0)),
        pl.BlockSpec((None, tokens, ATTN_DIM), lambda b, g, pt: (b, 0, 0)),
        pl.BlockSpec((None, tokens, ATTN_DIM), lambda b, g, pt: (b, 0, 0)),
        pl.BlockSpec((None, FOX_HEADS, tokens), lambda b, g, pt: (b, 0, 0)),
    ]
    in_specs += [page_spec(ATTN_DIM, n) for n in range(npg)]
    in_specs += [page_spec(ATTN_DIM, n) for n in range(npg)]
    in_specs += [page_spec(FOX_HEADS, n) for n in range(npg)]
    grid_spec = pltpu.PrefetchScalarGridSpec(
        num_scalar_prefetch=1,
        grid=(nseq, n_groups),
        in_specs=in_specs,
        out_specs=pl.BlockSpec((None, tokens, ATTN_DIM), lambda b, g, pt: (b, 0, 0)),
        scratch_shapes=[pltpu.VMEM((rows, LANES), F32), pltpu.VMEM((rows, LANES), F32),
                        pltpu.VMEM((rows, ATTN_DIM), F32), pltpu.VMEM((FOX_HEADS, LANES), F32),
                        pltpu.VMEM((npg * FOX_HEADS, LANES), F32)],
    )
    return pl.pallas_call(
        functools.partial(_fox_paged_kernel, tokens=tokens),
        out_shape=jax.ShapeDtypeStruct((nseq, tokens, ATTN_DIM), F32),
        grid_spec=grid_spec,
        compiler_params=_params(("arbitrary", "arbitrary")),
        name="fox_paged",
    )(page_table, qbd, kn_t, vn_t, lfn_t,
      *([cache_kt] * npg), *([cache_vt] * npg), *([cache_lf_t] * npg))


def _ssd_kernel(xbc_ref, halo_ref, st_ref, fdt_ref, z_ref, h0_ref,
                cw_ref, cb_ref, av_ref, dv_ref,
                y_ref, hout_ref, buf_ref, h_ref, *, lc):
    j = pl.program_id(1)

    @pl.when(j == 0)
    def _():
        h_ref[...] = h0_ref[...]
        buf_ref[0:SUBLANES, :] = st_ref[...]

    @pl.when(j > 0)
    def _():
        buf_ref[0:SUBLANES, :] = halo_ref[...]

    buf_ref[SUBLANES:SUBLANES + lc, :] = xbc_ref[...]
    base = SUBLANES - (SSD_CONV - 1)
    u = buf_ref[base:base + lc, :] * cw_ref[0:1, :]
    for k in range(1, SSD_CONV):
        u = u + buf_ref[base + k:base + k + lc, :] * cw_ref[k:k + 1, :]
    u = _silu(u + cb_ref[...])

    fdt = fdt_ref[...]
    da = fdt * av_ref[...]
    acum = _dot_exact_lhs(_lower_tri(lc), da)
    pick = (lax.broadcasted_iota(jnp.int32, (SUBLANES, LANES), 1)
            == lax.broadcasted_iota(jnp.int32, (SUBLANES, LANES), 0) + DT_LANE0).astype(BF16)

    def rows_of(x):
        return sum(_nt(pick, part) for part in _split3(x))

    dtt = rows_of(fdt)
    acum_t = _dot_exact_rhs(rows_of(da), _upper_tri(lc))
    row = lax.broadcasted_iota(jnp.int32, (lc, lc), 0)
    col = lax.broadcasted_iota(jnp.int32, (lc, lc), 1)
    causal = col <= row
    z = z_ref[...]
    dvec = dv_ref[...]

    outs = []
    for h in range(SSD_HEADS):
        grp = h // (SSD_HEADS // 2)
        xh = u[:, h * HEAD_DIM:(h + 1) * HEAD_DIM]
        bh = u[:, SSD_DIM + grp * SSD_STATE:SSD_DIM + (grp + 1) * SSD_STATE]
        ch = u[:, SSD_DIM + 2 * SSD_STATE + grp * SSD_STATE:SSD_DIM + 2 * SSD_STATE + (grp + 1) * SSD_STATE]
        lane = DT_LANE0 + h
        a_col = acum[:, lane:lane + 1]
        dt_col = fdt[:, lane:lane + 1]
        a_row = acum_t[h:h + 1, :]
        dt_row = dtt[h:h + 1, :]
        a_last = acum[lc - 1:lc, lane:lane + 1]
        decay = jnp.exp(jnp.where(causal, a_col - a_row, -jnp.inf))
        xb = xh.astype(BF16)
        bb = bh.astype(BF16)
        cb16 = ch.astype(BF16)
        w = _nt(cb16, bb) * decay * dt_row
        h_prev = h_ref[h]
        y = jnp.dot(w.astype(BF16), xb, preferred_element_type=F32)
        y = y + _nt(cb16, h_prev.astype(BF16)) * jnp.exp(a_col)
        w_end = jnp.exp(a_last - a_col) * dt_col
        h_ref[h] = h_prev * jnp.exp(a_last) + _tn((xh * w_end).astype(BF16), bb)
        sl = slice(h * HEAD_DIM, (h + 1) * HEAD_DIM)
        outs.append((y + dvec[:, sl] * xh) * _silu(z[:, sl]))
    y_ref[...] = jnp.concatenate(outs, axis=-1)

    @pl.when(j == pl.num_programs(1) - 1)
    def _():
        hout_ref[...] = h_ref[...]


def _ssd(xbc, st_pad, fdt, z, h0, cw, cb, av, dv, nseq, seq):
    lc = min(SSD_CHUNK, seq)
    nc = seq // lc
    hb = lc // SUBLANES
    return pl.pallas_call(
        functools.partial(_ssd_kernel, lc=lc),
        out_shape=[jax.ShapeDtypeStruct((nseq * seq, SSD_DIM), F32),
                   jax.ShapeDtypeStruct((nseq, SSD_HEADS, HEAD_DIM, SSD_STATE), F32)],
        grid=(nseq, nc),
        in_specs=[
            pl.BlockSpec((lc, XBC_DIM), lambda b, j: (b * nc + j, 0)),
            pl.BlockSpec((SUBLANES, XBC_DIM), lambda b, j: (jnp.maximum((b * nc + j) * hb - 1, 0), 0)),
            pl.BlockSpec((None, SUBLANES, XBC_DIM), lambda b, j: (b, 0, 0)),
            pl.BlockSpec((lc, LANES), lambda b, j: (b * nc + j, 0)),
            pl.BlockSpec((lc, SSD_DIM), lambda b, j: (b * nc + j, 0)),
            pl.BlockSpec((None, SSD_HEADS, HEAD_DIM, SSD_STATE), lambda b, j: (b, 0, 0, 0)),
            _resident((SSD_CONV, XBC_DIM)),
            _resident((1, XBC_DIM)),
            _resident((1, LANES)),
            _resident((1, SSD_DIM)),
        ],
        out_specs=[pl.BlockSpec((lc, SSD_DIM), lambda b, j: (b * nc + j, 0)),
                   pl.BlockSpec((None, SSD_HEADS, HEAD_DIM, SSD_STATE), lambda b, j: (b, 0, 0, 0))],
        scratch_shapes=[pltpu.VMEM((SUBLANES + lc, XBC_DIM), F32),
                        pltpu.VMEM((SSD_HEADS, HEAD_DIM, SSD_STATE), F32)],
        compiler_params=_params(("arbitrary", "arbitrary")),
        name="ssd",
    )(xbc, xbc, st_pad, fdt, z, h0, cw, cb, av, dv)


def _merge_kernel(x_ref, scb_ref, cin_ref, halo_ref, st_ref, attn_ref, yssd_ref,
                  cw_ref, g_ref, w_ref, o_ref, buf_ref, *, tl):
    j = pl.program_id(1)

    @pl.when(j == 0)
    def _():
        buf_ref[0:SUBLANES, :] = st_ref[...]

    @pl.when(j > 0)
    def _():
        buf_ref[0:SUBLANES, :] = halo_ref[...]

    buf_ref[SUBLANES:SUBLANES + tl, :] = cin_ref[...]
    base = SUBLANES - (SC_WIDTH - 1)
    y = buf_ref[base:base + tl, :] * cw_ref[0:1, :]
    for k in range(1, SC_WIDTH):
        y = y + buf_ref[base + k:base + k + tl, :] * cw_ref[k:k + 1, :]
    cat = jnp.concatenate([scb_ref[...] * y, attn_ref[...], yssd_ref[...]], axis=-1)
    cat = _head_rms(cat, g_ref[...], HEAD_DIM).astype(BF16)
    o_ref[...] = x_ref[...] + jnp.dot(cat, w_ref[...], preferred_element_type=F32)


def _merge(x, scb, cin, st_pad, attn, yssd, cw, g, w, layer, nseq, seq):
    tl = min(ROW_TILE, seq)
    nt = seq // tl
    hb = tl // SUBLANES

    def rows(width):
        return pl.BlockSpec((tl, width), lambda b, j: (b * nt + j, 0))

    return pl.pallas_call(
        functools.partial(_merge_kernel, tl=tl),
        out_shape=jax.ShapeDtypeStruct((nseq * seq, D_MODEL), F32),
        grid=(nseq, nt),
        in_specs=[
            rows(D_MODEL), rows(CONV_DIM), rows(CONV_DIM),
            pl.BlockSpec((SUBLANES, CONV_DIM), lambda b, j: (jnp.maximum((b * nt + j) * hb - 1, 0), 0)),
            pl.BlockSpec((None, SUBLANES, CONV_DIM), lambda b, j: (b, 0, 0)),
            rows(ATTN_DIM), rows(SSD_DIM),
            _resident((SC_WIDTH, CONV_DIM)),
            _resident((1, D_MODEL)),
            _resident_layer((D_MODEL, D_MODEL), layer),
        ],
        out_specs=rows(D_MODEL),
        scratch_shapes=[pltpu.VMEM((SUBLANES + tl, CONV_DIM), F32)],
        compiler_params=_params(("arbitrary", "arbitrary")),
        name="merge",
    )(x, scb, cin, cin, st_pad, attn, yssd, cw, g, w)


def _memkv_kernel(mem_ref, g_ref, wk_ref, wv_ref, kg_ref, km_ref, vm_ref):
    m = _rms(mem_ref[...], g_ref[...]).astype(BF16)
    km_ref[...] = _head_rms(jnp.dot(m, wk_ref[...], preferred_element_type=F32), kg_ref[...], MEM_HEAD_DIM)
    vm_ref[...] = jnp.dot(m, wv_ref[...], preferred_element_type=F32)


def _memkv(mem, g, wk, wv, kg):
    nb = mem.shape[0]
    out = jax.ShapeDtypeStruct((DEPTH, nb, MEM_LEN, MEM_DIM), F32)
    return pl.pallas_call(
        _memkv_kernel,
        out_shape=[out, out],
        grid=(DEPTH, nb),
        in_specs=[
            pl.BlockSpec((None, MEM_LEN, D_MODEL), lambda l, b: (b, 0, 0)),
            pl.BlockSpec((None, 1, D_MODEL), lambda l, b: (l, 0, 0)),
            pl.BlockSpec((None, D_MODEL, MEM_DIM), lambda l, b: (l, 0, 0)),
            pl.BlockSpec((None, D_MODEL, MEM_DIM), lambda l, b: (l, 0, 0)),
            pl.BlockSpec((None, 1, MEM_DIM), lambda l, b: (l, 0, 0)),
        ],
        out_specs=[pl.BlockSpec((None, None, MEM_LEN, MEM_DIM), lambda l, b: (l, b, 0, 0))] * 2,
        compiler_params=_params(("arbitrary", "arbitrary")),
        name="mem_kv",
    )(mem, g, wk, wv, kg)


def _xattn_kernel(x_ref, km_ref, vm_ref, g_ref, wq_ref, qg_ref, wo_ref, o_ref, q_ref, att_ref, *, bs, tl):
    x = x_ref[...]
    h = _rms(x, g_ref[...]).astype(BF16)
    q = _head_rms(jnp.dot(h, wq_ref[...], preferred_element_type=F32), qg_ref[...], MEM_HEAD_DIM)
    q_ref[...] = q * (MEM_HEAD_DIM ** -0.5)

    def one_sequence(s, carry):
        rows = pl.ds(pl.multiple_of(s * tl, tl), tl)
        for hh in range(MEM_HEADS):
            sl = slice(hh * MEM_HEAD_DIM, (hh + 1) * MEM_HEAD_DIM)
            head = (s, slice(None), hh, slice(None)) if len(km_ref.shape) == 4 else (s, slice(None), sl)
            sc = _nt(q_ref[rows, sl].astype(BF16), km_ref[head].astype(BF16))
            p = jnp.exp(sc - jnp.max(sc, axis=-1, keepdims=True))
            p = p / jnp.sum(p, axis=-1, keepdims=True)
            att_ref[rows, sl] = jnp.dot(p.astype(BF16), vm_ref[head].astype(BF16),
                                        preferred_element_type=F32)
        return carry

    lax.fori_loop(0, bs, one_sequence, 0)
    o_ref[...] = x + jnp.dot(att_ref[...].astype(BF16), wo_ref[...], preferred_element_type=F32)


def _xattn(x, km, vm, layer, g, wq, qg, wo, nseq, seq):
    tl = min(ROW_TILE, seq)
    nt = seq // tl
    bs = 1 if nt > 1 else min(nseq, XATTN_SEQS_PER_STEP)
    rows = bs * tl
    mem_block = (None, bs) + tuple(km.shape[2:])
    mem_zeros = (0,) * (km.ndim - 2)
    return pl.pallas_call(
        functools.partial(_xattn_kernel, bs=bs, tl=tl),
        out_shape=jax.ShapeDtypeStruct((nseq * seq, D_MODEL), F32),
        grid=(nseq // bs, nt),
        in_specs=[
            pl.BlockSpec((rows, D_MODEL), lambda b, j: (b * nt + j, 0)),
            pl.BlockSpec(mem_block, lambda b, j: (layer, b) + mem_zeros),
            pl.BlockSpec(mem_block, lambda b, j: (layer, b) + mem_zeros),
            _resident((1, D_MODEL)),
            _resident_layer((D_MODEL, MEM_DIM), layer),
            _resident((1, MEM_DIM)),
            _resident_layer((MEM_DIM, D_MODEL), layer),
        ],
        out_specs=pl.BlockSpec((rows, D_MODEL), lambda b, j: (b * nt + j, 0)),
        scratch_shapes=[pltpu.VMEM((rows, MEM_DIM), F32), pltpu.VMEM((rows, MEM_DIM), F32)],
        compiler_params=_params(("arbitrary", "arbitrary")),
        name="xattn",
    )(x, km, vm, g, wq, qg, wo)


def _pad_state(state, width):
    nseq, _, c = state.shape
    return jnp.concatenate([jnp.zeros((nseq, SUBLANES - (width - 1), c), F32), state], axis=1)


def _prep_w_in(w_in):
    w_t = jnp.transpose(w_in, (0, 2, 1))
    n_f0 = 3 * CONV_DIM + 3 * ATTN_DIM
    n_z0 = n_f0 + FOX_HEADS
    n_dt0 = n_z0 + SSD_DIM + XBC_DIM
    pad = jnp.zeros((DEPTH, PROJ_PAD - PROJ_MAIN - FOX_HEADS - SSD_HEADS, D_MODEL), F32)
    return jnp.concatenate([w_t[:, :n_f0], w_t[:, n_z0:n_dt0], w_t[:, n_f0:n_z0], w_t[:, n_dt0:], pad],
                           axis=1).astype(BF16)


def _layer_params(l, p):
    pad = jnp.zeros((LANES - FOX_HEADS - SSD_HEADS,), F32)
    a_neg = -jnp.exp(p['ssd_A_log'][l])
    return dict(
        fb=jnp.concatenate([p['fox_f_bias'][l], p['ssd_dt_bias'][l], pad])[None, :],
        qg=jnp.tile(p['fox_q_norm'][l], FOX_HEADS)[None, :],
        kg=jnp.tile(p['fox_k_norm'][l], FOX_HEADS)[None, :],
        av=jnp.concatenate([jnp.zeros((FOX_HEADS,), F32), a_neg, pad])[None, :],
        dv=jnp.repeat(p['ssd_D'][l], HEAD_DIM)[None, :],
        xa_qg=jnp.tile(p['xa_q_norm'][l], MEM_HEADS)[None, :],
    )


def _token_mix(l, x, p, lp, nseq, seq, conv_state, ssd_conv_state, ssd_state, paged):
    n = nseq * seq
    scb, cin, qa, kn, ka, v, va, z, xbc, fdt = _inproj(
        x, p['mix_norm'][l][None, :], p['w_in_t'], lp['qg'], lp['kg'], lp['fb'], p['pm'], p['pc'], l, seq)
    logf = fdt[:, F_LANE0:F_LANE0 + FOX_HEADS].reshape(nseq, seq, FOX_HEADS)

    if paged is None:
        attn = _fox_prompt(qa, ka, va, nseq, seq)
    else:
        page_table, cache_kt, cache_vt, cache_lf_t = paged
        eye = jnp.eye(FOX_HEADS, dtype=BF16)
        q4 = jnp.transpose(qa.reshape(nseq, seq, FOX_HEADS, LANES)[..., :HEAD_DIM], (0, 2, 1, 3))
        qbd = (q4[:, :, :, None, :] * eye[None, :, None, :, None]).reshape(nseq, FOX_HEADS * seq, ATTN_DIM)
        attn = _fox_paged(l, page_table, qbd, kn.reshape(nseq, seq, ATTN_DIM), v.reshape(nseq, seq, ATTN_DIM),
                          jnp.transpose(logf, (0, 2, 1)), cache_kt, cache_vt, cache_lf_t,
                          nseq, seq).reshape(n, ATTN_DIM)

    yssd, h_end = _ssd(xbc, _pad_state(ssd_conv_state, SSD_CONV), fdt, z, ssd_state, p['ssd_conv_w'][l],
                       p['ssd_conv_b'][l][None, :], lp['av'], lp['dv'], nseq, seq)
    x = _merge(x, scb, cin, _pad_state(conv_state, SC_WIDTH), attn, yssd, p['sc_conv_w'][l],
               p['mix_out_norm'][l][None, :], p['w_out_bf'], l, nseq, seq)

    cin3 = cin.reshape(nseq, seq, CONV_DIM)
    xbc3 = xbc.reshape(nseq, seq, XBC_DIM)
    new_conv = jnp.concatenate([conv_state, cin3], axis=1)[:, -(SC_WIDTH - 1):]
    new_ssd_conv = jnp.concatenate([ssd_conv_state, xbc3], axis=1)[:, -(SSD_CONV - 1):]
    state = (kn.reshape(nseq, seq, FOX_HEADS, HEAD_DIM), v.reshape(nseq, seq, FOX_HEADS, HEAD_DIM),
             logf, new_conv, new_ssd_conv, h_end)
    return x, state


def kernel(x_prompt, x_sample, cache_fox_k, cache_fox_v, cache_fox_logf, cache_mem_k, cache_mem_v,
           state_conv, state_ssd_conv, state_ssd, page_table, mem_prompt,
           ffn1_norm, ffn1_wg, ffn1_wu, ffn1_wd, mix_norm, w_in, sc_conv_w, fox_q_norm, fox_k_norm,
           fox_f_bias, ssd_conv_w, ssd_conv_b, ssd_dt_bias, ssd_A_log, ssd_D, mix_out_norm, w_out,
           xa_norm, mem_norm, xa_wq, xa_wk, xa_wv, xa_q_norm, xa_k_norm, xa_wo,
           ffn2_norm, ffn2_wg, ffn2_wu, ffn2_wd, final_norm):
    b_p, s_p, _ = x_prompt.shape
    b_s, s_s, _ = x_sample.shape
    n_phys = cache_fox_k.shape[1]
    pm, pc = _bias_placement()
    p = dict(mix_norm=mix_norm, w_in_t=_prep_w_in(w_in), sc_conv_w=sc_conv_w, fox_q_norm=fox_q_norm,
             fox_k_norm=fox_k_norm, fox_f_bias=fox_f_bias, ssd_conv_w=ssd_conv_w, ssd_conv_b=ssd_conv_b,
             ssd_dt_bias=ssd_dt_bias, ssd_A_log=ssd_A_log, ssd_D=ssd_D, mix_out_norm=mix_out_norm,
             xa_q_norm=xa_q_norm, w_out_bf=w_out.astype(BF16), pm=pm, pc=pc)
    wg1, wu1, wd1 = ffn1_wg.astype(BF16), ffn1_wu.astype(BF16), ffn1_wd.astype(BF16)
    wg2, wu2, wd2 = ffn2_wg.astype(BF16), ffn2_wu.astype(BF16), ffn2_wd.astype(BF16)
    wq, wo = xa_wq.astype(BF16), xa_wo.astype(BF16)
    fg = final_norm[None, :]

    km_p, vm_p = _memkv(mem_prompt, mem_norm[:, None, :], xa_wk.astype(BF16), xa_wv.astype(BF16),
                        jnp.tile(xa_k_norm, (1, MEM_HEADS))[:, None, :])
    km_s, vm_s = cache_mem_k, cache_mem_v
    cache_kt = jnp.transpose(cache_fox_k, (0, 1, 3, 4, 2)).reshape(DEPTH, n_phys, ATTN_DIM, PAGE_SIZE)
    cache_vt = jnp.transpose(cache_fox_v, (0, 1, 3, 4, 2)).reshape(DEPTH, n_phys, ATTN_DIM, PAGE_SIZE)
    cache_lf_t = jnp.transpose(cache_fox_logf, (0, 1, 3, 2))

    zero_conv = jnp.zeros((b_p, SC_WIDTH - 1, CONV_DIM), F32)
    zero_ssd_conv = jnp.zeros((b_p, SSD_CONV - 1, XBC_DIM), F32)
    zero_ssd = jnp.zeros((b_p, SSD_HEADS, HEAD_DIM, SSD_STATE), F32)

    xp = x_prompt.reshape(b_p * s_p, D_MODEL)
    xs = x_sample.reshape(b_s * s_s, D_MODEL)
    st_p, st_s = [], []
    for l in range(DEPTH):
        lp = _layer_params(l, p)
        last = l == DEPTH - 1
        groups = []
        for x, nseq, seq, km, vm, cs, scs, ss, paged in (
                (xp, b_p, s_p, km_p, vm_p, zero_conv, zero_ssd_conv, zero_ssd, None),
                (xs, b_s, s_s, km_s, vm_s, state_conv[l], state_ssd_conv[l], state_ssd[l],
                 (page_table, cache_kt, cache_vt, cache_lf_t))):
            x = _ffn(x, ffn1_norm[l][None, :], wg1, wu1, wd1, fg, l, False)
            x, st = _token_mix(l, x, p, lp, nseq, seq, cs, scs, ss, paged)
            x = _xattn(x, km, vm, l, xa_norm[l][None, :], wq, lp['xa_qg'], wo, nseq, seq)
            x = _ffn(x, ffn2_norm[l][None, :], wg2, wu2, wd2, fg, l, last)
            groups.append((x, st))
        (xp, sp), (xs, ss_) = groups
        st_p.append(sp)
        st_s.append(ss_)

    def stack(states, idx):
        return jnp.stack([s[idx] for s in states])

    return (xp.reshape(b_p, s_p, D_MODEL), xs.reshape(b_s, s_s, D_MODEL),
            stack(st_p, 0), stack(st_p, 1), stack(st_p, 2),
            km_p.reshape(DEPTH, b_p, MEM_LEN, MEM_HEADS, MEM_HEAD_DIM),
            vm_p.reshape(DEPTH, b_p, MEM_LEN, MEM_HEADS, MEM_HEAD_DIM),
            stack(st_p, 3), stack(st_p, 4), stack(st_p, 5),
            stack(st_s, 0), stack(st_s, 1), stack(st_s, 2), stack(st_s, 3), stack(st_s, 4), stack(st_s, 5))
```

```python
import functools

import numpy as np
import jax
import jax.numpy as jnp
from jax import lax
from jax.experimental import pallas as pl
from jax.experimental.pallas import tpu as pltpu

F32 = jnp.float32
BF16 = jnp.bfloat16

D_MODEL = 1024
DEPTH = 4
PAGE_SIZE = 128
HEAD_DIM = 64
CONV_DIM = 256
ATTN_DIM = 512
SSD_DIM = 256
FOX_HEADS = 8
SC_WIDTH = 3
SSD_HEADS = 4
SSD_STATE = 64
SSD_CONV = 4
SSD_CHUNK = 128
XBC_DIM = 512
FFN_DIM = 2816
MEM_LEN = 256
MEM_HEADS = 4
MEM_HEAD_DIM = 128
MEM_DIM = 512
RMS_EPS = 1e-6
LOG2E = 1.4426950408889634

LANES = 128
SUBLANES = 8
VMEM_LIMIT = 56 * 1024 * 1024
ROW_TILE = 512
PAGES_PER_STEP = 16
XATTN_SEQS_PER_STEP = 8
FOX_HEADS_PER_STEP = 4
PROJ_MAIN = 3 * CONV_DIM + 3 * ATTN_DIM + SSD_DIM + XBC_DIM
PROJ_PAD = PROJ_MAIN + LANES
F_LANE0 = 0
DT_LANE0 = FOX_HEADS
PAD_DIM = FOX_HEADS * LANES
BIAS_LANE0 = HEAD_DIM
N_SPLIT = 3


def _params(sem):
    return pltpu.CompilerParams(dimension_semantics=sem, vmem_limit_bytes=VMEM_LIMIT)


def _resident(shape):
    nd = len(shape)
    return pl.BlockSpec(shape, lambda *_: (0,) * nd, pipeline_mode=pl.Buffered(1))


def _resident_layer(shape, layer):
    nd = len(shape)
    return pl.BlockSpec((None,) + tuple(shape), lambda *_: (layer,) + (0,) * nd, pipeline_mode=pl.Buffered(1))


def _rms(x, g):
    ms = jnp.mean(x * x, axis=-1, keepdims=True)
    return x * lax.rsqrt(ms + RMS_EPS) * g


def _head_rms(x, g, head_dim):
    width = x.shape[-1]
    pieces = []
    for c in range(width // LANES):
        blk = x[:, c * LANES:(c + 1) * LANES]
        sq = blk * blk
        s_all = jnp.sum(sq, axis=-1, keepdims=True)
        if head_dim == LANES:
            ms = s_all * (1.0 / LANES)
        else:
            lo = lax.broadcasted_iota(jnp.int32, sq.shape, 1) < head_dim
            s_lo = jnp.sum(jnp.where(lo, sq, 0.0), axis=-1, keepdims=True)
            ms = jnp.where(lo, s_lo, s_all - s_lo) * (1.0 / head_dim)
        pieces.append(blk * lax.rsqrt(ms + RMS_EPS))
    return jnp.concatenate(pieces, axis=-1) * g


def _silu(x):
    return x * jax.nn.sigmoid(x)


def _softplus(x):
    return jnp.maximum(x, 0.0) + jnp.log1p(jnp.exp(-jnp.abs(x)))


def _split3(x):
    hi = x.astype(BF16)
    r1 = x - hi.astype(F32)
    mid = r1.astype(BF16)
    lo = (r1 - mid.astype(F32)).astype(BF16)
    return hi, mid, lo


def _dot_exact_rhs(x, m_bf16):
    hi, mid, lo = _split3(x)
    out = jnp.dot(hi, m_bf16, preferred_element_type=F32)
    out = out + jnp.dot(mid, m_bf16, preferred_element_type=F32)
    return out + jnp.dot(lo, m_bf16, preferred_element_type=F32)


def _dot_exact_lhs(m_bf16, x):
    hi, mid, lo = _split3(x)
    out = jnp.dot(m_bf16, hi, preferred_element_type=F32)
    out = out + jnp.dot(m_bf16, mid, preferred_element_type=F32)
    return out + jnp.dot(m_bf16, lo, preferred_element_type=F32)


def _lower_tri(n, seq=None):
    r = lax.broadcasted_iota(jnp.int32, (n, n), 0)
    c = lax.broadcasted_iota(jnp.int32, (n, n), 1)
    keep = c <= r
    if seq is not None and seq < n:
        shift = jnp.int32(seq.bit_length() - 1)
        keep = keep & (lax.shift_right_logical(r, shift) == lax.shift_right_logical(c, shift))
    return keep.astype(BF16)


def _upper_tri(n):
    r = lax.broadcasted_iota(jnp.int32, (n, n), 0)
    c = lax.broadcasted_iota(jnp.int32, (n, n), 1)
    return (r <= c).astype(BF16)


def _nt(a, b):
    return lax.dot_general(a, b, (((1,), (1,)), ((), ())), preferred_element_type=F32)


def _tn(a, b):
    return lax.dot_general(a, b, (((0,), (0,)), ((), ())), preferred_element_type=F32)


def _row_tile(x, reps):
    return x if reps == 1 else jnp.concatenate([x] * reps, axis=0)


def _lane_tile(x, reps):
    return x if reps == 1 else jnp.concatenate([x] * reps, axis=-1)


def _ffn_kernel(x_ref, g_ref, wg_ref, wu_ref, wd_ref, fg_ref, o_ref, *, final):
    x = x_ref[...]
    h = _rms(x, g_ref[...]).astype(BF16)
    a = jnp.dot(h, wg_ref[...], preferred_element_type=F32)
    b = jnp.dot(h, wu_ref[...], preferred_element_type=F32)
    t = (_silu(a) * b).astype(BF16)
    out = x + 0.5 * jnp.dot(t, wd_ref[...], preferred_element_type=F32)
    if final:
        out = _rms(out, fg_ref[...])
    o_ref[...] = out


def _ffn(x, g, wg, wu, wd, fg, layer, final):
    n = x.shape[0]
    tm = min(ROW_TILE, n)
    return pl.pallas_call(
        functools.partial(_ffn_kernel, final=final),
        out_shape=jax.ShapeDtypeStruct((n, D_MODEL), F32),
        grid=(n // tm,),
        in_specs=[
            pl.BlockSpec((tm, D_MODEL), lambda i: (i, 0)),
            _resident((1, D_MODEL)),
            _resident_layer((D_MODEL, FFN_DIM), layer),
            _resident_layer((D_MODEL, FFN_DIM), layer),
            _resident_layer((FFN_DIM, D_MODEL), layer),
            _resident((1, D_MODEL)),
        ],
        out_specs=pl.BlockSpec((tm, D_MODEL), lambda i: (i, 0)),
        compiler_params=_params(("arbitrary",)),
        name="ffn",
    )(x, g, wg, wu, wd, fg)


def _expand_heads(x, fill):
    lo = lax.broadcasted_iota(jnp.int32, (x.shape[0], LANES), 1) < HEAD_DIM
    blocks = []
    for c in range(ATTN_DIM // LANES):
        blk = x[:, c * LANES:(c + 1) * LANES]
        for half, src in enumerate((blk, pltpu.roll(blk, HEAD_DIM, axis=1))):
            h = 2 * c + half
            blocks.append(jnp.where(lo, src, fill[:, h * LANES:(h + 1) * LANES]))
    return jnp.concatenate(blocks, axis=-1).astype(BF16)


def _inproj_kernel(x_ref, g_ref, w_ref, qg_ref, kg_ref, fb_ref, pm_ref, pc_ref,
                   scb_ref, cin_ref, qa_ref, kn_ref, ka_ref, v_ref, va_ref,
                   z_ref, xbc_ref, fdt_ref, *rest, tm, seq, dim_major):
    carry_ref = rest[-1]
    h = _rms(x_ref[...], g_ref[...]).astype(BF16)

    def store_kv(ref, val):
        ref[...] = val.T.reshape(FOX_HEADS, HEAD_DIM, tm) if dim_major else val

    def proj(a, b):
        return _nt(h, w_ref[a:b, :])

    o = 0
    scb_ref[...] = proj(o, o + CONV_DIM)
    o += CONV_DIM
    cin_ref[...] = proj(o, o + CONV_DIM) * proj(o + CONV_DIM, o + 2 * CONV_DIM)
    o += 2 * CONV_DIM
    q = _head_rms(proj(o, o + ATTN_DIM), qg_ref[...], HEAD_DIM) * (HEAD_DIM ** -0.5 * LOG2E)
    o += ATTN_DIM
    k = _head_rms(proj(o, o + ATTN_DIM), kg_ref[...], HEAD_DIM)
    store_kv(kn_ref, k)
    o += ATTN_DIM
    v = proj(o, o + ATTN_DIM)
    store_kv(v_ref, v)
    o += ATTN_DIM
    z_ref[...] = proj(o, o + SSD_DIM)
    o += SSD_DIM
    xbc_ref[...] = proj(o, o + XBC_DIM)
    o += XBC_DIM
    u = proj(o, o + LANES) + fb_ref[...]
    lane = lax.broadcasted_iota(jnp.int32, u.shape, 1)
    fdt = jnp.where(lane < DT_LANE0, -_softplus(-u), _softplus(u))
    fdt_ref[...] = fdt
    if dim_major:
        rest[0][...] = fdt.T[0:FOX_HEADS, :]

    @pl.when(pl.program_id(0) % max(seq // tm, 1) == 0)
    def _():
        carry_ref[...] = jnp.zeros_like(carry_ref)

    lf = jnp.where(lane < FOX_HEADS, fdt, 0.0) * LOG2E
    c = _dot_exact_lhs(_lower_tri(tm, seq), lf) + carry_ref[0:1, :]
    carry_ref[...] = jnp.broadcast_to(c[tm - 1:tm, :], carry_ref.shape)
    hi, mid, lo = _split3(c)
    packed = (hi.astype(F32) + pltpu.roll(mid.astype(F32), FOX_HEADS, axis=1)
              + pltpu.roll(lo.astype(F32), 2 * FOX_HEADS, axis=1)).astype(BF16)
    bias = jnp.dot(packed, pm_ref[...], preferred_element_type=F32) + pc_ref[:, 0:2 * PAD_DIM]
    qa_ref[...] = _expand_heads(q, bias[:, 0:PAD_DIM])
    ka_ref[...] = _expand_heads(k, bias[:, PAD_DIM:2 * PAD_DIM])
    va_ref[...] = _expand_heads(v, pc_ref[:, 2 * PAD_DIM:3 * PAD_DIM])


def _bias_placement():
    pm = np.zeros((LANES, 2 * PAD_DIM), np.float32)
    pc = np.zeros((1, 3 * PAD_DIM), np.float32)
    for h in range(FOX_HEADS):
        for part in range(N_SPLIT):
            src = part * FOX_HEADS + h
            pm[src, h * LANES + BIAS_LANE0 + part] = 1.0
            pm[src, PAD_DIM + h * LANES + BIAS_LANE0 + N_SPLIT + part] = -1.0
            pc[0, h * LANES + BIAS_LANE0 + N_SPLIT + part] = 1.0
            pc[0, PAD_DIM + h * LANES + BIAS_LANE0 + part] = 1.0
        pc[0, 2 * PAD_DIM + h * LANES + HEAD_DIM] = 1.0
    return jnp.asarray(pm, BF16), jnp.asarray(pc, F32)


def _inproj(x, g, w, qg, kg, fb, pm, pc, layer, seq):
    n = x.shape[0]
    tm = min(ROW_TILE, n)
    widths = [(CONV_DIM, F32), (CONV_DIM, F32), (PAD_DIM, BF16), (ATTN_DIM, F32), (PAD_DIM, BF16),
              (ATTN_DIM, F32), (PAD_DIM, BF16), (SSD_DIM, F32), (XBC_DIM, F32), (LANES, F32)]
    out_shape = [jax.ShapeDtypeStruct((n, w_), dt) for w_, dt in widths]
    out_specs = [pl.BlockSpec((tm, w_), lambda i: (i, 0)) for w_, _ in widths]
    dim_major = seq >= tm
    if dim_major:
        tps = seq // tm
        kv_shape = jax.ShapeDtypeStruct((n // seq, FOX_HEADS, HEAD_DIM, seq), F32)
        kv_spec = pl.BlockSpec((None, FOX_HEADS, HEAD_DIM, tm), lambda i: (i // tps, 0, 0, i % tps))
        out_shape[3], out_shape[5] = kv_shape, kv_shape
        out_specs[3], out_specs[5] = kv_spec, kv_spec
        out_shape.append(jax.ShapeDtypeStruct((n // seq, FOX_HEADS, seq), F32))
        out_specs.append(pl.BlockSpec((None, FOX_HEADS, tm), lambda i: (i // tps, 0, i % tps)))
    return pl.pallas_call(
        functools.partial(_inproj_kernel, tm=tm, seq=seq, dim_major=dim_major),
        out_shape=out_shape,
        grid=(n // tm,),
        in_specs=[
            pl.BlockSpec((tm, D_MODEL), lambda i: (i, 0)),
            _resident((1, D_MODEL)),
            _resident_layer((PROJ_PAD, D_MODEL), layer),
            _resident((1, ATTN_DIM)),
            _resident((1, ATTN_DIM)),
            _resident((1, LANES)),
            _resident((LANES, 2 * PAD_DIM)),
            _resident((1, 3 * PAD_DIM)),
        ],
        out_specs=out_specs,
        scratch_shapes=[pltpu.VMEM((SUBLANES, LANES), F32)],
        compiler_params=_params(("arbitrary",)),
        name="inproj",
    )(x, g, w, qg, kg, fb, pm, pc)


def _fox_kernel(q_ref, k_ref, v_ref, o_ref, m_ref, acc_ref, *, tq):
    i = pl.program_id(2)
    m_ref[...] = jnp.full(m_ref.shape, -jnp.inf, F32)
    acc_ref[...] = jnp.zeros_like(acc_ref)

    def tile(start, width, masked):
        rows = pl.ds(pl.multiple_of(start, width), width)
        for hh in range(FOX_HEADS_PER_STEP):
            sl = slice(hh * LANES, (hh + 1) * LANES)
            s = _nt(q_ref[:, sl], k_ref[rows, sl])
            if masked:
                row = lax.broadcasted_iota(jnp.int32, s.shape, 0)
                col = lax.broadcasted_iota(jnp.int32, s.shape, 1)
                s = jnp.where(col <= row, s, -jnp.inf)
            m_old = m_ref[hh]
            m_new = jnp.maximum(m_old, jnp.max(s, axis=-1, keepdims=True))
            p = jnp.exp2(s - _lane_tile(m_new, width // LANES))
            acc_ref[hh] = jnp.exp2(m_old - m_new) * acc_ref[hh] + jnp.dot(
                p.astype(BF16), v_ref[rows, sl], preferred_element_type=F32)
            m_ref[hh] = m_new

    def body(j, carry):
        tile(j * (2 * tq), 2 * tq, False)
        return carry

    lax.fori_loop(0, lax.shift_right_logical(i, 1), body, 0)

    @pl.when(lax.bitwise_and(i, 1) == 1)
    def _():
        tile((i - 1) * tq, tq, False)

    tile(i * tq, tq, True)
    outs = []
    for hh in range(FOX_HEADS_PER_STEP):
        a = acc_ref[hh]
        outs.append(a[:, 0:HEAD_DIM] / a[:, HEAD_DIM:HEAD_DIM + 1])
    o_ref[...] = jnp.concatenate(outs, axis=-1)


def _fox_prompt(qa, ka, va, nseq, seq):
    tq = min(ROW_TILE, seq)
    nq = seq // tq
    nh = FOX_HEADS_PER_STEP
    return pl.pallas_call(
        functools.partial(_fox_kernel, tq=tq),
        out_shape=jax.ShapeDtypeStruct((nseq * seq, ATTN_DIM), F32),
        grid=(nseq, FOX_HEADS // nh, nq),
        in_specs=[
            pl.BlockSpec((tq, nh * LANES), lambda b, h, i: (b * nq + i, h)),
            pl.BlockSpec((seq, nh * LANES), lambda b, h, i: (b, h)),
            pl.BlockSpec((seq, nh * LANES), lambda b, h, i: (b, h)),
        ],
        out_specs=pl.BlockSpec((tq, nh * HEAD_DIM), lambda b, h, i: (b * nq + i, h)),
        scratch_shapes=[pltpu.VMEM((nh, tq, LANES), F32), pltpu.VMEM((nh, tq, LANES), F32)],
        compiler_params=_params(("arbitrary",) * 3),
        name="fox_prompt",
    )(qa, ka, va)


def _fox_paged_kernel(pt_ref, q_ref, kn_ref, vn_ref, lfn_ref, *rest, tokens):
    npg = PAGES_PER_STEP
    k_refs = rest[:npg]
    v_refs = rest[npg:2 * npg]
    lf_refs = rest[2 * npg:3 * npg]
    o_ref, m_ref, l_ref, acc_ref, carry_ref, c_ref = rest[3 * npg:]
    g = pl.program_id(1)
    rows = FOX_HEADS * tokens

    @pl.when(g == 0)
    def _():
        m_ref[...] = jnp.full(m_ref.shape, -jnp.inf, F32)
        l_ref[...] = jnp.zeros_like(l_ref)
        acc_ref[...] = jnp.zeros_like(acc_ref)
        carry_ref[...] = jnp.zeros_like(carry_ref)

    q = q_ref[...]
    tri = _upper_tri(PAGE_SIZE)

    def update(k_list, v_list, lf_list):
        n = len(k_list)
        nr = n * FOX_HEADS
        parts = _split3(jnp.concatenate(lf_list, axis=0) * LOG2E)

        def times(m_bf16):
            return sum(jnp.dot(x, m_bf16, preferred_element_type=F32) for x in parts)

        c = times(tri) + _row_tile(carry_ref[...], n)
        if n > 1:
            tot = times(jnp.ones((PAGE_SIZE, PAGE_SIZE), BF16))
            r = lax.broadcasted_iota(jnp.int32, (nr, nr), 0)
            col = lax.broadcasted_iota(jnp.int32, (nr, nr), 1)
            same_head = lax.bitwise_and(r, FOX_HEADS - 1) == lax.bitwise_and(col, FOX_HEADS - 1)
            earlier = jnp.where(col < r, jnp.where(same_head, 1.0, 0.0), 0.0).astype(BF16)
            off = _dot_exact_lhs(earlier, tot)
            c = c + off
            carry_ref[...] = carry_ref[...] + off[nr - FOX_HEADS:nr, :] + tot[nr - FOX_HEADS:nr, :]
        c_ref[0:nr, :] = c
        s_list = []
        for idx in range(n):
            c_rows = jnp.concatenate(
                [jnp.broadcast_to(c_ref[idx * FOX_HEADS + h:idx * FOX_HEADS + h + 1, :], (tokens, PAGE_SIZE))
                 for h in range(FOX_HEADS)], axis=0)
            s_list.append(jnp.dot(q, k_list[idx].astype(BF16), preferred_element_type=F32) - c_rows)
        s = jnp.concatenate(s_list, axis=-1)
        m_old = m_ref[...]
        m_new = jnp.maximum(m_old, jnp.max(s, axis=-1, keepdims=True))
        alpha = jnp.exp2(m_old - m_new)
        p = jnp.exp2(s - _lane_tile(m_new, n))
        l_ref[...] = alpha * l_ref[...] + jnp.sum(p, axis=-1, keepdims=True)
        pv = _nt(p[:, 0:PAGE_SIZE].astype(BF16), v_list[0].astype(BF16))
        for idx in range(1, n):
            pv = pv + _nt(p[:, idx * PAGE_SIZE:(idx + 1) * PAGE_SIZE].astype(BF16), v_list[idx].astype(BF16))
        acc_ref[...] = _lane_tile(alpha, ATTN_DIM // LANES) * acc_ref[...] + pv
        m_ref[...] = m_new

    update([r[...] for r in k_refs], [r[...] for r in v_refs], [r[...] for r in lf_refs])

    @pl.when(g == pl.num_programs(1) - 1)
    def _():
        c_new = _dot_exact_rhs(lfn_ref[...] * LOG2E, _upper_tri(tokens)) + carry_ref[:, 0:tokens]
        c_ref[0:FOX_HEADS, 0:tokens] = c_new
        c_rows = jnp.concatenate(
            [jnp.broadcast_to(c_ref[h:h + 1, 0:tokens], (tokens, tokens)) for h in range(FOX_HEADS)], axis=0)
        s = _nt(q, kn_ref[...].astype(BF16)) - c_rows
        t_idx = lax.bitwise_and(lax.broadcasted_iota(jnp.int32, s.shape, 0), tokens - 1)
        s = jnp.where(lax.broadcasted_iota(jnp.int32, s.shape, 1) <= t_idx, s, -jnp.inf)
        m_old = m_ref[...]
        m_new = jnp.maximum(m_old, jnp.max(s, axis=-1, keepdims=True))
        alpha = jnp.exp2(m_old - m_new)
        p = jnp.exp2(s - m_new[:, 0:tokens])
        l_new = alpha * l_ref[...] + jnp.sum(p, axis=-1, keepdims=True)
        acc = _lane_tile(alpha, ATTN_DIM // LANES) * acc_ref[...] + jnp.dot(
            p.astype(BF16), vn_ref[...].astype(BF16), preferred_element_type=F32)
        full = acc / _lane_tile(l_new, ATTN_DIM // LANES)
        lane_head = lax.shift_right_logical(
            lax.broadcasted_iota(jnp.int32, (tokens, ATTN_DIM), 1), jnp.int32(HEAD_DIM.bit_length() - 1))
        out = jnp.zeros((tokens, ATTN_DIM), F32)
        for h in range(FOX_HEADS):
            out = out + jnp.where(lane_head == h, full[h * tokens:(h + 1) * tokens, :], 0.0)
        o_ref[...] = out


def _fox_paged(layer, page_table, qbd, kn_t, vn_t, lfn_t, cache_kt, cache_vt, cache_lf_t, nseq, tokens):
    npg = PAGES_PER_STEP
    n_groups = page_table.shape[1] // npg
    rows = FOX_HEADS * tokens

    def page_spec(width2, n):
        return pl.BlockSpec((None, None, width2, PAGE_SIZE),
                            lambda b, g, pt: (layer, pt[b, g * npg + n], 0, 0))

    in_specs = [
        pl.BlockSpec((None, rows, ATTN_DIM), lambda b, g, pt: (b, 0, 0)),
        pl.BlockSpec((None, tokens, ATTN_DIM), lambda b, g, pt: (b, 0, 0)),
        pl.BlockSpec((None, tokens, ATTN_DIM), lambda b, g, pt: (b, 0, 0)),
        pl.BlockSpec((None, FOX_HEADS, tokens), lambda b, g, pt: (b, 0, 0)),
    ]
    in_specs += [page_spec(ATTN_DIM, n) for n in range(npg)]
    in_specs += [page_spec(ATTN_DIM, n) for n in range(npg)]
    in_specs += [page_spec(FOX_HEADS, n) for n in range(npg)]
    grid_spec = pltpu.PrefetchScalarGridSpec(
        num_scalar_prefetch=1,
        grid=(nseq, n_groups),
        in_specs=in_specs,
        out_specs=pl.BlockSpec((None, tokens, ATTN_DIM), lambda b, g, pt: (b, 0, 0)),
        scratch_shapes=[pltpu.VMEM((rows, LANES), F32), pltpu.VMEM((rows, LANES), F32),
                        pltpu.VMEM((rows, ATTN_DIM), F32), pltpu.VMEM((FOX_HEADS, LANES), F32),
                        pltpu.VMEM((npg * FOX_HEADS, LANES), F32)],
    )
    return pl.pallas_call(
        functools.partial(_fox_paged_kernel, tokens=tokens),
        out_shape=jax.ShapeDtypeStruct((nseq, tokens, ATTN_DIM), F32),
        grid_spec=grid_spec,
        compiler_params=_params(("arbitrary", "arbitrary")),
        name="fox_paged",
    )(page_table, qbd, kn_t, vn_t, lfn_t,
      *([cache_kt] * npg), *([cache_vt] * npg), *([cache_lf_t] * npg))


def _ssd_kernel(xbc_ref, halo_ref, st_ref, fdt_ref, z_ref, h0_ref,
                cw_ref, cb_ref, av_ref, dv_ref,
                y_ref, hout_ref, buf_ref, h_ref, *, lc):
    j = pl.program_id(1)

    @pl.when(j == 0)
    def _():
        h_ref[...] = h0_ref[...]
        buf_ref[0:SUBLANES, :] = st_ref[...]

    @pl.when(j > 0)
    def _():
        buf_ref[0:SUBLANES, :] = halo_ref[...]

    buf_ref[SUBLANES:SUBLANES + lc, :] = xbc_ref[...]
    base = SUBLANES - (SSD_CONV - 1)
    u = buf_ref[base:base + lc, :] * cw_ref[0:1, :]
    for k in range(1, SSD_CONV):
        u = u + buf_ref[base + k:base + k + lc, :] * cw_ref[k:k + 1, :]
    u = _silu(u + cb_ref[...])

    fdt = fdt_ref[...]
    da = fdt * av_ref[...]
    acum = _dot_exact_lhs(_lower_tri(lc), da)
    pick = (lax.broadcasted_iota(jnp.int32, (SUBLANES, LANES), 1)
            == lax.broadcasted_iota(jnp.int32, (SUBLANES, LANES), 0) + DT_LANE0).astype(BF16)

    def rows_of(x):
        return sum(_nt(pick, part) for part in _split3(x))

    dtt = rows_of(fdt)
    acum_t = _dot_exact_rhs(rows_of(da), _upper_tri(lc))
    row = lax.broadcasted_iota(jnp.int32, (lc, lc), 0)
    col = lax.broadcasted_iota(jnp.int32, (lc, lc), 1)
    causal = col <= row
    z = z_ref[...]
    dvec = dv_ref[...]

    outs = []
    for h in range(SSD_HEADS):
        grp = h // (SSD_HEADS // 2)
        xh = u[:, h * HEAD_DIM:(h + 1) * HEAD_DIM]
        bh = u[:, SSD_DIM + grp * SSD_STATE:SSD_DIM + (grp + 1) * SSD_STATE]
        ch = u[:, SSD_DIM + 2 * SSD_STATE + grp * SSD_STATE:SSD_DIM + 2 * SSD_STATE + (grp + 1) * SSD_STATE]
        lane = DT_LANE0 + h
        a_col = acum[:, lane:lane + 1]
        dt_col = fdt[:, lane:lane + 1]
        a_row = acum_t[h:h + 1, :]
        dt_row = dtt[h:h + 1, :]
        a_last = acum[lc - 1:lc, lane:lane + 1]
        decay = jnp.exp(jnp.where(causal, a_col - a_row, -jnp.inf))
        xb = xh.astype(BF16)
        bb = bh.astype(BF16)
        cb16 = ch.astype(BF16)
        w = _nt(cb16, bb) * decay * dt_row
        h_prev = h_ref[h]
        y = jnp.dot(w.astype(BF16), xb, preferred_element_type=F32)
        y = y + _nt(cb16, h_prev.astype(BF16)) * jnp.exp(a_col)
        w_end = jnp.exp(a_last - a_col) * dt_col
        h_ref[h] = h_prev * jnp.exp(a_last) + _tn((xh * w_end).astype(BF16), bb)
        sl = slice(h * HEAD_DIM, (h + 1) * HEAD_DIM)
        outs.append((y + dvec[:, sl] * xh) * _silu(z[:, sl]))
    y_ref[...] = jnp.concatenate(outs, axis=-1)

    @pl.when(j == pl.num_programs(1) - 1)
    def _():
        hout_ref[...] = h_ref[...]


def _ssd(xbc, st_pad, fdt, z, h0, cw, cb, av, dv, nseq, seq):
    lc = min(SSD_CHUNK, seq)
    nc = seq // lc
    hb = lc // SUBLANES
    return pl.pallas_call(
        functools.partial(_ssd_kernel, lc=lc),
        out_shape=[jax.ShapeDtypeStruct((nseq * seq, SSD_DIM), F32),
                   jax.ShapeDtypeStruct((nseq, SSD_HEADS, HEAD_DIM, SSD_STATE), F32)],
        grid=(nseq, nc),
        in_specs=[
            pl.BlockSpec((lc, XBC_DIM), lambda b, j: (b * nc + j, 0)),
            pl.BlockSpec((SUBLANES, XBC_DIM), lambda b, j: (jnp.maximum((b * nc + j) * hb - 1, 0), 0)),
            pl.BlockSpec((None, SUBLANES, XBC_DIM), lambda b, j: (b, 0, 0)),
            pl.BlockSpec((lc, LANES), lambda b, j: (b * nc + j, 0)),
            pl.BlockSpec((lc, SSD_DIM), lambda b, j: (b * nc + j, 0)),
            pl.BlockSpec((None, SSD_HEADS, HEAD_DIM, SSD_STATE), lambda b, j: (b, 0, 0, 0)),
            _resident((SSD_CONV, XBC_DIM)),
            _resident((1, XBC_DIM)),
            _resident((1, LANES)),
            _resident((1, SSD_DIM)),
        ],
        out_specs=[pl.BlockSpec((lc, SSD_DIM), lambda b, j: (b * nc + j, 0)),
                   pl.BlockSpec((None, SSD_HEADS, HEAD_DIM, SSD_STATE), lambda b, j: (b, 0, 0, 0))],
        scratch_shapes=[pltpu.VMEM((SUBLANES + lc, XBC_DIM), F32),
                        pltpu.VMEM((SSD_HEADS, HEAD_DIM, SSD_STATE), F32)],
        compiler_params=_params(("arbitrary", "arbitrary")),
        name="ssd",
    )(xbc, xbc, st_pad, fdt, z, h0, cw, cb, av, dv)


def _merge_kernel(x_ref, scb_ref, cin_ref, halo_ref, st_ref, attn_ref, yssd_ref,
                  cw_ref, g_ref, w_ref, o_ref, buf_ref, *, tl):
    j = pl.program_id(1)

    @pl.when(j == 0)
    def _():
        buf_ref[0:SUBLANES, :] = st_ref[...]

    @pl.when(j > 0)
    def _():
        buf_ref[0:SUBLANES, :] = halo_ref[...]

    buf_ref[SUBLANES:SUBLANES + tl, :] = cin_ref[...]
    base = SUBLANES - (SC_WIDTH - 1)
    y = buf_ref[base:base + tl, :] * cw_ref[0:1, :]
    for k in range(1, SC_WIDTH):
        y = y + buf_ref[base + k:base + k + tl, :] * cw_ref[k:k + 1, :]
    cat = jnp.concatenate([scb_ref[...] * y, attn_ref[...], yssd_ref[...]], axis=-1)
    cat = _head_rms(cat, g_ref[...], HEAD_DIM).astype(BF16)
    o_ref[...] = x_ref[...] + jnp.dot(cat, w_ref[...], preferred_element_type=F32)


def _merge(x, scb, cin, st_pad, attn, yssd, cw, g, w, layer, nseq, seq):
    tl = min(ROW_TILE, seq)
    nt = seq // tl
    hb = tl // SUBLANES

    def rows(width):
        return pl.BlockSpec((tl, width), lambda b, j: (b * nt + j, 0))

    return pl.pallas_call(
        functools.partial(_merge_kernel, tl=tl),
        out_shape=jax.ShapeDtypeStruct((nseq * seq, D_MODEL), F32),
        grid=(nseq, nt),
        in_specs=[
            rows(D_MODEL), rows(CONV_DIM), rows(CONV_DIM),
            pl.BlockSpec((SUBLANES, CONV_DIM), lambda b, j: (jnp.maximum((b * nt + j) * hb - 1, 0), 0)),
            pl.BlockSpec((None, SUBLANES, CONV_DIM), lambda b, j: (b, 0, 0)),
            rows(ATTN_DIM), rows(SSD_DIM),
            _resident((SC_WIDTH, CONV_DIM)),
            _resident((1, D_MODEL)),
            _resident_layer((D_MODEL, D_MODEL), layer),
        ],
        out_specs=rows(D_MODEL),
        scratch_shapes=[pltpu.VMEM((SUBLANES + tl, CONV_DIM), F32)],
        compiler_params=_params(("arbitrary", "arbitrary")),
        name="merge",
    )(x, scb, cin, cin, st_pad, attn, yssd, cw, g, w)


def _memkv_kernel(mem_ref, g_ref, wk_ref, wv_ref, kg_ref, km_ref, vm_ref):
    m = _rms(mem_ref[...], g_ref[...]).astype(BF16)
    km_ref[...] = _head_rms(jnp.dot(m, wk_ref[...], preferred_element_type=F32), kg_ref[...], MEM_HEAD_DIM)
    vm_ref[...] = jnp.dot(m, wv_ref[...], preferred_element_type=F32)


def _memkv(mem, g, wk, wv, kg):
    nb = mem.shape[0]
    out = jax.ShapeDtypeStruct((DEPTH, nb, MEM_LEN, MEM_DIM), F32)
    return pl.pallas_call(
        _memkv_kernel,
        out_shape=[out, out],
        grid=(DEPTH, nb),
        in_specs=[
            pl.BlockSpec((None, MEM_LEN, D_MODEL), lambda l, b: (b, 0, 0)),
            pl.BlockSpec((None, 1, D_MODEL), lambda l, b: (l, 0, 0)),
            pl.BlockSpec((None, D_MODEL, MEM_DIM), lambda l, b: (l, 0, 0)),
            pl.BlockSpec((None, D_MODEL, MEM_DIM), lambda l, b: (l, 0, 0)),
            pl.BlockSpec((None, 1, MEM_DIM), lambda l, b: (l, 0, 0)),
        ],
        out_specs=[pl.BlockSpec((None, None, MEM_LEN, MEM_DIM), lambda l, b: (l, b, 0, 0))] * 2,
        compiler_params=_params(("arbitrary", "arbitrary")),
        name="mem_kv",
    )(mem, g, wk, wv, kg)


def _xattn_kernel(x_ref, km_ref, vm_ref, g_ref, wq_ref, qg_ref, wo_ref, o_ref, q_ref, att_ref, *, bs, tl):
    x = x_ref[...]
    h = _rms(x, g_ref[...]).astype(BF16)
    q = _head_rms(jnp.dot(h, wq_ref[...], preferred_element_type=F32), qg_ref[...], MEM_HEAD_DIM)
    q = q * (MEM_HEAD_DIM ** -0.5)
    for hh in range(MEM_HEADS):
        q_ref[hh] = q[:, hh * MEM_HEAD_DIM:(hh + 1) * MEM_HEAD_DIM]

    def softmax(sc):
        p = jnp.exp(sc - jnp.max(sc, axis=-1, keepdims=True))
        return (p / jnp.sum(p, axis=-1, keepdims=True)).astype(BF16)

    for s in range(bs):
        rows = slice(s * tl, (s + 1) * tl)
        if len(km_ref.shape) == 4:
            km = km_ref[s].reshape(MEM_LEN * MEM_HEADS, MEM_HEAD_DIM).astype(BF16)
            vm = vm_ref[s].reshape(MEM_LEN * MEM_HEADS, MEM_HEAD_DIM).astype(BF16)
            qs = jnp.concatenate([q_ref[hh, rows, :] for hh in range(MEM_HEADS)], axis=0).astype(BF16)
            sc = _nt(qs, km)
            row_head = lax.shift_right_logical(lax.broadcasted_iota(jnp.int32, sc.shape, 0),
                                               jnp.int32(tl.bit_length() - 1))
            col_head = lax.bitwise_and(lax.broadcasted_iota(jnp.int32, sc.shape, 1), MEM_HEADS - 1)
            out = jnp.dot(softmax(jnp.where(row_head == col_head, sc, -jnp.inf)), vm,
                          preferred_element_type=F32)
            for hh in range(MEM_HEADS):
                att_ref[hh, rows, :] = out[hh * tl:(hh + 1) * tl, :]
        else:
            for hh in range(MEM_HEADS):
                sl = slice(hh * MEM_HEAD_DIM, (hh + 1) * MEM_HEAD_DIM)
                sc = _nt(q_ref[hh, rows, :].astype(BF16), km_ref[s, :, sl].astype(BF16))
                att_ref[hh, rows, :] = jnp.dot(softmax(sc), vm_ref[s, :, sl].astype(BF16),
                                               preferred_element_type=F32)

    att = jnp.concatenate([att_ref[hh] for hh in range(MEM_HEADS)], axis=-1).astype(BF16)
    o_ref[...] = x + jnp.dot(att, wo_ref[...], preferred_element_type=F32)


def _xattn(x, km, vm, layer, g, wq, qg, wo, nseq, seq):
    tl = min(ROW_TILE, seq)
    nt = seq // tl
    bs = 1 if nt > 1 else min(nseq, XATTN_SEQS_PER_STEP)
    rows = bs * tl
    mem_zeros = (0,) * (km.ndim - 2)
    mem_spec = pl.BlockSpec((None, bs) + tuple(km.shape[2:]), lambda b, j: (layer, b) + mem_zeros)
    row_spec = pl.BlockSpec((rows, D_MODEL), lambda b, j: (b * nt + j, 0))
    return pl.pallas_call(
        functools.partial(_xattn_kernel, bs=bs, tl=tl),
        out_shape=jax.ShapeDtypeStruct((nseq * seq, D_MODEL), F32),
        grid=(nseq // bs, nt),
        in_specs=[
            row_spec, mem_spec, mem_spec,
            _resident((1, D_MODEL)),
            _resident_layer((D_MODEL, MEM_DIM), layer),
            _resident((1, MEM_DIM)),
            _resident_layer((MEM_DIM, D_MODEL), layer),
        ],
        out_specs=row_spec,
        scratch_shapes=[pltpu.VMEM((MEM_HEADS, rows, MEM_HEAD_DIM), F32),
                        pltpu.VMEM((MEM_HEADS, rows, MEM_HEAD_DIM), F32)],
        compiler_params=_params(("arbitrary", "arbitrary")),
        name="xattn",
    )(x, km, vm, g, wq, qg, wo)


def _pad_state(state, width):
    nseq, _, c = state.shape
    return jnp.concatenate([jnp.zeros((nseq, SUBLANES - (width - 1), c), F32), state], axis=1)


def _prep_w_in(w_in):
    w_t = jnp.transpose(w_in, (0, 2, 1))
    n_f0 = 3 * CONV_DIM + 3 * ATTN_DIM
    n_z0 = n_f0 + FOX_HEADS
    n_dt0 = n_z0 + SSD_DIM + XBC_DIM
    pad = jnp.zeros((DEPTH, PROJ_PAD - PROJ_MAIN - FOX_HEADS - SSD_HEADS, D_MODEL), F32)
    return jnp.concatenate([w_t[:, :n_f0], w_t[:, n_z0:n_dt0], w_t[:, n_f0:n_z0], w_t[:, n_dt0:], pad],
                           axis=1).astype(BF16)


def _layer_params(l, p):
    pad = jnp.zeros((LANES - FOX_HEADS - SSD_HEADS,), F32)
    a_neg = -jnp.exp(p['ssd_A_log'][l])
    return dict(
        fb=jnp.concatenate([p['fox_f_bias'][l], p['ssd_dt_bias'][l], pad])[None, :],
        qg=jnp.tile(p['fox_q_norm'][l], FOX_HEADS)[None, :],
        kg=jnp.tile(p['fox_k_norm'][l], FOX_HEADS)[None, :],
        av=jnp.concatenate([jnp.zeros((FOX_HEADS,), F32), a_neg, pad])[None, :],
        dv=jnp.repeat(p['ssd_D'][l], HEAD_DIM)[None, :],
        xa_qg=jnp.tile(p['xa_q_norm'][l], MEM_HEADS)[None, :],
    )


def _token_mix(l, x, p, lp, nseq, seq, conv_state, ssd_conv_state, ssd_state, paged):
    n = nseq * seq
    scb, cin, qa, kn, ka, v, va, z, xbc, fdt, *lf_t = _inproj(
        x, p['mix_norm'][l][None, :], p['w_in_t'], lp['qg'], lp['kg'], lp['fb'], p['pm'], p['pc'], l, seq)

    if paged is None:
        attn = _fox_prompt(qa, ka, va, nseq, seq)
        k_out = jnp.transpose(kn, (0, 3, 1, 2))
        v_out = jnp.transpose(v, (0, 3, 1, 2))
        logf = jnp.transpose(lf_t[0], (0, 2, 1))
    else:
        logf = fdt[:, F_LANE0:F_LANE0 + FOX_HEADS].reshape(nseq, seq, FOX_HEADS)
        k_out = kn.reshape(nseq, seq, FOX_HEADS, HEAD_DIM)
        v_out = v.reshape(nseq, seq, FOX_HEADS, HEAD_DIM)
        page_table, cache_kt, cache_vt, cache_lf_t = paged
        eye = jnp.eye(FOX_HEADS, dtype=BF16)
        q4 = jnp.transpose(qa.reshape(nseq, seq, FOX_HEADS, LANES)[..., :HEAD_DIM], (0, 2, 1, 3))
        qbd = (q4[:, :, :, None, :] * eye[None, :, None, :, None]).reshape(nseq, FOX_HEADS * seq, ATTN_DIM)
        attn = _fox_paged(l, page_table, qbd, kn.reshape(nseq, seq, ATTN_DIM), v.reshape(nseq, seq, ATTN_DIM),
                          jnp.transpose(logf, (0, 2, 1)), cache_kt, cache_vt, cache_lf_t,
                          nseq, seq).reshape(n, ATTN_DIM)

    yssd, h_end = _ssd(xbc, _pad_state(ssd_conv_state, SSD_CONV), fdt, z, ssd_state, p['ssd_conv_w'][l],
                       p['ssd_conv_b'][l][None, :], lp['av'], lp['dv'], nseq, seq)
    x = _merge(x, scb, cin, _pad_state(conv_state, SC_WIDTH), attn, yssd, p['sc_conv_w'][l],
               p['mix_out_norm'][l][None, :], p['w_out_bf'], l, nseq, seq)

    cin3 = cin.reshape(nseq, seq, CONV_DIM)
    xbc3 = xbc.reshape(nseq, seq, XBC_DIM)
    new_conv = jnp.concatenate([conv_state, cin3], axis=1)[:, -(SC_WIDTH - 1):]
    new_ssd_conv = jnp.concatenate([ssd_conv_state, xbc3], axis=1)[:, -(SSD_CONV - 1):]
    state = (k_out, v_out, logf, new_conv, new_ssd_conv, h_end)
    return x, state


def kernel(x_prompt, x_sample, cache_fox_k, cache_fox_v, cache_fox_logf, cache_mem_k, cache_mem_v,
           state_conv, state_ssd_conv, state_ssd, page_table, mem_prompt,
           ffn1_norm, ffn1_wg, ffn1_wu, ffn1_wd, mix_norm, w_in, sc_conv_w, fox_q_norm, fox_k_norm,
           fox_f_bias, ssd_conv_w, ssd_conv_b, ssd_dt_bias, ssd_A_log, ssd_D, mix_out_norm, w_out,
           xa_norm, mem_norm, xa_wq, xa_wk, xa_wv, xa_q_norm, xa_k_norm, xa_wo,
           ffn2_norm, ffn2_wg, ffn2_wu, ffn2_wd, final_norm):
    b_p, s_p, _ = x_prompt.shape
    b_s, s_s, _ = x_sample.shape
    n_phys = cache_fox_k.shape[1]
    pm, pc = _bias_placement()
    p = dict(mix_norm=mix_norm, w_in_t=_prep_w_in(w_in), sc_conv_w=sc_conv_w, fox_q_norm=fox_q_norm,
             fox_k_norm=fox_k_norm, fox_f_bias=fox_f_bias, ssd_conv_w=ssd_conv_w, ssd_conv_b=ssd_conv_b,
             ssd_dt_bias=ssd_dt_bias, ssd_A_log=ssd_A_log, ssd_D=ssd_D, mix_out_norm=mix_out_norm,
             xa_q_norm=xa_q_norm, w_out_bf=w_out.astype(BF16), pm=pm, pc=pc)
    wg1, wu1, wd1 = ffn1_wg.astype(BF16), ffn1_wu.astype(BF16), ffn1_wd.astype(BF16)
    wg2, wu2, wd2 = ffn2_wg.astype(BF16), ffn2_wu.astype(BF16), ffn2_wd.astype(BF16)
    wq, wo = xa_wq.astype(BF16), xa_wo.astype(BF16)
    fg = final_norm[None, :]

    km_p, vm_p = _memkv(mem_prompt, mem_norm[:, None, :], xa_wk.astype(BF16), xa_wv.astype(BF16),
                        jnp.tile(xa_k_norm, (1, MEM_HEADS))[:, None, :])
    km_s, vm_s = cache_mem_k, cache_mem_v
    cache_kt = jnp.transpose(cache_fox_k, (0, 1, 3, 4, 2)).reshape(DEPTH, n_phys, ATTN_DIM, PAGE_SIZE)
    cache_vt = jnp.transpose(cache_fox_v, (0, 1, 3, 4, 2)).reshape(DEPTH, n_phys, ATTN_DIM, PAGE_SIZE)
    cache_lf_t = jnp.transpose(cache_fox_logf, (0, 1, 3, 2))

    zero_conv = jnp.zeros((b_p, SC_WIDTH - 1, CONV_DIM), F32)
    zero_ssd_conv = jnp.zeros((b_p, SSD_CONV - 1, XBC_DIM), F32)
    zero_ssd = jnp.zeros((b_p, SSD_HEADS, HEAD_DIM, SSD_STATE), F32)

    xp = x_prompt.reshape(b_p * s_p, D_MODEL)
    xs = x_sample.reshape(b_s * s_s, D_MODEL)
    st_p, st_s = [], []
    for l in range(DEPTH):
        lp = _layer_params(l, p)
        last = l == DEPTH - 1
        groups = []
        for x, nseq, seq, km, vm, cs, scs, ss, paged in (
                (xp, b_p, s_p, km_p, vm_p, zero_conv, zero_ssd_conv, zero_ssd, None),
                (xs, b_s, s_s, km_s, vm_s, state_conv[l], state_ssd_conv[l], state_ssd[l],
                 (page_table, cache_kt, cache_vt, cache_lf_t))):
            x = _ffn(x, ffn1_norm[l][None, :], wg1, wu1, wd1, fg, l, False)
            x, st = _token_mix(l, x, p, lp, nseq, seq, cs, scs, ss, paged)
            x = _xattn(x, km, vm, l, xa_norm[l][None, :], wq, lp['xa_qg'], wo, nseq, seq)
            x = _ffn(x, ffn2_norm[l][None, :], wg2, wu2, wd2, fg, l, last)
            groups.append((x, st))
        (xp, sp), (xs, ss_) = groups
        st_p.append(sp)
        st_s.append(ss_)

    def stack(states, idx):
        return jnp.stack([s[idx] for s in states])

    return (xp.reshape(b_p, s_p, D_MODEL), xs.reshape(b_s, s_s, D_MODEL),
            stack(st_p, 0), stack(st_p, 1), stack(st_p, 2),
            km_p.reshape(DEPTH, b_p, MEM_LEN, MEM_HEADS, MEM_HEAD_DIM),
            vm_p.reshape(DEPTH, b_p, MEM_LEN, MEM_HEADS, MEM_HEAD_DIM),
            stack(st_p, 3), stack(st_p, 4), stack(st_p, 5),
            stack(st_s, 0), stack(st_s, 1), stack(st_s, 2), stack(st_s, 3), stack(st_s, 4), stack(st_s, 5))
```

```python
import functools

import numpy as np
import jax
import jax.numpy as jnp
from jax import lax
from jax.experimental import pallas as pl
from jax.experimental.pallas import tpu as pltpu

F32 = jnp.float32
BF16 = jnp.bfloat16

D_MODEL = 1024
DEPTH = 4
PAGE_SIZE = 128
HEAD_DIM = 64
CONV_DIM = 256
ATTN_DIM = 512
SSD_DIM = 256
FOX_HEADS = 8
SC_WIDTH = 3
SSD_HEADS = 4
SSD_STATE = 64
SSD_CONV = 4
SSD_CHUNK = 128
XBC_DIM = 512
FFN_DIM = 2816
MEM_LEN = 256
MEM_HEADS = 4
MEM_HEAD_DIM = 128
MEM_DIM = 512
RMS_EPS = 1e-6
LOG2E = 1.4426950408889634

LANES = 128
SUBLANES = 8
VMEM_LIMIT = 56 * 1024 * 1024
ROW_TILE = 512
PAGES_PER_STEP = 16
XATTN_SEQS_PER_STEP = 8
FOX_HEADS_PER_STEP = 4
SSD_CHUNKS_PER_STEP = 4
PROJ_QKV = 3 * CONV_DIM + 3 * ATTN_DIM
F_LANE0 = 0
DT_LANE0 = FOX_HEADS
PAD_DIM = FOX_HEADS * LANES
BIAS_LANE0 = HEAD_DIM
N_SPLIT = 3


def _params(sem):
    return pltpu.CompilerParams(dimension_semantics=sem, vmem_limit_bytes=VMEM_LIMIT)


def _resident(shape):
    nd = len(shape)
    return pl.BlockSpec(shape, lambda *_: (0,) * nd, pipeline_mode=pl.Buffered(1))


def _resident_layer(shape, layer):
    nd = len(shape)
    return pl.BlockSpec((None,) + tuple(shape), lambda *_: (layer,) + (0,) * nd, pipeline_mode=pl.Buffered(1))


def _rms(x, g):
    ms = jnp.mean(x * x, axis=-1, keepdims=True)
    return x * lax.rsqrt(ms + RMS_EPS) * g


def _head_rms(x, g, head_dim):
    width = x.shape[-1]
    pieces = []
    for c in range(width // LANES):
        blk = x[:, c * LANES:(c + 1) * LANES]
        sq = blk * blk
        s_all = jnp.sum(sq, axis=-1, keepdims=True)
        if head_dim == LANES:
            ms = s_all * (1.0 / LANES)
        else:
            lo = lax.broadcasted_iota(jnp.int32, sq.shape, 1) < head_dim
            s_lo = jnp.sum(jnp.where(lo, sq, 0.0), axis=-1, keepdims=True)
            ms = jnp.where(lo, s_lo, s_all - s_lo) * (1.0 / head_dim)
        pieces.append(blk * lax.rsqrt(ms + RMS_EPS))
    return jnp.concatenate(pieces, axis=-1) * g


def _silu(x):
    return x * jax.nn.sigmoid(x)


def _softplus(x):
    return jnp.maximum(x, 0.0) + jnp.log1p(jnp.exp(-jnp.abs(x)))


def _split3(x):
    hi = x.astype(BF16)
    r1 = x - hi.astype(F32)
    mid = r1.astype(BF16)
    lo = (r1 - mid.astype(F32)).astype(BF16)
    return hi, mid, lo


def _dot_exact_rhs(x, m_bf16):
    hi, mid, lo = _split3(x)
    out = jnp.dot(hi, m_bf16, preferred_element_type=F32)
    out = out + jnp.dot(mid, m_bf16, preferred_element_type=F32)
    return out + jnp.dot(lo, m_bf16, preferred_element_type=F32)


def _dot_exact_lhs(m_bf16, x):
    hi, mid, lo = _split3(x)
    out = jnp.dot(m_bf16, hi, preferred_element_type=F32)
    out = out + jnp.dot(m_bf16, mid, preferred_element_type=F32)
    return out + jnp.dot(m_bf16, lo, preferred_element_type=F32)


def _lower_tri(n, seq=None):
    r = lax.broadcasted_iota(jnp.int32, (n, n), 0)
    c = lax.broadcasted_iota(jnp.int32, (n, n), 1)
    keep = c <= r
    if seq is not None and seq < n:
        shift = jnp.int32(seq.bit_length() - 1)
        keep = keep & (lax.shift_right_logical(r, shift) == lax.shift_right_logical(c, shift))
    return keep.astype(BF16)


def _upper_tri(n):
    r = lax.broadcasted_iota(jnp.int32, (n, n), 0)
    c = lax.broadcasted_iota(jnp.int32, (n, n), 1)
    return (r <= c).astype(BF16)


def _nt(a, b):
    return lax.dot_general(a, b, (((1,), (1,)), ((), ())), preferred_element_type=F32)


def _tn(a, b):
    return lax.dot_general(a, b, (((0,), (0,)), ((), ())), preferred_element_type=F32)


def _row_tile(x, reps):
    return x if reps == 1 else jnp.concatenate([x] * reps, axis=0)


def _lane_tile(x, reps):
    return x if reps == 1 else jnp.concatenate([x] * reps, axis=-1)


def _ffn_body(x, g_ref, wg_ref, wu_ref, wd_ref, fg_ref, final):
    h = _rms(x, g_ref[...]).astype(BF16)
    a = jnp.dot(h, wg_ref[...], preferred_element_type=F32)
    b = jnp.dot(h, wu_ref[...], preferred_element_type=F32)
    t = (_silu(a) * b).astype(BF16)
    out = x + 0.5 * jnp.dot(t, wd_ref[...], preferred_element_type=F32)
    return _rms(out, fg_ref[...]) if final else out


def _ffn_kernel(x_ref, g_ref, wg_ref, wu_ref, wd_ref, fg_ref, o_ref, *, final):
    o_ref[...] = _ffn_body(x_ref[...], g_ref, wg_ref, wu_ref, wd_ref, fg_ref, final)


def _ffn(x, g, wg, wu, wd, fg, layer, final):
    n = x.shape[0]
    tm = min(ROW_TILE, n)
    return pl.pallas_call(
        functools.partial(_ffn_kernel, final=final),
        out_shape=jax.ShapeDtypeStruct((n, D_MODEL), F32),
        grid=(n // tm,),
        in_specs=[
            pl.BlockSpec((tm, D_MODEL), lambda i: (i, 0)),
            _resident((1, D_MODEL)),
            _resident_layer((D_MODEL, FFN_DIM), layer),
            _resident_layer((D_MODEL, FFN_DIM), layer),
            _resident_layer((FFN_DIM, D_MODEL), layer),
            _resident((1, D_MODEL)),
        ],
        out_specs=pl.BlockSpec((tm, D_MODEL), lambda i: (i, 0)),
        compiler_params=_params(("arbitrary",)),
        name="ffn",
    )(x, g, wg, wu, wd, fg)


def _expand_heads(x, fill):
    lo = lax.broadcasted_iota(jnp.int32, (x.shape[0], LANES), 1) < HEAD_DIM
    blocks = []
    for c in range(ATTN_DIM // LANES):
        blk = x[:, c * LANES:(c + 1) * LANES]
        for half, src in enumerate((blk, pltpu.roll(blk, HEAD_DIM, axis=1))):
            h = 2 * c + half
            blocks.append(jnp.where(lo, src, fill[:, h * LANES:(h + 1) * LANES]))
    return jnp.concatenate(blocks, axis=-1).astype(BF16)


def _inproj_kernel(x_ref, g_ref, w_ref, wzx_ref, wfd_ref, qg_ref, kg_ref, fb_ref, pm_ref, pc_ref,
                   scb_ref, cin_ref, qa_ref, kn_ref, ka_ref, v_ref, va_ref,
                   z_ref, xbc_ref, fdt_ref, *rest, tm, seq, dim_major):
    carry_ref = rest[-1]
    h = _rms(x_ref[...], g_ref[...]).astype(BF16)

    def store_kv(ref, val):
        ref[...] = val.T.reshape(FOX_HEADS, HEAD_DIM, tm) if dim_major else val

    def proj(a, b, ref=w_ref):
        return _nt(h, ref[a:b, :])

    o = 0
    scb_ref[...] = proj(o, o + CONV_DIM)
    o += CONV_DIM
    cin_ref[...] = proj(o, o + CONV_DIM) * proj(o + CONV_DIM, o + 2 * CONV_DIM)
    o += 2 * CONV_DIM
    q = _head_rms(proj(o, o + ATTN_DIM), qg_ref[...], HEAD_DIM) * (HEAD_DIM ** -0.5 * LOG2E)
    o += ATTN_DIM
    k = _head_rms(proj(o, o + ATTN_DIM), kg_ref[...], HEAD_DIM)
    store_kv(kn_ref, k)
    o += ATTN_DIM
    v = proj(o, o + ATTN_DIM)
    store_kv(v_ref, v)
    z_ref[...] = proj(0, SSD_DIM, wzx_ref)
    xbc_ref[...] = proj(SSD_DIM, SSD_DIM + XBC_DIM, wzx_ref)
    u = proj(0, LANES, wfd_ref) + fb_ref[...]
    lane = lax.broadcasted_iota(jnp.int32, u.shape, 1)
    fdt = jnp.where(lane < DT_LANE0, -_softplus(-u), _softplus(u))
    fdt_ref[...] = fdt
    if dim_major:
        rest[0][...] = fdt.T[0:FOX_HEADS, :]

    @pl.when(pl.program_id(0) % max(seq // tm, 1) == 0)
    def _():
        carry_ref[...] = jnp.zeros_like(carry_ref)

    lf = jnp.where(lane < FOX_HEADS, fdt, 0.0) * LOG2E
    c = _dot_exact_lhs(_lower_tri(tm, seq), lf) + carry_ref[0:1, :]
    carry_ref[...] = jnp.broadcast_to(c[tm - 1:tm, :], carry_ref.shape)
    hi, mid, lo = _split3(c)
    packed = (hi.astype(F32) + pltpu.roll(mid.astype(F32), FOX_HEADS, axis=1)
              + pltpu.roll(lo.astype(F32), 2 * FOX_HEADS, axis=1)).astype(BF16)
    bias = jnp.dot(packed, pm_ref[...], preferred_element_type=F32) + pc_ref[:, 0:2 * PAD_DIM]
    qa_ref[...] = _expand_heads(q, bias[:, 0:PAD_DIM])
    ka_ref[...] = _expand_heads(k, bias[:, PAD_DIM:2 * PAD_DIM])
    va_ref[...] = _expand_heads(v, pc_ref[:, 2 * PAD_DIM:3 * PAD_DIM])


def _bias_placement():
    pm = np.zeros((LANES, 2 * PAD_DIM), np.float32)
    pc = np.zeros((1, 3 * PAD_DIM), np.float32)
    for h in range(FOX_HEADS):
        for part in range(N_SPLIT):
            src = part * FOX_HEADS + h
            pm[src, h * LANES + BIAS_LANE0 + part] = 1.0
            pm[src, PAD_DIM + h * LANES + BIAS_LANE0 + N_SPLIT + part] = -1.0
            pc[0, h * LANES + BIAS_LANE0 + N_SPLIT + part] = 1.0
            pc[0, PAD_DIM + h * LANES + BIAS_LANE0 + part] = 1.0
        pc[0, 2 * PAD_DIM + h * LANES + HEAD_DIM] = 1.0
    return jnp.asarray(pm, BF16), jnp.asarray(pc, F32)


def _inproj(x, g, w, qg, kg, fb, pm, pc, layer, seq):
    w_t, w_zx, w_fd = w
    n = x.shape[0]
    tm = min(ROW_TILE, n)
    widths = [(CONV_DIM, F32), (CONV_DIM, F32), (PAD_DIM, BF16), (ATTN_DIM, F32), (PAD_DIM, BF16),
              (ATTN_DIM, F32), (PAD_DIM, BF16), (SSD_DIM, F32), (XBC_DIM, F32), (LANES, F32)]
    out_shape = [jax.ShapeDtypeStruct((n, w_), dt) for w_, dt in widths]
    out_specs = [pl.BlockSpec((tm, w_), lambda i: (i, 0)) for w_, _ in widths]
    dim_major = seq >= tm
    if dim_major:
        tps = seq // tm
        kv_shape = jax.ShapeDtypeStruct((n // seq, FOX_HEADS, HEAD_DIM, seq), F32)
        kv_spec = pl.BlockSpec((None, FOX_HEADS, HEAD_DIM, tm), lambda i: (i // tps, 0, 0, i % tps))
        out_shape[3], out_shape[5] = kv_shape, kv_shape
        out_specs[3], out_specs[5] = kv_spec, kv_spec
        out_shape.append(jax.ShapeDtypeStruct((n // seq, FOX_HEADS, seq), F32))
        out_specs.append(pl.BlockSpec((None, FOX_HEADS, tm), lambda i: (i // tps, 0, i % tps)))
    return pl.pallas_call(
        functools.partial(_inproj_kernel, tm=tm, seq=seq, dim_major=dim_major),
        out_shape=out_shape,
        grid=(n // tm,),
        in_specs=[
            pl.BlockSpec((tm, D_MODEL), lambda i: (i, 0)),
            _resident((1, D_MODEL)),
            _resident_layer((PROJ_QKV, D_MODEL), layer),
            _resident_layer((SSD_DIM + XBC_DIM, D_MODEL), layer),
            _resident_layer((LANES, D_MODEL), layer),
            _resident((1, ATTN_DIM)),
            _resident((1, ATTN_DIM)),
            _resident((1, LANES)),
            _resident((LANES, 2 * PAD_DIM)),
            _resident((1, 3 * PAD_DIM)),
        ],
        out_specs=out_specs,
        scratch_shapes=[pltpu.VMEM((SUBLANES, LANES), F32)],
        compiler_params=_params(("arbitrary",)),
        name="inproj",
    )(x, g, w_t, w_zx, w_fd, qg, kg, fb, pm, pc)


def _fox_kernel(q_ref, k_ref, v_ref, o_ref, m_ref, acc_ref, *, tq):
    i = pl.program_id(2)
    m_ref[...] = jnp.full(m_ref.shape, -jnp.inf, F32)
    acc_ref[...] = jnp.zeros_like(acc_ref)

    def tile(start, width, masked):
        rows = pl.ds(pl.multiple_of(start, width), width)
        for hh in range(FOX_HEADS_PER_STEP):
            sl = slice(hh * LANES, (hh + 1) * LANES)
            s = _nt(q_ref[:, sl], k_ref[rows, sl])
            if masked:
                row = lax.broadcasted_iota(jnp.int32, s.shape, 0)
                col = lax.broadcasted_iota(jnp.int32, s.shape, 1)
                s = jnp.where(col <= row, s, -jnp.inf)
            m_old = m_ref[hh]
            m_new = jnp.maximum(m_old, jnp.max(s, axis=-1, keepdims=True))
            p = jnp.exp2(s - _lane_tile(m_new, width // LANES))
            acc_ref[hh] = jnp.exp2(m_old - m_new) * acc_ref[hh] + jnp.dot(
                p.astype(BF16), v_ref[rows, sl], preferred_element_type=F32)
            m_ref[hh] = m_new

    def body(j, carry):
        tile(j * (2 * tq), 2 * tq, False)
        return carry

    lax.fori_loop(0, lax.shift_right_logical(i, 1), body, 0)

    @pl.when(lax.bitwise_and(i, 1) == 1)
    def _():
        tile((i - 1) * tq, tq, False)

    tile(i * tq, tq, True)
    outs = []
    for hh in range(FOX_HEADS_PER_STEP):
        a = acc_ref[hh]
        outs.append(a[:, 0:HEAD_DIM] / a[:, HEAD_DIM:HEAD_DIM + 1])
    o_ref[...] = jnp.concatenate(outs, axis=-1)


def _fox_prompt(qa, ka, va, nseq, seq):
    tq = min(ROW_TILE, seq)
    nq = seq // tq
    nh = FOX_HEADS_PER_STEP
    return pl.pallas_call(
        functools.partial(_fox_kernel, tq=tq),
        out_shape=jax.ShapeDtypeStruct((nseq * seq, ATTN_DIM), F32),
        grid=(nseq, FOX_HEADS // nh, nq),
        in_specs=[
            pl.BlockSpec((tq, nh * LANES), lambda b, h, i: (b * nq + i, h)),
            pl.BlockSpec((seq, nh * LANES), lambda b, h, i: (b, h)),
            pl.BlockSpec((seq, nh * LANES), lambda b, h, i: (b, h)),
        ],
        out_specs=pl.BlockSpec((tq, nh * HEAD_DIM), lambda b, h, i: (b * nq + i, h)),
        scratch_shapes=[pltpu.VMEM((nh, tq, LANES), F32), pltpu.VMEM((nh, tq, LANES), F32)],
        compiler_params=_params(("arbitrary",) * 3),
        name="fox_prompt",
    )(qa, ka, va)


def _fox_paged_kernel(pt_ref, q_ref, kn_ref, vn_ref, lfn_ref, *rest, tokens):
    npg = PAGES_PER_STEP
    k_refs = rest[:npg]
    v_refs = rest[npg:2 * npg]
    lf_refs = rest[2 * npg:3 * npg]
    o_ref, m_ref, l_ref, acc_ref, carry_ref, c_ref = rest[3 * npg:]
    g = pl.program_id(1)
    rows = FOX_HEADS * tokens

    @pl.when(g == 0)
    def _():
        m_ref[...] = jnp.full(m_ref.shape, -jnp.inf, F32)
        l_ref[...] = jnp.zeros_like(l_ref)
        acc_ref[...] = jnp.zeros_like(acc_ref)
        carry_ref[...] = jnp.zeros_like(carry_ref)

    q = q_ref[...]
    tri = _upper_tri(PAGE_SIZE)

    def update(k_list, v_list, lf_list):
        n = len(k_list)
        nr = n * FOX_HEADS
        parts = _split3(jnp.concatenate(lf_list, axis=0) * LOG2E)

        def times(m_bf16):
            return sum(jnp.dot(x, m_bf16, preferred_element_type=F32) for x in parts)

        c = times(tri) + _row_tile(carry_ref[...], n)
        if n > 1:
            tot = times(jnp.ones((PAGE_SIZE, PAGE_SIZE), BF16))
            r = lax.broadcasted_iota(jnp.int32, (nr, nr), 0)
            col = lax.broadcasted_iota(jnp.int32, (nr, nr), 1)
            same_head = lax.bitwise_and(r, FOX_HEADS - 1) == lax.bitwise_and(col, FOX_HEADS - 1)
            earlier = jnp.where(col < r, jnp.where(same_head, 1.0, 0.0), 0.0).astype(BF16)
            off = _dot_exact_lhs(earlier, tot)
            c = c + off
            carry_ref[...] = carry_ref[...] + off[nr - FOX_HEADS:nr, :] + tot[nr - FOX_HEADS:nr, :]
        c_ref[0:nr, :] = c
        s_list = []
        for idx in range(n):
            c_rows = jnp.concatenate(
                [jnp.broadcast_to(c_ref[idx * FOX_HEADS + h:idx * FOX_HEADS + h + 1, :], (tokens, PAGE_SIZE))
                 for h in range(FOX_HEADS)], axis=0)
            s_list.append(jnp.dot(q, k_list[idx].astype(BF16), preferred_element_type=F32) - c_rows)
        s = jnp.concatenate(s_list, axis=-1)
        m_old = m_ref[...]
        m_new = jnp.maximum(m_old, jnp.max(s, axis=-1, keepdims=True))
        alpha = jnp.exp2(m_old - m_new)
        p = jnp.exp2(s - _lane_tile(m_new, n))
        l_ref[...] = alpha * l_ref[...] + jnp.sum(p, axis=-1, keepdims=True)
        pv = _nt(p[:, 0:PAGE_SIZE].astype(BF16), v_list[0].astype(BF16))
        for idx in range(1, n):
            pv = pv + _nt(p[:, idx * PAGE_SIZE:(idx + 1) * PAGE_SIZE].astype(BF16), v_list[idx].astype(BF16))
        acc_ref[...] = _lane_tile(alpha, ATTN_DIM // LANES) * acc_ref[...] + pv
        m_ref[...] = m_new

    update([r[...] for r in k_refs], [r[...] for r in v_refs], [r[...] for r in lf_refs])

    @pl.when(g == pl.num_programs(1) - 1)
    def _():
        c_new = _dot_exact_rhs(lfn_ref[...] * LOG2E, _upper_tri(tokens)) + carry_ref[:, 0:tokens]
        c_ref[0:FOX_HEADS, 0:tokens] = c_new
        c_rows = jnp.concatenate(
            [jnp.broadcast_to(c_ref[h:h + 1, 0:tokens], (tokens, tokens)) for h in range(FOX_HEADS)], axis=0)
        s = _nt(q, kn_ref[...].astype(BF16)) - c_rows
        t_idx = lax.bitwise_and(lax.broadcasted_iota(jnp.int32, s.shape, 0), tokens - 1)
        s = jnp.where(lax.broadcasted_iota(jnp.int32, s.shape, 1) <= t_idx, s, -jnp.inf)
        m_old = m_ref[...]
        m_new = jnp.maximum(m_old, jnp.max(s, axis=-1, keepdims=True))
        alpha = jnp.exp2(m_old - m_new)
        p = jnp.exp2(s - m_new[:, 0:tokens])
        l_new = alpha * l_ref[...] + jnp.sum(p, axis=-1, keepdims=True)
        acc = _lane_tile(alpha, ATTN_DIM // LANES) * acc_ref[...] + jnp.dot(
            p.astype(BF16), vn_ref[...].astype(BF16), preferred_element_type=F32)
        full = acc / _lane_tile(l_new, ATTN_DIM // LANES)
        lane_head = lax.shift_right_logical(
            lax.broadcasted_iota(jnp.int32, (tokens, ATTN_DIM), 1), jnp.int32(HEAD_DIM.bit_length() - 1))
        out = jnp.zeros((tokens, ATTN_DIM), F32)
        for h in range(FOX_HEADS):
            out = out + jnp.where(lane_head == h, full[h * tokens:(h + 1) * tokens, :], 0.0)
        o_ref[...] = out


def _fox_paged(layer, page_table, qbd, kn_t, vn_t, lfn_t, cache_kt, cache_vt, cache_lf_t, nseq, tokens):
    npg = PAGES_PER_STEP
    n_groups = page_table.shape[1] // npg
    rows = FOX_HEADS * tokens

    def page_spec(width2, n):
        return pl.BlockSpec((None, None, width2, PAGE_SIZE),
                            lambda b, g, pt: (layer, pt[b, g * npg + n], 0, 0))

    in_specs = [
        pl.BlockSpec((None, rows, ATTN_DIM), lambda b, g, pt: (b, 0, 0)),
        pl.BlockSpec((None, tokens, ATTN_DIM), lambda b, g, pt: (b, 0, 0)),
        pl.BlockSpec((None, tokens, ATTN_DIM), lambda b, g, pt: (b, 0, 0)),
        pl.BlockSpec((None, FOX_HEADS, tokens), lambda b, g, pt: (b, 0, 0)),
    ]
    in_specs += [page_spec(ATTN_DIM, n) for n in range(npg)]
    in_specs += [page_spec(ATTN_DIM, n) for n in range(npg)]
    in_specs += [page_spec(FOX_HEADS, n) for n in range(npg)]
    grid_spec = pltpu.PrefetchScalarGridSpec(
        num_scalar_prefetch=1,
        grid=(nseq, n_groups),
        in_specs=in_specs,
        out_specs=pl.BlockSpec((None, tokens, ATTN_DIM), lambda b, g, pt: (b, 0, 0)),
        scratch_shapes=[pltpu.VMEM((rows, LANES), F32), pltpu.VMEM((rows, LANES), F32),
                        pltpu.VMEM((rows, ATTN_DIM), F32), pltpu.VMEM((FOX_HEADS, LANES), F32),
                        pltpu.VMEM((npg * FOX_HEADS, LANES), F32)],
    )
    return pl.pallas_call(
        functools.partial(_fox_paged_kernel, tokens=tokens),
        out_shape=jax.ShapeDtypeStruct((nseq, tokens, ATTN_DIM), F32),
        grid_spec=grid_spec,
        compiler_params=_params(("arbitrary", "arbitrary")),
        name="fox_paged",
    )(page_table, qbd, kn_t, vn_t, lfn_t,
      *([cache_kt] * npg), *([cache_vt] * npg), *([cache_lf_t] * npg))


def _ssd_kernel(xbc_ref, halo_ref, st_ref, fdt_ref, z_ref, h0_ref,
                cw_ref, cb_ref, av_ref, dv_ref,
                y_ref, hout_ref, buf_ref, h_ref, *, lc, rows):
    j = pl.program_id(1)

    @pl.when(j == 0)
    def _():
        h_ref[...] = h0_ref[...]
        buf_ref[0:SUBLANES, :] = st_ref[...]

    @pl.when(j > 0)
    def _():
        buf_ref[0:SUBLANES, :] = halo_ref[...]

    buf_ref[SUBLANES:SUBLANES + rows, :] = xbc_ref[...]
    base = SUBLANES - (SSD_CONV - 1)
    u_all = buf_ref[base:base + rows, :] * cw_ref[0:1, :]
    for k in range(1, SSD_CONV):
        u_all = u_all + buf_ref[base + k:base + k + rows, :] * cw_ref[k:k + 1, :]
    u_all = _silu(u_all + cb_ref[...])

    tri_lo = _lower_tri(lc)
    tri_up = _upper_tri(lc)
    pick = (lax.broadcasted_iota(jnp.int32, (SUBLANES, LANES), 1)
            == lax.broadcasted_iota(jnp.int32, (SUBLANES, LANES), 0) + DT_LANE0).astype(BF16)

    def rows_of(x):
        return sum(_nt(pick, part) for part in _split3(x))

    row = lax.broadcasted_iota(jnp.int32, (lc, lc), 0)
    col = lax.broadcasted_iota(jnp.int32, (lc, lc), 1)
    causal = col <= row
    dvec = dv_ref[...]

    for c in range(rows // lc):
        tok = slice(c * lc, (c + 1) * lc)
        u = u_all[tok, :]
        z = z_ref[tok, :]
        fdt = fdt_ref[tok, :]
        da = fdt * av_ref[...]
        acum = _dot_exact_lhs(tri_lo, da)
        dtt = rows_of(fdt)
        acum_t = _dot_exact_rhs(rows_of(da), tri_up)
        outs = []
        for h in range(SSD_HEADS):
            grp = h // (SSD_HEADS // 2)
            xh = u[:, h * HEAD_DIM:(h + 1) * HEAD_DIM]
            bh = u[:, SSD_DIM + grp * SSD_STATE:SSD_DIM + (grp + 1) * SSD_STATE]
            ch = u[:, SSD_DIM + 2 * SSD_STATE + grp * SSD_STATE:SSD_DIM + 2 * SSD_STATE + (grp + 1) * SSD_STATE]
            lane = DT_LANE0 + h
            a_col = acum[:, lane:lane + 1]
            dt_col = fdt[:, lane:lane + 1]
            a_row = acum_t[h:h + 1, :]
            dt_row = dtt[h:h + 1, :]
            a_last = acum[lc - 1:lc, lane:lane + 1]
            decay = jnp.exp(jnp.where(causal, a_col - a_row, -jnp.inf))
            xb = xh.astype(BF16)
            bb = bh.astype(BF16)
            cb16 = ch.astype(BF16)
            w = _nt(cb16, bb) * decay * dt_row
            h_prev = h_ref[h]
            y = jnp.dot(w.astype(BF16), xb, preferred_element_type=F32)
            y = y + _nt(cb16, h_prev.astype(BF16)) * jnp.exp(a_col)
            w_end = jnp.exp(a_last - a_col) * dt_col
            h_ref[h] = h_prev * jnp.exp(a_last) + _tn((xh * w_end).astype(BF16), bb)
            sl = slice(h * HEAD_DIM, (h + 1) * HEAD_DIM)
            outs.append((y + dvec[:, sl] * xh) * _silu(z[:, sl]))
        y_ref[tok, :] = jnp.concatenate(outs, axis=-1)

    @pl.when(j == pl.num_programs(1) - 1)
    def _():
        hout_ref[...] = h_ref[...]


def _ssd(xbc, st_pad, fdt, z, h0, cw, cb, av, dv, nseq, seq):
    lc = min(SSD_CHUNK, seq)
    rows = min(SSD_CHUNKS_PER_STEP * lc, seq)
    nc = seq // rows
    hb = rows // SUBLANES
    return pl.pallas_call(
        functools.partial(_ssd_kernel, lc=lc, rows=rows),
        out_shape=[jax.ShapeDtypeStruct((nseq * seq, SSD_DIM), F32),
                   jax.ShapeDtypeStruct((nseq, SSD_HEADS, HEAD_DIM, SSD_STATE), F32)],
        grid=(nseq, nc),
        in_specs=[
            pl.BlockSpec((rows, XBC_DIM), lambda b, j: (b * nc + j, 0)),
            pl.BlockSpec((SUBLANES, XBC_DIM), lambda b, j: (jnp.maximum((b * nc + j) * hb - 1, 0), 0)),
            pl.BlockSpec((None, SUBLANES, XBC_DIM), lambda b, j: (b, 0, 0)),
            pl.BlockSpec((rows, LANES), lambda b, j: (b * nc + j, 0)),
            pl.BlockSpec((rows, SSD_DIM), lambda b, j: (b * nc + j, 0)),
            pl.BlockSpec((None, SSD_HEADS, HEAD_DIM, SSD_STATE), lambda b, j: (b, 0, 0, 0)),
            _resident((SSD_CONV, XBC_DIM)),
            _resident((1, XBC_DIM)),
            _resident((1, LANES)),
            _resident((1, SSD_DIM)),
        ],
        out_specs=[pl.BlockSpec((rows, SSD_DIM), lambda b, j: (b * nc + j, 0)),
                   pl.BlockSpec((None, SSD_HEADS, HEAD_DIM, SSD_STATE), lambda b, j: (b, 0, 0, 0))],
        scratch_shapes=[pltpu.VMEM((SUBLANES + rows, XBC_DIM), F32),
                        pltpu.VMEM((SSD_HEADS, HEAD_DIM, SSD_STATE), F32)],
        compiler_params=_params(("arbitrary", "arbitrary")),
        name="ssd",
    )(xbc, xbc, st_pad, fdt, z, h0, cw, cb, av, dv)


def _merge_body(x_ref, scb_ref, cin_ref, halo_ref, st_ref, attn_ref, yssd_ref,
                cw_ref, g_ref, w_ref, buf_ref, tl):
    j = pl.program_id(1)

    @pl.when(j == 0)
    def _():
        buf_ref[0:SUBLANES, :] = st_ref[...]

    @pl.when(j > 0)
    def _():
        buf_ref[0:SUBLANES, :] = halo_ref[...]

    buf_ref[SUBLANES:SUBLANES + tl, :] = cin_ref[...]
    base = SUBLANES - (SC_WIDTH - 1)
    y = buf_ref[base:base + tl, :] * cw_ref[0:1, :]
    for k in range(1, SC_WIDTH):
        y = y + buf_ref[base + k:base + k + tl, :] * cw_ref[k:k + 1, :]
    cat = jnp.concatenate([scb_ref[...] * y, attn_ref[...], yssd_ref[...]], axis=-1)
    cat = _head_rms(cat, g_ref[...], HEAD_DIM).astype(BF16)
    return x_ref[...] + jnp.dot(cat, w_ref[...], preferred_element_type=F32)


def _merge_kernel(x_ref, scb_ref, cin_ref, halo_ref, st_ref, attn_ref, yssd_ref,
                  cw_ref, g_ref, w_ref, o_ref, buf_ref, *, tl):
    o_ref[...] = _merge_body(x_ref, scb_ref, cin_ref, halo_ref, st_ref, attn_ref, yssd_ref,
                             cw_ref, g_ref, w_ref, buf_ref, tl)


def _merge(x, scb, cin, st_pad, attn, yssd, cw, g, w, layer, nseq, seq):
    tl = min(ROW_TILE, seq)
    nt = seq // tl
    hb = tl // SUBLANES

    def rows(width):
        return pl.BlockSpec((tl, width), lambda b, j: (b * nt + j, 0))

    return pl.pallas_call(
        functools.partial(_merge_kernel, tl=tl),
        out_shape=jax.ShapeDtypeStruct((nseq * seq, D_MODEL), F32),
        grid=(nseq, nt),
        in_specs=[
            rows(D_MODEL), rows(CONV_DIM), rows(CONV_DIM),
            pl.BlockSpec((SUBLANES, CONV_DIM), lambda b, j: (jnp.maximum((b * nt + j) * hb - 1, 0), 0)),
            pl.BlockSpec((None, SUBLANES, CONV_DIM), lambda b, j: (b, 0, 0)),
            rows(ATTN_DIM), rows(SSD_DIM),
            _resident((SC_WIDTH, CONV_DIM)),
            _resident((1, D_MODEL)),
            _resident_layer((D_MODEL, D_MODEL), layer),
        ],
        out_specs=rows(D_MODEL),
        scratch_shapes=[pltpu.VMEM((SUBLANES + tl, CONV_DIM), F32)],
        compiler_params=_params(("arbitrary", "arbitrary")),
        name="merge",
    )(x, scb, cin, cin, st_pad, attn, yssd, cw, g, w)


def _memkv_kernel(mem_ref, g_ref, wk_ref, wv_ref, kg_ref, km_ref, vm_ref):
    m = _rms(mem_ref[...], g_ref[...]).astype(BF16)
    km_ref[...] = _head_rms(jnp.dot(m, wk_ref[...], preferred_element_type=F32), kg_ref[...], MEM_HEAD_DIM)
    vm_ref[...] = jnp.dot(m, wv_ref[...], preferred_element_type=F32)


def _memkv(mem, g, wk, wv, kg):
    nb = mem.shape[0]
    out = jax.ShapeDtypeStruct((DEPTH, nb, MEM_LEN, MEM_DIM), F32)
    return pl.pallas_call(
        _memkv_kernel,
        out_shape=[out, out],
        grid=(DEPTH, nb),
        in_specs=[
            pl.BlockSpec((None, MEM_LEN, D_MODEL), lambda l, b: (b, 0, 0)),
            pl.BlockSpec((None, 1, D_MODEL), lambda l, b: (l, 0, 0)),
            pl.BlockSpec((None, D_MODEL, MEM_DIM), lambda l, b: (l, 0, 0)),
            pl.BlockSpec((None, D_MODEL, MEM_DIM), lambda l, b: (l, 0, 0)),
            pl.BlockSpec((None, 1, MEM_DIM), lambda l, b: (l, 0, 0)),
        ],
        out_specs=[pl.BlockSpec((None, None, MEM_LEN, MEM_DIM), lambda l, b: (l, b, 0, 0))] * 2,
        compiler_params=_params(("arbitrary", "arbitrary")),
        name="mem_kv",
    )(mem, g, wk, wv, kg)


def _xattn_body(x, km_ref, vm_ref, g_ref, wq_ref, qg_ref, wo_ref, q_ref, att_ref, bs, tl):
    h = _rms(x, g_ref[...]).astype(BF16)
    q = _head_rms(jnp.dot(h, wq_ref[...], preferred_element_type=F32), qg_ref[...], MEM_HEAD_DIM)
    q = q * (MEM_HEAD_DIM ** -0.5)
    for hh in range(MEM_HEADS):
        q_ref[hh] = q[:, hh * MEM_HEAD_DIM:(hh + 1) * MEM_HEAD_DIM]

    def softmax(sc):
        p = jnp.exp(sc - jnp.max(sc, axis=-1, keepdims=True))
        return (p / jnp.sum(p, axis=-1, keepdims=True)).astype(BF16)

    for s in range(bs):
        rows = slice(s * tl, (s + 1) * tl)
        if len(km_ref.shape) == 4:
            km = km_ref[s].reshape(MEM_LEN * MEM_HEADS, MEM_HEAD_DIM).astype(BF16)
            vm = vm_ref[s].reshape(MEM_LEN * MEM_HEADS, MEM_HEAD_DIM).astype(BF16)
            qs = jnp.concatenate([q_ref[hh, rows, :] for hh in range(MEM_HEADS)], axis=0).astype(BF16)
            sc = _nt(qs, km)
            row_head = lax.shift_right_logical(lax.broadcasted_iota(jnp.int32, sc.shape, 0),
                                               jnp.int32(tl.bit_length() - 1))
            col_head = lax.bitwise_and(lax.broadcasted_iota(jnp.int32, sc.shape, 1), MEM_HEADS - 1)
            out = jnp.dot(softmax(jnp.where(row_head == col_head, sc, -jnp.inf)), vm,
                          preferred_element_type=F32)
            for hh in range(MEM_HEADS):
                att_ref[hh, rows, :] = out[hh * tl:(hh + 1) * tl, :]
        else:
            for hh in range(MEM_HEADS):
                sl = slice(hh * MEM_HEAD_DIM, (hh + 1) * MEM_HEAD_DIM)
                sc = _nt(q_ref[hh, rows, :].astype(BF16), km_ref[s, :, sl].astype(BF16))
                att_ref[hh, rows, :] = jnp.dot(softmax(sc), vm_ref[s, :, sl].astype(BF16),
                                               preferred_element_type=F32)

    att = jnp.concatenate([att_ref[hh] for hh in range(MEM_HEADS)], axis=-1).astype(BF16)
    return x + jnp.dot(att, wo_ref[...], preferred_element_type=F32)


def _xattn_kernel(x_ref, km_ref, vm_ref, g_ref, wq_ref, qg_ref, wo_ref, o_ref, q_ref, att_ref, *, bs, tl):
    o_ref[...] = _xattn_body(x_ref[...], km_ref, vm_ref, g_ref, wq_ref, qg_ref, wo_ref, q_ref, att_ref, bs, tl)


def _xattn(x, km, vm, layer, g, wq, qg, wo, nseq, seq):
    tl = min(ROW_TILE, seq)
    nt = seq // tl
    bs = 1 if nt > 1 else min(nseq, XATTN_SEQS_PER_STEP)
    rows = bs * tl
    mem_zeros = (0,) * (km.ndim - 2)
    mem_spec = pl.BlockSpec((None, bs) + tuple(km.shape[2:]), lambda b, j: (layer, b) + mem_zeros)
    row_spec = pl.BlockSpec((rows, D_MODEL), lambda b, j: (b * nt + j, 0))
    return pl.pallas_call(
        functools.partial(_xattn_kernel, bs=bs, tl=tl),
        out_shape=jax.ShapeDtypeStruct((nseq * seq, D_MODEL), F32),
        grid=(nseq // bs, nt),
        in_specs=[
            row_spec, mem_spec, mem_spec,
            _resident((1, D_MODEL)),
            _resident_layer((D_MODEL, MEM_DIM), layer),
            _resident((1, MEM_DIM)),
            _resident_layer((MEM_DIM, D_MODEL), layer),
        ],
        out_specs=row_spec,
        scratch_shapes=[pltpu.VMEM((MEM_HEADS, rows, MEM_HEAD_DIM), F32),
                        pltpu.VMEM((MEM_HEADS, rows, MEM_HEAD_DIM), F32)],
        compiler_params=_params(("arbitrary", "arbitrary")),
        name="xattn",
    )(x, km, vm, g, wq, qg, wo)


def _merge_xattn_kernel(x_ref, scb_ref, cin_ref, halo_ref, st_ref, attn_ref, yssd_ref, cw_ref, go_ref,
                        wout_ref, km_ref, vm_ref, gx_ref, wq_ref, qg_ref, wo_ref,
                        o_ref, buf_ref, q_ref, att_ref, *, tl):
    x = _merge_body(x_ref, scb_ref, cin_ref, halo_ref, st_ref, attn_ref, yssd_ref,
                    cw_ref, go_ref, wout_ref, buf_ref, tl)
    o_ref[...] = _xattn_body(x, km_ref, vm_ref, gx_ref, wq_ref, qg_ref, wo_ref, q_ref, att_ref, 1, tl)


def _merge_xattn(x, scb, cin, st_pad, attn, yssd, cw, g_out, w_out, km, vm, g_xa, wq, qg, wo,
                 layer, nseq, seq):
    tl = ROW_TILE
    nt = seq // tl
    hb = tl // SUBLANES

    def rows(width):
        return pl.BlockSpec((tl, width), lambda b, j: (b * nt + j, 0))

    mem_spec = pl.BlockSpec((None, 1, MEM_LEN, MEM_DIM), lambda b, j: (layer, b, 0, 0))
    return pl.pallas_call(
        functools.partial(_merge_xattn_kernel, tl=tl),
        out_shape=jax.ShapeDtypeStruct((nseq * seq, D_MODEL), F32),
        grid=(nseq, nt),
        in_specs=[
            rows(D_MODEL), rows(CONV_DIM), rows(CONV_DIM),
            pl.BlockSpec((SUBLANES, CONV_DIM), lambda b, j: (jnp.maximum((b * nt + j) * hb - 1, 0), 0)),
            pl.BlockSpec((None, SUBLANES, CONV_DIM), lambda b, j: (b, 0, 0)),
            rows(ATTN_DIM), rows(SSD_DIM),
            _resident((SC_WIDTH, CONV_DIM)),
            _resident((1, D_MODEL)),
            _resident_layer((D_MODEL, D_MODEL), layer),
            mem_spec, mem_spec,
            _resident((1, D_MODEL)),
            _resident_layer((D_MODEL, MEM_DIM), layer),
            _resident((1, MEM_DIM)),
            _resident_layer((MEM_DIM, D_MODEL), layer),
        ],
        out_specs=rows(D_MODEL),
        scratch_shapes=[pltpu.VMEM((SUBLANES + tl, CONV_DIM), F32),
                        pltpu.VMEM((MEM_HEADS, tl, MEM_HEAD_DIM), F32),
                        pltpu.VMEM((MEM_HEADS, tl, MEM_HEAD_DIM), F32)],
        compiler_params=_params(("arbitrary", "arbitrary")),
        name="merge_xattn",
    )(x, scb, cin, cin, st_pad, attn, yssd, cw, g_out, w_out, km, vm, g_xa, wq, qg, wo)


def _pad_state(state, width):
    nseq, _, c = state.shape
    return jnp.concatenate([jnp.zeros((nseq, SUBLANES - (width - 1), c), F32), state], axis=1)


def _prep_w_in(w_in):
    w_t = jnp.transpose(w_in, (0, 2, 1)).astype(BF16)
    n_z0 = PROJ_QKV + FOX_HEADS
    n_dt0 = n_z0 + SSD_DIM + XBC_DIM
    pad = jnp.zeros((DEPTH, LANES - FOX_HEADS - SSD_HEADS, D_MODEL), BF16)
    w_fd = jnp.concatenate([w_t[:, PROJ_QKV:n_z0], w_t[:, n_dt0:], pad], axis=1)
    return w_t, w_t[:, n_z0:n_dt0], w_fd


def _layer_params(l, p):
    pad = jnp.zeros((LANES - FOX_HEADS - SSD_HEADS,), F32)
    a_neg = -jnp.exp(p['ssd_A_log'][l])
    return dict(
        fb=jnp.concatenate([p['fox_f_bias'][l], p['ssd_dt_bias'][l], pad])[None, :],
        qg=jnp.tile(p['fox_q_norm'][l], FOX_HEADS)[None, :],
        kg=jnp.tile(p['fox_k_norm'][l], FOX_HEADS)[None, :],
        av=jnp.concatenate([jnp.zeros((FOX_HEADS,), F32), a_neg, pad])[None, :],
        dv=jnp.repeat(p['ssd_D'][l], HEAD_DIM)[None, :],
        xa_qg=jnp.tile(p['xa_q_norm'][l], MEM_HEADS)[None, :],
    )


def _token_mix(l, x, p, lp, nseq, seq, conv_state, ssd_conv_state, ssd_state, paged, finish):
    n = nseq * seq
    scb, cin, qa, kn, ka, v, va, z, xbc, fdt, *lf_t = _inproj(
        x, p['mix_norm'][l][None, :], p['w_in_t'], lp['qg'], lp['kg'], lp['fb'], p['pm'], p['pc'], l, seq)

    if paged is None:
        attn = _fox_prompt(qa, ka, va, nseq, seq)
        k_out = jnp.transpose(kn, (0, 3, 1, 2))
        v_out = jnp.transpose(v, (0, 3, 1, 2))
        logf = jnp.transpose(lf_t[0], (0, 2, 1))
    else:
        logf = fdt[:, F_LANE0:F_LANE0 + FOX_HEADS].reshape(nseq, seq, FOX_HEADS)
        k_out = kn.reshape(nseq, seq, FOX_HEADS, HEAD_DIM)
        v_out = v.reshape(nseq, seq, FOX_HEADS, HEAD_DIM)
        page_table, cache_kt, cache_vt, cache_lf_t = paged
        eye = jnp.eye(FOX_HEADS, dtype=BF16)
        q4 = jnp.transpose(qa.reshape(nseq, seq, FOX_HEADS, LANES)[..., :HEAD_DIM], (0, 2, 1, 3))
        qbd = (q4[:, :, :, None, :] * eye[None, :, None, :, None]).reshape(nseq, FOX_HEADS * seq, ATTN_DIM)
        attn = _fox_paged(l, page_table, qbd, kn.reshape(nseq, seq, ATTN_DIM), v.reshape(nseq, seq, ATTN_DIM),
                          jnp.transpose(logf, (0, 2, 1)), cache_kt, cache_vt, cache_lf_t,
                          nseq, seq).reshape(n, ATTN_DIM)

    yssd, h_end = _ssd(xbc, _pad_state(ssd_conv_state, SSD_CONV), fdt, z, ssd_state, p['ssd_conv_w'][l],
                       p['ssd_conv_b'][l][None, :], lp['av'], lp['dv'], nseq, seq)
    x = finish(x, scb, cin, _pad_state(conv_state, SC_WIDTH), attn, yssd)

    cin3 = cin.reshape(nseq, seq, CONV_DIM)
    xbc3 = xbc.reshape(nseq, seq, XBC_DIM)
    new_conv = jnp.concatenate([conv_state, cin3], axis=1)[:, -(SC_WIDTH - 1):]
    new_ssd_conv = jnp.concatenate([ssd_conv_state, xbc3], axis=1)[:, -(SSD_CONV - 1):]
    state = (k_out, v_out, logf, new_conv, new_ssd_conv, h_end)
    return x, state


def kernel(x_prompt, x_sample, cache_fox_k, cache_fox_v, cache_fox_logf, cache_mem_k, cache_mem_v,
           state_conv, state_ssd_conv, state_ssd, page_table, mem_prompt,
           ffn1_norm, ffn1_wg, ffn1_wu, ffn1_wd, mix_norm, w_in, sc_conv_w, fox_q_norm, fox_k_norm,
           fox_f_bias, ssd_conv_w, ssd_conv_b, ssd_dt_bias, ssd_A_log, ssd_D, mix_out_norm, w_out,
           xa_norm, mem_norm, xa_wq, xa_wk, xa_wv, xa_q_norm, xa_k_norm, xa_wo,
           ffn2_norm, ffn2_wg, ffn2_wu, ffn2_wd, final_norm):
    b_p, s_p, _ = x_prompt.shape
    b_s, s_s, _ = x_sample.shape
    n_phys = cache_fox_k.shape[1]
    pm, pc = _bias_placement()
    p = dict(mix_norm=mix_norm, w_in_t=_prep_w_in(w_in), sc_conv_w=sc_conv_w, fox_q_norm=fox_q_norm,
             fox_k_norm=fox_k_norm, fox_f_bias=fox_f_bias, ssd_conv_w=ssd_conv_w, ssd_conv_b=ssd_conv_b,
             ssd_dt_bias=ssd_dt_bias, ssd_A_log=ssd_A_log, ssd_D=ssd_D, mix_out_norm=mix_out_norm,
             xa_q_norm=xa_q_norm, w_out_bf=w_out.astype(BF16), pm=pm, pc=pc)
    wg1, wu1, wd1 = ffn1_wg.astype(BF16), ffn1_wu.astype(BF16), ffn1_wd.astype(BF16)
    wg2, wu2, wd2 = ffn2_wg.astype(BF16), ffn2_wu.astype(BF16), ffn2_wd.astype(BF16)
    wq, wo = xa_wq.astype(BF16), xa_wo.astype(BF16)
    fg = final_norm[None, :]

    km_p, vm_p = _memkv(mem_prompt, mem_norm[:, None, :], xa_wk.astype(BF16), xa_wv.astype(BF16),
                        jnp.tile(xa_k_norm, (1, MEM_HEADS))[:, None, :])
    km_s, vm_s = cache_mem_k, cache_mem_v
    cache_kt = jnp.transpose(cache_fox_k, (0, 1, 3, 4, 2)).reshape(DEPTH, n_phys, ATTN_DIM, PAGE_SIZE)
    cache_vt = jnp.transpose(cache_fox_v, (0, 1, 3, 4, 2)).reshape(DEPTH, n_phys, ATTN_DIM, PAGE_SIZE)
    cache_lf_t = jnp.transpose(cache_fox_logf, (0, 1, 3, 2))

    zero_conv = jnp.zeros((b_p, SC_WIDTH - 1, CONV_DIM), F32)
    zero_ssd_conv = jnp.zeros((b_p, SSD_CONV - 1, XBC_DIM), F32)
    zero_ssd = jnp.zeros((b_p, SSD_HEADS, HEAD_DIM, SSD_STATE), F32)

    xp = x_prompt.reshape(b_p * s_p, D_MODEL)
    xs = x_sample.reshape(b_s * s_s, D_MODEL)
    st_p, st_s = [], []
    for l in range(DEPTH):
        lp = _layer_params(l, p)
        last = l == DEPTH - 1
        groups = []
        for x, nseq, seq, km, vm, cs, scs, ss, paged in (
                (xp, b_p, s_p, km_p, vm_p, zero_conv, zero_ssd_conv, zero_ssd, None),
                (xs, b_s, s_s, km_s, vm_s, state_conv[l], state_ssd_conv[l], state_ssd[l],
                 (page_table, cache_kt, cache_vt, cache_lf_t))):
            def finish(x, scb, cin, conv_pad, attn, yssd, nseq=nseq, seq=seq, km=km, vm=vm):
                mix = (p['sc_conv_w'][l], p['mix_out_norm'][l][None, :], p['w_out_bf'])
                xa = (xa_norm[l][None, :], wq, lp['xa_qg'], wo)
                f2 = (ffn2_norm[l][None, :], wg2, wu2, wd2, fg)
                if seq % ROW_TILE == 0:
                    x = _merge_xattn(x, scb, cin, conv_pad, attn, yssd, *mix, km, vm, *xa, l, nseq, seq)
                else:
                    x = _merge(x, scb, cin, conv_pad, attn, yssd, *mix, l, nseq, seq)
                    x = _xattn(x, km, vm, l, *xa, nseq, seq)
                return _ffn(x, *f2, l, last)

            x = _ffn(x, ffn1_norm[l][None, :], wg1, wu1, wd1, fg, l, False)
            x, st = _token_mix(l, x, p, lp, nseq, seq, cs, scs, ss, paged, finish)
            groups.append((x, st))
        (xp, sp), (xs, ss_) = groups
        st_p.append(sp)
        st_s.append(ss_)

    def stack(states, idx):
        return jnp.stack([s[idx] for s in states])

    return (xp.reshape(b_p, s_p, D_MODEL), xs.reshape(b_s, s_s, D_MODEL),
            stack(st_p, 0), stack(st_p, 1), stack(st_p, 2),
            km_p.reshape(DEPTH, b_p, MEM_LEN, MEM_HEADS, MEM_HEAD_DIM),
            vm_p.reshape(DEPTH, b_p, MEM_LEN, MEM_HEADS, MEM_HEAD_DIM),
            stack(st_p, 3), stack(st_p, 4), stack(st_p, 5),
            stack(st_s, 0), stack(st_s, 1), stack(st_s, 2), stack(st_s, 3), stack(st_s, 4), stack(st_s, 5))
```

```python
import functools

import numpy as np
import jax
import jax.numpy as jnp
from jax import lax
from jax.experimental import pallas as pl
from jax.experimental.pallas import tpu as pltpu

F32 = jnp.float32
BF16 = jnp.bfloat16

D_MODEL = 1024
DEPTH = 4
PAGE_SIZE = 128
HEAD_DIM = 64
CONV_DIM = 256
ATTN_DIM = 512
SSD_DIM = 256
FOX_HEADS = 8
SC_WIDTH = 3
SSD_HEADS = 4
SSD_STATE = 64
SSD_CONV = 4
SSD_CHUNK = 128
XBC_DIM = 512
FFN_DIM = 2816
MEM_LEN = 256
MEM_HEADS = 4
MEM_HEAD_DIM = 128
MEM_DIM = 512
RMS_EPS = 1e-6
LOG2E = 1.4426950408889634

CUMSUM_BLOCK = 256
LANES = 128
SUBLANES = 8
VMEM_LIMIT = 56 * 1024 * 1024
ROW_TILE = 512
PAGES_PER_STEP = 16
XATTN_SEQS_PER_STEP = 8
FOX_HEADS_PER_STEP = 4
SSD_CHUNKS_PER_STEP = 4
SSD_SEQS_PER_STEP = 8
PROJ_QKV = 3 * CONV_DIM + 3 * ATTN_DIM
F_LANE0 = 0
DT_LANE0 = FOX_HEADS
PAD_DIM = FOX_HEADS * LANES
BIAS_LANE0 = HEAD_DIM
N_SPLIT = 3


def _params(sem):
    return pltpu.CompilerParams(dimension_semantics=sem, vmem_limit_bytes=VMEM_LIMIT)


def _resident(shape):
    nd = len(shape)
    return pl.BlockSpec(shape, lambda *_: (0,) * nd, pipeline_mode=pl.Buffered(1))


def _resident_layer(shape, layer):
    nd = len(shape)
    return pl.BlockSpec((None,) + tuple(shape), lambda *_: (layer,) + (0,) * nd, pipeline_mode=pl.Buffered(1))


def _rms(x, g):
    ms = jnp.mean(x * x, axis=-1, keepdims=True)
    return x * lax.rsqrt(ms + RMS_EPS) * g


def _head_rms(x, g, head_dim):
    width = x.shape[-1]
    pieces = []
    for c in range(width // LANES):
        blk = x[:, c * LANES:(c + 1) * LANES]
        sq = blk * blk
        s_all = jnp.sum(sq, axis=-1, keepdims=True)
        if head_dim == LANES:
            ms = s_all * (1.0 / LANES)
        else:
            lo = lax.broadcasted_iota(jnp.int32, sq.shape, 1) < head_dim
            s_lo = jnp.sum(jnp.where(lo, sq, 0.0), axis=-1, keepdims=True)
            ms = jnp.where(lo, s_lo, s_all - s_lo) * (1.0 / head_dim)
        pieces.append(blk * lax.rsqrt(ms + RMS_EPS))
    return jnp.concatenate(pieces, axis=-1) * g


def _silu(x):
    return x * jax.nn.sigmoid(x)


def _softplus(x):
    return jnp.maximum(x, 0.0) + jnp.log1p(jnp.exp(-jnp.abs(x)))


def _split3(x):
    hi = x.astype(BF16)
    r1 = x - hi.astype(F32)
    mid = r1.astype(BF16)
    lo = (r1 - mid.astype(F32)).astype(BF16)
    return hi, mid, lo


def _dot_exact_rhs(x, m_bf16):
    hi, mid, lo = _split3(x)
    out = jnp.dot(hi, m_bf16, preferred_element_type=F32)
    out = out + jnp.dot(mid, m_bf16, preferred_element_type=F32)
    return out + jnp.dot(lo, m_bf16, preferred_element_type=F32)


def _dot_exact_lhs(m_bf16, x):
    hi, mid, lo = _split3(x)
    out = jnp.dot(m_bf16, hi, preferred_element_type=F32)
    out = out + jnp.dot(m_bf16, mid, preferred_element_type=F32)
    return out + jnp.dot(m_bf16, lo, preferred_element_type=F32)


def _lower_tri(n, seq=None):
    r = lax.broadcasted_iota(jnp.int32, (n, n), 0)
    c = lax.broadcasted_iota(jnp.int32, (n, n), 1)
    keep = c <= r
    if seq is not None and seq < n:
        shift = jnp.int32(seq.bit_length() - 1)
        keep = keep & (lax.shift_right_logical(r, shift) == lax.shift_right_logical(c, shift))
    return keep.astype(BF16)


def _upper_tri(n):
    r = lax.broadcasted_iota(jnp.int32, (n, n), 0)
    c = lax.broadcasted_iota(jnp.int32, (n, n), 1)
    return (r <= c).astype(BF16)


def _nt(a, b):
    return lax.dot_general(a, b, (((1,), (1,)), ((), ())), preferred_element_type=F32)


def _tn(a, b):
    return lax.dot_general(a, b, (((0,), (0,)), ((), ())), preferred_element_type=F32)


def _row_tile(x, reps):
    return x if reps == 1 else jnp.concatenate([x] * reps, axis=0)


def _lane_tile(x, reps):
    return x if reps == 1 else jnp.concatenate([x] * reps, axis=-1)


def _ffn_body(x, g_ref, wg_ref, wu_ref, wd_ref, fg_ref, final):
    h = _rms(x, g_ref[...]).astype(BF16)
    a = jnp.dot(h, wg_ref[...], preferred_element_type=F32)
    b = jnp.dot(h, wu_ref[...], preferred_element_type=F32)
    t = (_silu(a) * b).astype(BF16)
    out = x + 0.5 * jnp.dot(t, wd_ref[...], preferred_element_type=F32)
    return _rms(out, fg_ref[...]) if final else out


def _ffn_kernel(x_ref, g_ref, wg_ref, wu_ref, wd_ref, fg_ref, o_ref, *, final):
    o_ref[...] = _ffn_body(x_ref[...], g_ref, wg_ref, wu_ref, wd_ref, fg_ref, final)


def _ffn(x, g, wg, wu, wd, fg, layer, final):
    n = x.shape[0]
    tm = min(ROW_TILE, n)
    return pl.pallas_call(
        functools.partial(_ffn_kernel, final=final),
        out_shape=jax.ShapeDtypeStruct((n, D_MODEL), F32),
        grid=(n // tm,),
        in_specs=[
            pl.BlockSpec((tm, D_MODEL), lambda i: (i, 0)),
            _resident((1, D_MODEL)),
            _resident_layer((D_MODEL, FFN_DIM), layer),
            _resident_layer((D_MODEL, FFN_DIM), layer),
            _resident_layer((FFN_DIM, D_MODEL), layer),
            _resident((1, D_MODEL)),
        ],
        out_specs=pl.BlockSpec((tm, D_MODEL), lambda i: (i, 0)),
        compiler_params=_params(("arbitrary",)),
        name="ffn",
    )(x, g, wg, wu, wd, fg)


def _expand_heads(x, fill):
    lo = lax.broadcasted_iota(jnp.int32, (x.shape[0], LANES), 1) < HEAD_DIM
    blocks = []
    for c in range(ATTN_DIM // LANES):
        blk = x[:, c * LANES:(c + 1) * LANES]
        for half, src in enumerate((blk, pltpu.roll(blk, HEAD_DIM, axis=1))):
            h = 2 * c + half
            blocks.append(jnp.where(lo, src, fill[:, h * LANES:(h + 1) * LANES]))
    return jnp.concatenate(blocks, axis=-1).astype(BF16)


def _inproj_kernel(x_ref, g_ref, w_ref, wzx_ref, wfd_ref, qg_ref, kg_ref, fb_ref, pm_ref, pc_ref,
                   scb_ref, cin_ref, qa_ref, kn_ref, ka_ref, v_ref, va_ref,
                   z_ref, xbc_ref, fdt_ref, *rest, tm, seq, dim_major):
    carry_ref = rest[-1]
    h = _rms(x_ref[...], g_ref[...]).astype(BF16)

    def store_kv(ref, val):
        ref[...] = val.T.reshape(FOX_HEADS, HEAD_DIM, tm) if dim_major else val

    def proj(a, b, ref=w_ref):
        return _nt(h, ref[a:b, :])

    o = 0
    scb_ref[...] = proj(o, o + CONV_DIM)
    o += CONV_DIM
    cin_ref[...] = proj(o, o + CONV_DIM) * proj(o + CONV_DIM, o + 2 * CONV_DIM)
    o += 2 * CONV_DIM
    q = _head_rms(proj(o, o + ATTN_DIM), qg_ref[...], HEAD_DIM) * (HEAD_DIM ** -0.5 * LOG2E)
    o += ATTN_DIM
    k = _head_rms(proj(o, o + ATTN_DIM), kg_ref[...], HEAD_DIM)
    store_kv(kn_ref, k)
    o += ATTN_DIM
    v = proj(o, o + ATTN_DIM)
    store_kv(v_ref, v)
    z_ref[...] = proj(0, SSD_DIM, wzx_ref)
    xbc_ref[...] = proj(SSD_DIM, SSD_DIM + XBC_DIM, wzx_ref)
    u = proj(0, LANES, wfd_ref) + fb_ref[...]
    lane = lax.broadcasted_iota(jnp.int32, u.shape, 1)
    fdt = jnp.where(lane < DT_LANE0, -_softplus(-u), _softplus(u))
    fdt_ref[...] = fdt
    if dim_major:
        rest[0][...] = fdt.T[0:FOX_HEADS, :]

    @pl.when(pl.program_id(0) % max(seq // tm, 1) == 0)
    def _():
        carry_ref[...] = jnp.zeros_like(carry_ref)

    lf = jnp.where(lane < FOX_HEADS, fdt, 0.0) * LOG2E
    blk = min(tm, CUMSUM_BLOCK)
    tri = _lower_tri(blk, seq)
    carry = carry_ref[0:1, :]
    c_parts = []
    for r in range(tm // blk):
        c_blk = _dot_exact_lhs(tri, lf[r * blk:(r + 1) * blk, :])
        if seq >= blk and (r == 0 or (r * blk) % seq != 0):
            c_blk = c_blk + carry
        carry = c_blk[blk - 1:blk, :]
        c_parts.append(c_blk)
    c = jnp.concatenate(c_parts, axis=0)
    carry_ref[...] = jnp.broadcast_to(carry, carry_ref.shape)
    hi, mid, lo = _split3(c)
    packed = (hi.astype(F32) + pltpu.roll(mid.astype(F32), FOX_HEADS, axis=1)
              + pltpu.roll(lo.astype(F32), 2 * FOX_HEADS, axis=1)).astype(BF16)
    both = jnp.dot(packed, pm_ref[...], preferred_element_type=F32)
    lane_blk = _lane_tile(lax.broadcasted_iota(jnp.int32, (tm, LANES), 1), FOX_HEADS)
    q_side = lane_blk < BIAS_LANE0 + N_SPLIT
    qa_ref[...] = _expand_heads(q, jnp.where(q_side, both, pc_ref[:, 0:PAD_DIM]))
    ka_ref[...] = _expand_heads(k, jnp.where(q_side, pc_ref[:, PAD_DIM:2 * PAD_DIM], both))
    va_ref[...] = _expand_heads(v, pc_ref[:, 2 * PAD_DIM:3 * PAD_DIM])


def _bias_placement():
    pm = np.zeros((LANES, PAD_DIM), np.float32)
    pc = np.zeros((1, 3 * PAD_DIM), np.float32)
    for h in range(FOX_HEADS):
        for part in range(N_SPLIT):
            src = part * FOX_HEADS + h
            pm[src, h * LANES + BIAS_LANE0 + part] = 1.0
            pm[src, h * LANES + BIAS_LANE0 + N_SPLIT + part] = -1.0
            pc[0, h * LANES + BIAS_LANE0 + N_SPLIT + part] = 1.0
            pc[0, PAD_DIM + h * LANES + BIAS_LANE0 + part] = 1.0
        pc[0, 2 * PAD_DIM + h * LANES + HEAD_DIM] = 1.0
    return jnp.asarray(pm, BF16), jnp.asarray(pc, F32)


def _inproj(x, g, w, qg, kg, fb, pm, pc, layer, seq):
    w_t, w_zx, w_fd = w
    n = x.shape[0]
    tm = min(ROW_TILE, n)
    widths = [(CONV_DIM, F32), (CONV_DIM, F32), (PAD_DIM, BF16), (ATTN_DIM, F32), (PAD_DIM, BF16),
              (ATTN_DIM, F32), (PAD_DIM, BF16), (SSD_DIM, F32), (XBC_DIM, F32), (LANES, F32)]
    out_shape = [jax.ShapeDtypeStruct((n, w_), dt) for w_, dt in widths]
    out_specs = [pl.BlockSpec((tm, w_), lambda i: (i, 0)) for w_, _ in widths]
    dim_major = seq >= tm
    if dim_major:
        tps = seq // tm
        kv_shape = jax.ShapeDtypeStruct((n // seq, FOX_HEADS, HEAD_DIM, seq), F32)
        kv_spec = pl.BlockSpec((None, FOX_HEADS, HEAD_DIM, tm), lambda i: (i // tps, 0, 0, i % tps))
        out_shape[3], out_shape[5] = kv_shape, kv_shape
        out_specs[3], out_specs[5] = kv_spec, kv_spec
        out_shape.append(jax.ShapeDtypeStruct((n // seq, FOX_HEADS, seq), F32))
        out_specs.append(pl.BlockSpec((None, FOX_HEADS, tm), lambda i: (i // tps, 0, i % tps)))
    return pl.pallas_call(
        functools.partial(_inproj_kernel, tm=tm, seq=seq, dim_major=dim_major),
        out_shape=out_shape,
        grid=(n // tm,),
        in_specs=[
            pl.BlockSpec((tm, D_MODEL), lambda i: (i, 0)),
            _resident((1, D_MODEL)),
            _resident_layer((PROJ_QKV, D_MODEL), layer),
            _resident_layer((SSD_DIM + XBC_DIM, D_MODEL), layer),
            _resident_layer((LANES, D_MODEL), layer),
            _resident((1, ATTN_DIM)),
            _resident((1, ATTN_DIM)),
            _resident((1, LANES)),
            _resident((LANES, PAD_DIM)),
            _resident((1, 3 * PAD_DIM)),
        ],
        out_specs=out_specs,
        scratch_shapes=[pltpu.VMEM((SUBLANES, LANES), F32)],
        compiler_params=_params(("arbitrary",)),
        name="inproj",
    )(x, g, w_t, w_zx, w_fd, qg, kg, fb, pm, pc)


def _fox_kernel(q_ref, k_ref, v_ref, o_ref, m_ref, acc_ref, *, tq):
    i = pl.program_id(2)
    m_ref[...] = jnp.full(m_ref.shape, -jnp.inf, F32)
    acc_ref[...] = jnp.zeros_like(acc_ref)

    def tile(start, width, masked):
        rows = pl.ds(pl.multiple_of(start, width), width)
        for hh in range(FOX_HEADS_PER_STEP):
            sl = slice(hh * LANES, (hh + 1) * LANES)
            s = _nt(q_ref[:, sl], k_ref[rows, sl])
            if masked:
                row = lax.broadcasted_iota(jnp.int32, s.shape, 0)
                col = lax.broadcasted_iota(jnp.int32, s.shape, 1)
                s = jnp.where(col <= row, s, -jnp.inf)
            m_old = m_ref[hh]
            m_new = jnp.maximum(m_old, jnp.max(s, axis=-1, keepdims=True))
            p = jnp.exp2(s - _lane_tile(m_new, width // LANES))
            acc_ref[hh] = jnp.exp2(m_old - m_new) * acc_ref[hh] + jnp.dot(
                p.astype(BF16), v_ref[rows, sl], preferred_element_type=F32)
            m_ref[hh] = m_new

    def body(j, carry):
        tile(j * (2 * tq), 2 * tq, False)
        return carry

    lax.fori_loop(0, lax.shift_right_logical(i, 1), body, 0)

    @pl.when(lax.bitwise_and(i, 1) == 1)
    def _():
        tile((i - 1) * tq, tq, False)

    tile(i * tq, tq, True)
    outs = []
    for hh in range(FOX_HEADS_PER_STEP):
        a = acc_ref[hh]
        outs.append(a[:, 0:HEAD_DIM] / a[:, HEAD_DIM:HEAD_DIM + 1])
    o_ref[...] = jnp.concatenate(outs, axis=-1)


def _fox_prompt(qa, ka, va, nseq, seq):
    tq = min(ROW_TILE, seq)
    nq = seq // tq
    nh = FOX_HEADS_PER_STEP
    return pl.pallas_call(
        functools.partial(_fox_kernel, tq=tq),
        out_shape=jax.ShapeDtypeStruct((nseq * seq, ATTN_DIM), F32),
        grid=(nseq, FOX_HEADS // nh, nq),
        in_specs=[
            pl.BlockSpec((tq, nh * LANES), lambda b, h, i: (b * nq + i, h)),
            pl.BlockSpec((seq, nh * LANES), lambda b, h, i: (b, h)),
            pl.BlockSpec((seq, nh * LANES), lambda b, h, i: (b, h)),
        ],
        out_specs=pl.BlockSpec((tq, nh * HEAD_DIM), lambda b, h, i: (b * nq + i, h)),
        scratch_shapes=[pltpu.VMEM((nh, tq, LANES), F32), pltpu.VMEM((nh, tq, LANES), F32)],
        compiler_params=_params(("arbitrary",) * 3),
        name="fox_prompt",
    )(qa, ka, va)


def _fox_paged_kernel(pt_ref, q_ref, kn_ref, vn_ref, lfn_ref, *rest, tokens):
    npg = PAGES_PER_STEP
    k_refs = rest[:npg]
    v_refs = rest[npg:2 * npg]
    lf_refs = rest[2 * npg:3 * npg]
    o_ref, m_ref, l_ref, acc_ref, carry_ref, c_ref = rest[3 * npg:]
    g = pl.program_id(1)
    rows = FOX_HEADS * tokens

    @pl.when(g == 0)
    def _():
        m_ref[...] = jnp.full(m_ref.shape, -jnp.inf, F32)
        l_ref[...] = jnp.zeros_like(l_ref)
        acc_ref[...] = jnp.zeros_like(acc_ref)
        carry_ref[...] = jnp.zeros_like(carry_ref)

    q = q_ref[...]
    tri = _upper_tri(PAGE_SIZE)

    def update(k_list, v_list, lf_list):
        n = len(k_list)
        nr = n * FOX_HEADS
        parts = _split3(jnp.concatenate(lf_list, axis=0) * LOG2E)

        def times(m_bf16):
            return sum(jnp.dot(x, m_bf16, preferred_element_type=F32) for x in parts)

        c = times(tri) + _row_tile(carry_ref[...], n)
        if n > 1:
            tot = times(jnp.ones((PAGE_SIZE, PAGE_SIZE), BF16))
            r = lax.broadcasted_iota(jnp.int32, (nr, nr), 0)
            col = lax.broadcasted_iota(jnp.int32, (nr, nr), 1)
            same_head = lax.bitwise_and(r, FOX_HEADS - 1) == lax.bitwise_and(col, FOX_HEADS - 1)
            earlier = jnp.where(col < r, jnp.where(same_head, 1.0, 0.0), 0.0).astype(BF16)
            off = _dot_exact_lhs(earlier, tot)
            c = c + off
            carry_ref[...] = carry_ref[...] + off[nr - FOX_HEADS:nr, :] + tot[nr - FOX_HEADS:nr, :]
        c_ref[0:nr, :] = c
        s_list = []
        for idx in range(n):
            c_rows = jnp.concatenate(
                [jnp.broadcast_to(c_ref[idx * FOX_HEADS + h:idx * FOX_HEADS + h + 1, :], (tokens, PAGE_SIZE))
                 for h in range(FOX_HEADS)], axis=0)
            s_list.append(jnp.dot(q, k_list[idx].astype(BF16), preferred_element_type=F32) - c_rows)
        s = jnp.concatenate(s_list, axis=-1)
        m_old = m_ref[...]
        m_new = jnp.maximum(m_old, jnp.max(s, axis=-1, keepdims=True))
        alpha = jnp.exp2(m_old - m_new)
        p = jnp.exp2(s - _lane_tile(m_new, n))
        l_ref[...] = alpha * l_ref[...] + jnp.sum(p, axis=-1, keepdims=True)
        pv = _nt(p[:, 0:PAGE_SIZE].astype(BF16), v_list[0].astype(BF16))
        for idx in range(1, n):
            pv = pv + _nt(p[:, idx * PAGE_SIZE:(idx + 1) * PAGE_SIZE].astype(BF16), v_list[idx].astype(BF16))
        acc_ref[...] = _lane_tile(alpha, ATTN_DIM // LANES) * acc_ref[...] + pv
        m_ref[...] = m_new

    update([r[...] for r in k_refs], [r[...] for r in v_refs], [r[...] for r in lf_refs])

    @pl.when(g == pl.num_programs(1) - 1)
    def _():
        c_new = _dot_exact_rhs(lfn_ref[...] * LOG2E, _upper_tri(tokens)) + carry_ref[:, 0:tokens]
        c_ref[0:FOX_HEADS, 0:tokens] = c_new
        c_rows = jnp.concatenate(
            [jnp.broadcast_to(c_ref[h:h + 1, 0:tokens], (tokens, tokens)) for h in range(FOX_HEADS)], axis=0)
        s = _nt(q, kn_ref[...].astype(BF16)) - c_rows
        t_idx = lax.bitwise_and(lax.broadcasted_iota(jnp.int32, s.shape, 0), tokens - 1)
        s = jnp.where(lax.broadcasted_iota(jnp.int32, s.shape, 1) <= t_idx, s, -jnp.inf)
        m_old = m_ref[...]
        m_new = jnp.maximum(m_old, jnp.max(s, axis=-1, keepdims=True))
        alpha = jnp.exp2(m_old - m_new)
        p = jnp.exp2(s - m_new[:, 0:tokens])
        l_new = alpha * l_ref[...] + jnp.sum(p, axis=-1, keepdims=True)
        acc = _lane_tile(alpha, ATTN_DIM // LANES) * acc_ref[...] + jnp.dot(
            p.astype(BF16), vn_ref[...].astype(BF16), preferred_element_type=F32)
        full = acc / _lane_tile(l_new, ATTN_DIM // LANES)
        lane_head = lax.shift_right_logical(
            lax.broadcasted_iota(jnp.int32, (tokens, ATTN_DIM), 1), jnp.int32(HEAD_DIM.bit_length() - 1))
        out = jnp.zeros((tokens, ATTN_DIM), F32)
        for h in range(FOX_HEADS):
            out = out + jnp.where(lane_head == h, full[h * tokens:(h + 1) * tokens, :], 0.0)
        o_ref[...] = out


def _fox_paged(layer, page_table, qbd, kn_t, vn_t, lfn_t, cache_kt, cache_vt, cache_lf_t, nseq, tokens):
    npg = PAGES_PER_STEP
    n_groups = page_table.shape[1] // npg
    rows = FOX_HEADS * tokens

    def page_spec(width2, n):
        return pl.BlockSpec((None, None, width2, PAGE_SIZE),
                            lambda b, g, pt: (layer, pt[b, g * npg + n], 0, 0))

    in_specs = [
        pl.BlockSpec((None, rows, ATTN_DIM), lambda b, g, pt: (b, 0, 0)),
        pl.BlockSpec((None, tokens, ATTN_DIM), lambda b, g, pt: (b, 0, 0)),
        pl.BlockSpec((None, tokens, ATTN_DIM), lambda b, g, pt: (b, 0, 0)),
        pl.BlockSpec((None, FOX_HEADS, tokens), lambda b, g, pt: (b, 0, 0)),
    ]
    in_specs += [page_spec(ATTN_DIM, n) for n in range(npg)]
    in_specs += [page_spec(ATTN_DIM, n) for n in range(npg)]
    in_specs += [page_spec(FOX_HEADS, n) for n in range(npg)]
    grid_spec = pltpu.PrefetchScalarGridSpec(
        num_scalar_prefetch=1,
        grid=(nseq, n_groups),
        in_specs=in_specs,
        out_specs=pl.BlockSpec((None, tokens, ATTN_DIM), lambda b, g, pt: (b, 0, 0)),
        scratch_shapes=[pltpu.VMEM((rows, LANES), F32), pltpu.VMEM((rows, LANES), F32),
                        pltpu.VMEM((rows, ATTN_DIM), F32), pltpu.VMEM((FOX_HEADS, LANES), F32),
                        pltpu.VMEM((npg * FOX_HEADS, LANES), F32)],
    )
    return pl.pallas_call(
        functools.partial(_fox_paged_kernel, tokens=tokens),
        out_shape=jax.ShapeDtypeStruct((nseq, tokens, ATTN_DIM), F32),
        grid_spec=grid_spec,
        compiler_params=_params(("arbitrary", "arbitrary")),
        name="fox_paged",
    )(page_table, qbd, kn_t, vn_t, lfn_t,
      *([cache_kt] * npg), *([cache_vt] * npg), *([cache_lf_t] * npg))


def _ssd_kernel(xbc_ref, halo_ref, st_ref, fdt_ref, z_ref, h0_ref,
                cw_ref, cb_ref, av_ref, dv_ref,
                y_ref, hout_ref, buf_ref, h_ref, *, lc, rows):
    j = pl.program_id(1)
    bs = st_ref.shape[0]

    @pl.when(j == 0)
    def _():
        h_ref[...] = h0_ref[...]
        buf_ref[:, 0:SUBLANES, :] = st_ref[...]

    @pl.when(j > 0)
    def _():
        buf_ref[0, 0:SUBLANES, :] = halo_ref[...]

    buf_ref[:, SUBLANES:SUBLANES + rows, :] = xbc_ref[...].reshape(bs, rows, XBC_DIM)
    base = SUBLANES - (SSD_CONV - 1)
    u_all = buf_ref[:, base:base + rows, :] * cw_ref[0:1, :]
    for k in range(1, SSD_CONV):
        u_all = u_all + buf_ref[:, base + k:base + k + rows, :] * cw_ref[k:k + 1, :]
    u_all = _silu(u_all + cb_ref[...]).reshape(bs * rows, XBC_DIM)

    tri_lo = _lower_tri(lc)
    tri_up = _upper_tri(lc)
    pick = (lax.broadcasted_iota(jnp.int32, (SUBLANES, LANES), 1)
            == lax.broadcasted_iota(jnp.int32, (SUBLANES, LANES), 0) + DT_LANE0).astype(BF16)

    def rows_of(x):
        return sum(_nt(pick, part) for part in _split3(x))

    row = lax.broadcasted_iota(jnp.int32, (lc, lc), 0)
    col = lax.broadcasted_iota(jnp.int32, (lc, lc), 1)
    causal = col <= row
    dvec = dv_ref[...]

    for s, c in ((s, c) for s in range(bs) for c in range(rows // lc)):
        tok = slice(s * rows + c * lc, s * rows + (c + 1) * lc)
        u = u_all[tok, :]
        z = z_ref[tok, :]
        fdt = fdt_ref[tok, :]
        da = fdt * av_ref[...]
        acum = _dot_exact_lhs(tri_lo, da)
        dtt = rows_of(fdt)
        acum_t = _dot_exact_rhs(rows_of(da), tri_up)
        outs = []
        for h in range(SSD_HEADS):
            grp = h // (SSD_HEADS // 2)
            xh = u[:, h * HEAD_DIM:(h + 1) * HEAD_DIM]
            bh = u[:, SSD_DIM + grp * SSD_STATE:SSD_DIM + (grp + 1) * SSD_STATE]
            ch = u[:, SSD_DIM + 2 * SSD_STATE + grp * SSD_STATE:SSD_DIM + 2 * SSD_STATE + (grp + 1) * SSD_STATE]
            lane = DT_LANE0 + h
            a_col = acum[:, lane:lane + 1]
            dt_col = fdt[:, lane:lane + 1]
            a_row = acum_t[h:h + 1, :]
            dt_row = dtt[h:h + 1, :]
            a_last = acum[lc - 1:lc, lane:lane + 1]
            decay = jnp.exp(jnp.where(causal, a_col - a_row, -jnp.inf))
            xb = xh.astype(BF16)
            bb = bh.astype(BF16)
            cb16 = ch.astype(BF16)
            w = _nt(cb16, bb) * decay * dt_row
            h_prev = h_ref[s, h]
            y = jnp.dot(w.astype(BF16), xb, preferred_element_type=F32)
            y = y + _nt(cb16, h_prev.astype(BF16)) * jnp.exp(a_col)
            w_end = jnp.exp(a_last - a_col) * dt_col
            h_ref[s, h] = h_prev * jnp.exp(a_last) + _tn((xh * w_end).astype(BF16), bb)
            sl = slice(h * HEAD_DIM, (h + 1) * HEAD_DIM)
            outs.append((y + dvec[:, sl] * xh) * _silu(z[:, sl]))
        y_ref[tok, :] = jnp.concatenate(outs, axis=-1)

    @pl.when(j == pl.num_programs(1) - 1)
    def _():
        hout_ref[...] = h_ref[...]


def _ssd(xbc, st_pad, fdt, z, h0, cw, cb, av, dv, nseq, seq):
    lc = min(SSD_CHUNK, seq)
    rows = min(SSD_CHUNKS_PER_STEP * lc, seq)
    nc = seq // rows
    bs = 1 if nc > 1 else min(nseq, SSD_SEQS_PER_STEP)
    nrows = bs * rows
    hb = nrows // SUBLANES
    state_spec = pl.BlockSpec((bs, SSD_HEADS, HEAD_DIM, SSD_STATE), lambda b, j: (b, 0, 0, 0))
    return pl.pallas_call(
        functools.partial(_ssd_kernel, lc=lc, rows=rows),
        out_shape=[jax.ShapeDtypeStruct((nseq * seq, SSD_DIM), F32),
                   jax.ShapeDtypeStruct((nseq, SSD_HEADS, HEAD_DIM, SSD_STATE), F32)],
        grid=(nseq // bs, nc),
        in_specs=[
            pl.BlockSpec((nrows, XBC_DIM), lambda b, j: (b * nc + j, 0)),
            pl.BlockSpec((SUBLANES, XBC_DIM), lambda b, j: (jnp.maximum((b * nc + j) * hb - 1, 0), 0)),
            pl.BlockSpec((bs, SUBLANES, XBC_DIM), lambda b, j: (b, 0, 0)),
            pl.BlockSpec((nrows, LANES), lambda b, j: (b * nc + j, 0)),
            pl.BlockSpec((nrows, SSD_DIM), lambda b, j: (b * nc + j, 0)),
            state_spec,
            _resident((SSD_CONV, XBC_DIM)),
            _resident((1, XBC_DIM)),
            _resident((1, LANES)),
            _resident((1, SSD_DIM)),
        ],
        out_specs=[pl.BlockSpec((nrows, SSD_DIM), lambda b, j: (b * nc + j, 0)), state_spec],
        scratch_shapes=[pltpu.VMEM((bs, SUBLANES + rows, XBC_DIM), F32),
                        pltpu.VMEM((bs, SSD_HEADS, HEAD_DIM, SSD_STATE), F32)],
        compiler_params=_params(("arbitrary", "arbitrary")),
        name="ssd",
    )(xbc, xbc, st_pad, fdt, z, h0, cw, cb, av, dv)


def _merge_body(x_ref, scb_ref, cin_ref, halo_ref, st_ref, attn_ref, yssd_ref,
                cw_ref, g_ref, w_ref, buf_ref, tl):
    j = pl.program_id(1)
    bs = st_ref.shape[0]

    @pl.when(j == 0)
    def _():
        buf_ref[:, 0:SUBLANES, :] = st_ref[...]

    @pl.when(j > 0)
    def _():
        buf_ref[0, 0:SUBLANES, :] = halo_ref[...]

    buf_ref[:, SUBLANES:SUBLANES + tl, :] = cin_ref[...].reshape(bs, tl, CONV_DIM)
    base = SUBLANES - (SC_WIDTH - 1)
    y = buf_ref[:, base:base + tl, :] * cw_ref[0:1, :]
    for k in range(1, SC_WIDTH):
        y = y + buf_ref[:, base + k:base + k + tl, :] * cw_ref[k:k + 1, :]
    y = y.reshape(bs * tl, CONV_DIM)
    cat = jnp.concatenate([scb_ref[...] * y, attn_ref[...], yssd_ref[...]], axis=-1)
    cat = _head_rms(cat, g_ref[...], HEAD_DIM).astype(BF16)
    return x_ref[...] + jnp.dot(cat, w_ref[...], preferred_element_type=F32)


def _merge_kernel(x_ref, scb_ref, cin_ref, halo_ref, st_ref, attn_ref, yssd_ref,
                  cw_ref, g_ref, w_ref, o_ref, buf_ref, *, tl):
    o_ref[...] = _merge_body(x_ref, scb_ref, cin_ref, halo_ref, st_ref, attn_ref, yssd_ref,
                             cw_ref, g_ref, w_ref, buf_ref, tl)


def _merge(x, scb, cin, st_pad, attn, yssd, cw, g, w, layer, nseq, seq):
    tl = min(ROW_TILE, seq)
    nt = seq // tl
    bs = 1 if nt > 1 else min(nseq, ROW_TILE // tl)
    nrows = bs * tl
    hb = nrows // SUBLANES

    def rows(width):
        return pl.BlockSpec((nrows, width), lambda b, j: (b * nt + j, 0))

    return pl.pallas_call(
        functools.partial(_merge_kernel, tl=tl),
        out_shape=jax.ShapeDtypeStruct((nseq * seq, D_MODEL), F32),
        grid=(nseq // bs, nt),
        in_specs=[
            rows(D_MODEL), rows(CONV_DIM), rows(CONV_DIM),
            pl.BlockSpec((SUBLANES, CONV_DIM), lambda b, j: (jnp.maximum((b * nt + j) * hb - 1, 0), 0)),
            pl.BlockSpec((bs, SUBLANES, CONV_DIM), lambda b, j: (b, 0, 0)),
            rows(ATTN_DIM), rows(SSD_DIM),
            _resident((SC_WIDTH, CONV_DIM)),
            _resident((1, D_MODEL)),
            _resident_layer((D_MODEL, D_MODEL), layer),
        ],
        out_specs=rows(D_MODEL),
        scratch_shapes=[pltpu.VMEM((bs, SUBLANES + tl, CONV_DIM), F32)],
        compiler_params=_params(("arbitrary", "arbitrary")),
        name="merge",
    )(x, scb, cin, cin, st_pad, attn, yssd, cw, g, w)


def _memkv_kernel(mem_ref, g_ref, wk_ref, wv_ref, kg_ref, km_ref, vm_ref):
    m = _rms(mem_ref[...], g_ref[...]).astype(BF16)
    km_ref[...] = _head_rms(jnp.dot(m, wk_ref[...], preferred_element_type=F32), kg_ref[...], MEM_HEAD_DIM)
    vm_ref[...] = jnp.dot(m, wv_ref[...], preferred_element_type=F32)


def _memkv(mem, g, wk, wv, kg):
    nb = mem.shape[0]
    out = jax.ShapeDtypeStruct((DEPTH, nb, MEM_LEN, MEM_DIM), F32)
    return pl.pallas_call(
        _memkv_kernel,
        out_shape=[out, out],
        grid=(DEPTH, nb),
        in_specs=[
            pl.BlockSpec((None, MEM_LEN, D_MODEL), lambda l, b: (b, 0, 0)),
            pl.BlockSpec((None, 1, D_MODEL), lambda l, b: (l, 0, 0)),
            pl.BlockSpec((None, D_MODEL, MEM_DIM), lambda l, b: (l, 0, 0)),
            pl.BlockSpec((None, D_MODEL, MEM_DIM), lambda l, b: (l, 0, 0)),
            pl.BlockSpec((None, 1, MEM_DIM), lambda l, b: (l, 0, 0)),
        ],
        out_specs=[pl.BlockSpec((None, None, MEM_LEN, MEM_DIM), lambda l, b: (l, b, 0, 0))] * 2,
        compiler_params=_params(("arbitrary", "arbitrary")),
        name="mem_kv",
    )(mem, g, wk, wv, kg)


def _xattn_body(x, km_ref, vm_ref, g_ref, wq_ref, qg_ref, wo_ref, q_ref, att_ref, bs, tl):
    h = _rms(x, g_ref[...]).astype(BF16)
    q = _head_rms(jnp.dot(h, wq_ref[...], preferred_element_type=F32), qg_ref[...], MEM_HEAD_DIM)
    q = q * (MEM_HEAD_DIM ** -0.5)
    for hh in range(MEM_HEADS):
        q_ref[hh] = q[:, hh * MEM_HEAD_DIM:(hh + 1) * MEM_HEAD_DIM]

    def softmax(sc):
        p = jnp.exp(sc - jnp.max(sc, axis=-1, keepdims=True))
        return (p / jnp.sum(p, axis=-1, keepdims=True)).astype(BF16)

    for s in range(bs):
        rows = slice(s * tl, (s + 1) * tl)
        if len(km_ref.shape) == 4:
            km = km_ref[s].reshape(MEM_LEN * MEM_HEADS, MEM_HEAD_DIM).astype(BF16)
            vm = vm_ref[s].reshape(MEM_LEN * MEM_HEADS, MEM_HEAD_DIM).astype(BF16)
            qs = jnp.concatenate([q_ref[hh, rows, :] for hh in range(MEM_HEADS)], axis=0).astype(BF16)
            sc = _nt(qs, km)
            row_head = lax.shift_right_logical(lax.broadcasted_iota(jnp.int32, sc.shape, 0),
                                               jnp.int32(tl.bit_length() - 1))
            col_head = lax.bitwise_and(lax.broadcasted_iota(jnp.int32, sc.shape, 1), MEM_HEADS - 1)
            out = jnp.dot(softmax(jnp.where(row_head == col_head, sc, -jnp.inf)), vm,
                          preferred_element_type=F32)
            for hh in range(MEM_HEADS):
                att_ref[hh, rows, :] = out[hh * tl:(hh + 1) * tl, :]
        else:
            for hh in range(MEM_HEADS):
                sl = slice(hh * MEM_HEAD_DIM, (hh + 1) * MEM_HEAD_DIM)
                sc = _nt(q_ref[hh, rows, :].astype(BF16), km_ref[s, :, sl].astype(BF16))
                att_ref[hh, rows, :] = jnp.dot(softmax(sc), vm_ref[s, :, sl].astype(BF16),
                                               preferred_element_type=F32)

    att = jnp.concatenate([att_ref[hh] for hh in range(MEM_HEADS)], axis=-1).astype(BF16)
    return x + jnp.dot(att, wo_ref[...], preferred_element_type=F32)


def _xattn_kernel(x_ref, km_ref, vm_ref, g_ref, wq_ref, qg_ref, wo_ref, o_ref, q_ref, att_ref, *, bs, tl):
    o_ref[...] = _xattn_body(x_ref[...], km_ref, vm_ref, g_ref, wq_ref, qg_ref, wo_ref, q_ref, att_ref, bs, tl)


def _xattn(x, km, vm, layer, g, wq, qg, wo, nseq, seq):
    tl = min(ROW_TILE, seq)
    nt = seq // tl
    bs = 1 if nt > 1 else min(nseq, XATTN_SEQS_PER_STEP)
    rows = bs * tl
    mem_zeros = (0,) * (km.ndim - 2)
    mem_spec = pl.BlockSpec((None, bs) + tuple(km.shape[2:]), lambda b, j: (layer, b) + mem_zeros)
    row_spec = pl.BlockSpec((rows, D_MODEL), lambda b, j: (b * nt + j, 0))
    return pl.pallas_call(
        functools.partial(_xattn_kernel, bs=bs, tl=tl),
        out_shape=jax.ShapeDtypeStruct((nseq * seq, D_MODEL), F32),
        grid=(nseq // bs, nt),
        in_specs=[
            row_spec, mem_spec, mem_spec,
            _resident((1, D_MODEL)),
            _resident_layer((D_MODEL, MEM_DIM), layer),
            _resident((1, MEM_DIM)),
            _resident_layer((MEM_DIM, D_MODEL), layer),
        ],
        out_specs=row_spec,
        scratch_shapes=[pltpu.VMEM((MEM_HEADS, rows, MEM_HEAD_DIM), F32),
                        pltpu.VMEM((MEM_HEADS, rows, MEM_HEAD_DIM), F32)],
        compiler_params=_params(("arbitrary", "arbitrary")),
        name="xattn",
    )(x, km, vm, g, wq, qg, wo)


def _merge_xattn_kernel(x_ref, scb_ref, cin_ref, halo_ref, st_ref, attn_ref, yssd_ref, cw_ref, go_ref,
                        wout_ref, km_ref, vm_ref, gx_ref, wq_ref, qg_ref, wo_ref,
                        o_ref, buf_ref, q_ref, att_ref, *, tl):
    x = _merge_body(x_ref, scb_ref, cin_ref, halo_ref, st_ref, attn_ref, yssd_ref,
                    cw_ref, go_ref, wout_ref, buf_ref, tl)
    o_ref[...] = _xattn_body(x, km_ref, vm_ref, gx_ref, wq_ref, qg_ref, wo_ref, q_ref, att_ref, 1, tl)


def _merge_xattn(x, scb, cin, st_pad, attn, yssd, cw, g_out, w_out, km, vm, g_xa, wq, qg, wo,
                 layer, nseq, seq):
    tl = ROW_TILE
    nt = seq // tl
    hb = tl // SUBLANES

    def rows(width):
        return pl.BlockSpec((tl, width), lambda b, j: (b * nt + j, 0))

    mem_spec = pl.BlockSpec((None, 1, MEM_LEN, MEM_DIM), lambda b, j: (layer, b, 0, 0))
    return pl.pallas_call(
        functools.partial(_merge_xattn_kernel, tl=tl),
        out_shape=jax.ShapeDtypeStruct((nseq * seq, D_MODEL), F32),
        grid=(nseq, nt),
        in_specs=[
            rows(D_MODEL), rows(CONV_DIM), rows(CONV_DIM),
            pl.BlockSpec((SUBLANES, CONV_DIM), lambda b, j: (jnp.maximum((b * nt + j) * hb - 1, 0), 0)),
            pl.BlockSpec((1, SUBLANES, CONV_DIM), lambda b, j: (b, 0, 0)),
            rows(ATTN_DIM), rows(SSD_DIM),
            _resident((SC_WIDTH, CONV_DIM)),
            _resident((1, D_MODEL)),
            _resident_layer((D_MODEL, D_MODEL), layer),
            mem_spec, mem_spec,
            _resident((1, D_MODEL)),
            _resident_layer((D_MODEL, MEM_DIM), layer),
            _resident((1, MEM_DIM)),
            _resident_layer((MEM_DIM, D_MODEL), layer),
        ],
        out_specs=rows(D_MODEL),
        scratch_shapes=[pltpu.VMEM((1, SUBLANES + tl, CONV_DIM), F32),
                        pltpu.VMEM((MEM_HEADS, tl, MEM_HEAD_DIM), F32),
                        pltpu.VMEM((MEM_HEADS, tl, MEM_HEAD_DIM), F32)],
        compiler_params=_params(("arbitrary", "arbitrary")),
        name="merge_xattn",
    )(x, scb, cin, cin, st_pad, attn, yssd, cw, g_out, w_out, km, vm, g_xa, wq, qg, wo)


def _pad_state(state, width):
    nseq, _, c = state.shape
    return jnp.concatenate([jnp.zeros((nseq, SUBLANES - (width - 1), c), F32), state], axis=1)


def _prep_w_in(w_in):
    w_t = jnp.transpose(w_in, (0, 2, 1)).astype(BF16)
    n_z0 = PROJ_QKV + FOX_HEADS
    n_dt0 = n_z0 + SSD_DIM + XBC_DIM
    pad = jnp.zeros((DEPTH, LANES - FOX_HEADS - SSD_HEADS, D_MODEL), BF16)
    w_fd = jnp.concatenate([w_t[:, PROJ_QKV:n_z0], w_t[:, n_dt0:], pad], axis=1)
    return w_t, w_t[:, n_z0:n_dt0], w_fd


def _layer_params(l, p):
    pad = jnp.zeros((LANES - FOX_HEADS - SSD_HEADS,), F32)
    a_neg = -jnp.exp(p['ssd_A_log'][l])
    return dict(
        fb=jnp.concatenate([p['fox_f_bias'][l], p['ssd_dt_bias'][l], pad])[None, :],
        qg=jnp.tile(p['fox_q_norm'][l], FOX_HEADS)[None, :],
        kg=jnp.tile(p['fox_k_norm'][l], FOX_HEADS)[None, :],
        av=jnp.concatenate([jnp.zeros((FOX_HEADS,), F32), a_neg, pad])[None, :],
        dv=jnp.repeat(p['ssd_D'][l], HEAD_DIM)[None, :],
        xa_qg=jnp.tile(p['xa_q_norm'][l], MEM_HEADS)[None, :],
    )


def _token_mix(l, x, p, lp, nseq, seq, conv_state, ssd_conv_state, ssd_state, paged, finish):
    n = nseq * seq
    scb, cin, qa, kn, ka, v, va, z, xbc, fdt, *lf_t = _inproj(
        x, p['mix_norm'][l][None, :], p['w_in_t'], lp['qg'], lp['kg'], lp['fb'], p['pm'], p['pc'], l, seq)

    if paged is None:
        attn = _fox_prompt(qa, ka, va, nseq, seq)
        k_out = jnp.transpose(kn, (0, 3, 1, 2))
        v_out = jnp.transpose(v, (0, 3, 1, 2))
        logf = jnp.transpose(lf_t[0], (0, 2, 1))
    else:
        logf = fdt[:, F_LANE0:F_LANE0 + FOX_HEADS].reshape(nseq, seq, FOX_HEADS)
        k_out = kn.reshape(nseq, seq, FOX_HEADS, HEAD_DIM)
        v_out = v.reshape(nseq, seq, FOX_HEADS, HEAD_DIM)
        page_table, cache_kt, cache_vt, cache_lf_t = paged
        eye = jnp.eye(FOX_HEADS, dtype=BF16)
        q4 = jnp.transpose(qa.reshape(nseq, seq, FOX_HEADS, LANES)[..., :HEAD_DIM], (0, 2, 1, 3))
        qbd = (q4[:, :, :, None, :] * eye[None, :, None, :, None]).reshape(nseq, FOX_HEADS * seq, ATTN_DIM)
        attn = _fox_paged(l, page_table, qbd, kn.reshape(nseq, seq, ATTN_DIM), v.reshape(nseq, seq, ATTN_DIM),
                          jnp.transpose(logf, (0, 2, 1)), cache_kt, cache_vt, cache_lf_t,
                          nseq, seq).reshape(n, ATTN_DIM)

    yssd, h_end = _ssd(xbc, _pad_state(ssd_conv_state, SSD_CONV), fdt, z, ssd_state, p['ssd_conv_w'][l],
                       p['ssd_conv_b'][l][None, :], lp['av'], lp['dv'], nseq, seq)
    x = finish(x, scb, cin, _pad_state(conv_state, SC_WIDTH), attn, yssd)

    cin3 = cin.reshape(nseq, seq, CONV_DIM)
    xbc3 = xbc.reshape(nseq, seq, XBC_DIM)
    new_conv = jnp.concatenate([conv_state, cin3], axis=1)[:, -(SC_WIDTH - 1):]
    new_ssd_conv = jnp.concatenate([ssd_conv_state, xbc3], axis=1)[:, -(SSD_CONV - 1):]
    state = (k_out, v_out, logf, new_conv, new_ssd_conv, h_end)
    return x, state


def kernel(x_prompt, x_sample, cache_fox_k, cache_fox_v, cache_fox_logf, cache_mem_k, cache_mem_v,
           state_conv, state_ssd_conv, state_ssd, page_table, mem_prompt,
           ffn1_norm, ffn1_wg, ffn1_wu, ffn1_wd, mix_norm, w_in, sc_conv_w, fox_q_norm, fox_k_norm,
           fox_f_bias, ssd_conv_w, ssd_conv_b, ssd_dt_bias, ssd_A_log, ssd_D, mix_out_norm, w_out,
           xa_norm, mem_norm, xa_wq, xa_wk, xa_wv, xa_q_norm, xa_k_norm, xa_wo,
           ffn2_norm, ffn2_wg, ffn2_wu, ffn2_wd, final_norm):
    b_p, s_p, _ = x_prompt.shape
    b_s, s_s, _ = x_sample.shape
    n_phys = cache_fox_k.shape[1]
    pm, pc = _bias_placement()
    p = dict(mix_norm=mix_norm, w_in_t=_prep_w_in(w_in), sc_conv_w=sc_conv_w, fox_q_norm=fox_q_norm,
             fox_k_norm=fox_k_norm, fox_f_bias=fox_f_bias, ssd_conv_w=ssd_conv_w, ssd_conv_b=ssd_conv_b,
             ssd_dt_bias=ssd_dt_bias, ssd_A_log=ssd_A_log, ssd_D=ssd_D, mix_out_norm=mix_out_norm,
             xa_q_norm=xa_q_norm, w_out_bf=w_out.astype(BF16), pm=pm, pc=pc)
    wg1, wu1, wd1 = ffn1_wg.astype(BF16), ffn1_wu.astype(BF16), ffn1_wd.astype(BF16)
    wg2, wu2, wd2 = ffn2_wg.astype(BF16), ffn2_wu.astype(BF16), ffn2_wd.astype(BF16)
    wq, wo = xa_wq.astype(BF16), xa_wo.astype(BF16)
    fg = final_norm[None, :]

    km_p, vm_p = _memkv(mem_prompt, mem_norm[:, None, :], xa_wk.astype(BF16), xa_wv.astype(BF16),
                        jnp.tile(xa_k_norm, (1, MEM_HEADS))[:, None, :])
    km_s, vm_s = cache_mem_k, cache_mem_v
    cache_kt = jnp.transpose(cache_fox_k, (0, 1, 3, 4, 2)).reshape(DEPTH, n_phys, ATTN_DIM, PAGE_SIZE)
    cache_vt = jnp.transpose(cache_fox_v, (0, 1, 3, 4, 2)).reshape(DEPTH, n_phys, ATTN_DIM, PAGE_SIZE)
    cache_lf_t = jnp.transpose(cache_fox_logf, (0, 1, 3, 2))

    zero_conv = jnp.zeros((b_p, SC_WIDTH - 1, CONV_DIM), F32)
    zero_ssd_conv = jnp.zeros((b_p, SSD_CONV - 1, XBC_DIM), F32)
    zero_ssd = jnp.zeros((b_p, SSD_HEADS, HEAD_DIM, SSD_STATE), F32)

    xp = x_prompt.reshape(b_p * s_p, D_MODEL)
    xs = x_sample.reshape(b_s * s_s, D_MODEL)
    st_p, st_s = [], []
    for l in range(DEPTH):
        lp = _layer_params(l, p)
        last = l == DEPTH - 1
        groups = []
        for x, nseq, seq, km, vm, cs, scs, ss, paged in (
                (xp, b_p, s_p, km_p, vm_p, zero_conv, zero_ssd_conv, zero_ssd, None),
                (xs, b_s, s_s, km_s, vm_s, state_conv[l], state_ssd_conv[l], state_ssd[l],
                 (page_table, cache_kt, cache_vt, cache_lf_t))):
            def finish(x, scb, cin, conv_pad, attn, yssd, nseq=nseq, seq=seq, km=km, vm=vm):
                mix = (p['sc_conv_w'][l], p['mix_out_norm'][l][None, :], p['w_out_bf'])
                xa = (xa_norm[l][None, :], wq, lp['xa_qg'], wo)
                f2 = (ffn2_norm[l][None, :], wg2, wu2, wd2, fg)
                if seq % ROW_TILE == 0:
                    x = _merge_xattn(x, scb, cin, conv_pad, attn, yssd, *mix, km, vm, *xa, l, nseq, seq)
                else:
                    x = _merge(x, scb, cin, conv_pad, attn, yssd, *mix, l, nseq, seq)
                    x = _xattn(x, km, vm, l, *xa, nseq, seq)
                return _ffn(x, *f2, l, last)

            x = _ffn(x, ffn1_norm[l][None, :], wg1, wu1, wd1, fg, l, False)
            x, st = _token_mix(l, x, p, lp, nseq, seq, cs, scs, ss, paged, finish)
            groups.append((x, st))
        (xp, sp), (xs, ss_) = groups
        st_p.append(sp)
        st_s.append(ss_)

    def stack(states, idx):
        return jnp.stack([s[idx] for s in states])

    return (xp.reshape(b_p, s_p, D_MODEL), xs.reshape(b_s, s_s, D_MODEL),
            stack(st_p, 0), stack(st_p, 1), stack(st_p, 2),
            km_p.reshape(DEPTH, b_p, MEM_LEN, MEM_HEADS, MEM_HEAD_DIM),
            vm_p.reshape(DEPTH, b_p, MEM_LEN, MEM_HEADS, MEM_HEAD_DIM),
            stack(st_p, 3), stack(st_p, 4), stack(st_p, 5),
            stack(st_s, 0), stack(st_s, 1), stack(st_s, 2), stack(st_s, 3), stack(st_s, 4), stack(st_s, 5))
```

```python
import functools

import numpy as np
import jax
import jax.numpy as jnp
from jax import lax
from jax.experimental import pallas as pl
from jax.experimental.pallas import tpu as pltpu

F32 = jnp.float32
BF16 = jnp.bfloat16

D_MODEL = 1024
DEPTH = 4
PAGE_SIZE = 128
HEAD_DIM = 64
CONV_DIM = 256
ATTN_DIM = 512
SSD_DIM = 256
FOX_HEADS = 8
SC_WIDTH = 3
SSD_HEADS = 4
SSD_STATE = 64
SSD_CONV = 4
SSD_CHUNK = 128
XBC_DIM = 512
FFN_DIM = 2816
MEM_LEN = 256
MEM_HEADS = 4
MEM_HEAD_DIM = 128
MEM_DIM = 512
RMS_EPS = 1e-6
LOG2E = 1.4426950408889634

CUMSUM_BLOCK = 256
LANES = 128
SUBLANES = 8
VMEM_LIMIT = 56 * 1024 * 1024
ROW_TILE = 512
PAGES_PER_STEP = 32
XATTN_SEQS_PER_STEP = 8
FOX_HEADS_PER_STEP = 4
SSD_CHUNKS_PER_STEP = 4
SSD_SEQS_PER_STEP = 8
PROJ_QKV = 3 * CONV_DIM + 3 * ATTN_DIM
F_LANE0 = 0
DT_LANE0 = FOX_HEADS
PAD_DIM = FOX_HEADS * LANES
BIAS_LANE0 = HEAD_DIM
N_SPLIT = 3


def _params(sem):
    return pltpu.CompilerParams(dimension_semantics=sem, vmem_limit_bytes=VMEM_LIMIT)


def _resident(shape):
    nd = len(shape)
    return pl.BlockSpec(shape, lambda *_: (0,) * nd, pipeline_mode=pl.Buffered(1))


def _resident_layer(shape, layer):
    nd = len(shape)
    return pl.BlockSpec((None,) + tuple(shape), lambda *_: (layer,) + (0,) * nd, pipeline_mode=pl.Buffered(1))


def _rms(x, g):
    ms = jnp.mean(x * x, axis=-1, keepdims=True)
    return x * lax.rsqrt(ms + RMS_EPS) * g


def _head_rms(x, g, head_dim):
    width = x.shape[-1]
    pieces = []
    for c in range(width // LANES):
        blk = x[:, c * LANES:(c + 1) * LANES]
        sq = blk * blk
        s_all = jnp.sum(sq, axis=-1, keepdims=True)
        if head_dim == LANES:
            ms = s_all * (1.0 / LANES)
        else:
            lo = lax.broadcasted_iota(jnp.int32, sq.shape, 1) < head_dim
            s_lo = jnp.sum(jnp.where(lo, sq, 0.0), axis=-1, keepdims=True)
            ms = jnp.where(lo, s_lo, s_all - s_lo) * (1.0 / head_dim)
        pieces.append(blk * lax.rsqrt(ms + RMS_EPS))
    return jnp.concatenate(pieces, axis=-1) * g


def _silu(x):
    return x * jax.nn.sigmoid(x)


def _softplus(x):
    return jnp.maximum(x, 0.0) + jnp.log1p(jnp.exp(-jnp.abs(x)))


def _split3(x):
    hi = x.astype(BF16)
    r1 = x - hi.astype(F32)
    mid = r1.astype(BF16)
    lo = (r1 - mid.astype(F32)).astype(BF16)
    return hi, mid, lo


def _dot_exact_rhs(x, m_bf16):
    hi, mid, lo = _split3(x)
    out = jnp.dot(hi, m_bf16, preferred_element_type=F32)
    out = out + jnp.dot(mid, m_bf16, preferred_element_type=F32)
    return out + jnp.dot(lo, m_bf16, preferred_element_type=F32)


def _dot_exact_lhs(m_bf16, x):
    hi, mid, lo = _split3(x)
    out = jnp.dot(m_bf16, hi, preferred_element_type=F32)
    out = out + jnp.dot(m_bf16, mid, preferred_element_type=F32)
    return out + jnp.dot(m_bf16, lo, preferred_element_type=F32)


def _lower_tri(n, seq=None):
    r = lax.broadcasted_iota(jnp.int32, (n, n), 0)
    c = lax.broadcasted_iota(jnp.int32, (n, n), 1)
    keep = c <= r
    if seq is not None and seq < n:
        shift = jnp.int32(seq.bit_length() - 1)
        keep = keep & (lax.shift_right_logical(r, shift) == lax.shift_right_logical(c, shift))
    return keep.astype(BF16)


def _upper_tri(n):
    r = lax.broadcasted_iota(jnp.int32, (n, n), 0)
    c = lax.broadcasted_iota(jnp.int32, (n, n), 1)
    return (r <= c).astype(BF16)


def _nt(a, b):
    return lax.dot_general(a, b, (((1,), (1,)), ((), ())), preferred_element_type=F32)


def _tn(a, b):
    return lax.dot_general(a, b, (((0,), (0,)), ((), ())), preferred_element_type=F32)


def _row_tile(x, reps):
    return x if reps == 1 else jnp.concatenate([x] * reps, axis=0)


def _lane_tile(x, reps):
    return x if reps == 1 else jnp.concatenate([x] * reps, axis=-1)


def _ffn_body(x, g_ref, wg_ref, wu_ref, wd_ref, fg_ref, final):
    h = _rms(x, g_ref[...]).astype(BF16)
    a = jnp.dot(h, wg_ref[...], preferred_element_type=F32)
    b = jnp.dot(h, wu_ref[...], preferred_element_type=F32)
    t = (_silu(a) * b).astype(BF16)
    out = x + 0.5 * jnp.dot(t, wd_ref[...], preferred_element_type=F32)
    return _rms(out, fg_ref[...]) if final else out


def _ffn_kernel(x_ref, g_ref, wg_ref, wu_ref, wd_ref, fg_ref, o_ref, *, final):
    o_ref[...] = _ffn_body(x_ref[...], g_ref, wg_ref, wu_ref, wd_ref, fg_ref, final)


def _ffn(x, g, wg, wu, wd, fg, layer, final):
    n = x.shape[0]
    tm = min(ROW_TILE, n)
    return pl.pallas_call(
        functools.partial(_ffn_kernel, final=final),
        out_shape=jax.ShapeDtypeStruct((n, D_MODEL), F32),
        grid=(n // tm,),
        in_specs=[
            pl.BlockSpec((tm, D_MODEL), lambda i: (i, 0)),
            _resident((1, D_MODEL)),
            _resident_layer((D_MODEL, FFN_DIM), layer),
            _resident_layer((D_MODEL, FFN_DIM), layer),
            _resident_layer((FFN_DIM, D_MODEL), layer),
            _resident((1, D_MODEL)),
        ],
        out_specs=pl.BlockSpec((tm, D_MODEL), lambda i: (i, 0)),
        compiler_params=_params(("arbitrary",)),
        name="ffn",
    )(x, g, wg, wu, wd, fg)


def _expand_heads(x, fill):
    lo = lax.broadcasted_iota(jnp.int32, (x.shape[0], LANES), 1) < HEAD_DIM
    blocks = []
    for c in range(ATTN_DIM // LANES):
        blk = x[:, c * LANES:(c + 1) * LANES]
        for half, src in enumerate((blk, pltpu.roll(blk, HEAD_DIM, axis=1))):
            h = 2 * c + half
            blocks.append(jnp.where(lo, src, fill[:, h * LANES:(h + 1) * LANES]))
    return jnp.concatenate(blocks, axis=-1).astype(BF16)


def _inproj_kernel(x_ref, g_ref, w_ref, wzx_ref, wfd_ref, qg_ref, kg_ref, fb_ref, pm_ref, pc_ref,
                   scb_ref, cin_ref, qa_ref, kn_ref, ka_ref, v_ref, va_ref,
                   z_ref, xbc_ref, fdt_ref, *rest, tm, seq, dim_major):
    carry_ref = rest[-1]
    h = _rms(x_ref[...], g_ref[...]).astype(BF16)

    def store_kv(ref, val):
        ref[...] = val.T.reshape(FOX_HEADS, HEAD_DIM, tm) if dim_major else val

    def proj(a, b, ref=w_ref):
        return _nt(h, ref[a:b, :])

    o = 0
    scb_ref[...] = proj(o, o + CONV_DIM)
    o += CONV_DIM
    cin_ref[...] = proj(o, o + CONV_DIM) * proj(o + CONV_DIM, o + 2 * CONV_DIM)
    o += 2 * CONV_DIM
    q = _head_rms(proj(o, o + ATTN_DIM), qg_ref[...], HEAD_DIM) * (HEAD_DIM ** -0.5 * LOG2E)
    o += ATTN_DIM
    k = _head_rms(proj(o, o + ATTN_DIM), kg_ref[...], HEAD_DIM)
    store_kv(kn_ref, k)
    o += ATTN_DIM
    v = proj(o, o + ATTN_DIM)
    store_kv(v_ref, v)
    z_ref[...] = proj(0, SSD_DIM, wzx_ref)
    xbc_ref[...] = proj(SSD_DIM, SSD_DIM + XBC_DIM, wzx_ref)
    u = proj(0, LANES, wfd_ref) + fb_ref[...]
    lane = lax.broadcasted_iota(jnp.int32, u.shape, 1)
    fdt = jnp.where(lane < DT_LANE0, -_softplus(-u), _softplus(u))
    fdt_ref[...] = fdt
    if dim_major:
        rest[0][...] = fdt.T[0:FOX_HEADS, :]

    @pl.when(pl.program_id(0) % max(seq // tm, 1) == 0)
    def _():
        carry_ref[...] = jnp.zeros_like(carry_ref)

    lf = jnp.where(lane < FOX_HEADS, fdt, 0.0) * LOG2E
    blk = min(tm, CUMSUM_BLOCK)
    tri = _lower_tri(blk, seq)
    carry = carry_ref[0:1, :]
    c_parts = []
    for r in range(tm // blk):
        c_blk = _dot_exact_lhs(tri, lf[r * blk:(r + 1) * blk, :])
        if seq >= blk and (r == 0 or (r * blk) % seq != 0):
            c_blk = c_blk + carry
        carry = c_blk[blk - 1:blk, :]
        c_parts.append(c_blk)
    c = jnp.concatenate(c_parts, axis=0)
    carry_ref[...] = jnp.broadcast_to(carry, carry_ref.shape)
    hi, mid, lo = _split3(c)
    packed = (hi.astype(F32) + pltpu.roll(mid.astype(F32), FOX_HEADS, axis=1)
              + pltpu.roll(lo.astype(F32), 2 * FOX_HEADS, axis=1)).astype(BF16)
    both = jnp.dot(packed, pm_ref[...], preferred_element_type=F32)
    lane_blk = _lane_tile(lax.broadcasted_iota(jnp.int32, (tm, LANES), 1), FOX_HEADS)
    q_side = lane_blk < BIAS_LANE0 + N_SPLIT
    qa_ref[...] = _expand_heads(q, jnp.where(q_side, both, pc_ref[:, 0:PAD_DIM]))
    ka_ref[...] = _expand_heads(k, jnp.where(q_side, pc_ref[:, PAD_DIM:2 * PAD_DIM], both))
    va_ref[...] = _expand_heads(v, pc_ref[:, 2 * PAD_DIM:3 * PAD_DIM])


def _bias_placement():
    pm = np.zeros((LANES, PAD_DIM), np.float32)
    pc = np.zeros((1, 3 * PAD_DIM), np.float32)
    for h in range(FOX_HEADS):
        for part in range(N_SPLIT):
            src = part * FOX_HEADS + h
            pm[src, h * LANES + BIAS_LANE0 + part] = 1.0
            pm[src, h * LANES + BIAS_LANE0 + N_SPLIT + part] = -1.0
            pc[0, h * LANES + BIAS_LANE0 + N_SPLIT + part] = 1.0
            pc[0, PAD_DIM + h * LANES + BIAS_LANE0 + part] = 1.0
        pc[0, 2 * PAD_DIM + h * LANES + HEAD_DIM] = 1.0
    return jnp.asarray(pm, BF16), jnp.asarray(pc, F32)


def _inproj(x, g, w, qg, kg, fb, pm, pc, layer, seq):
    w_t, w_zx, w_fd = w
    n = x.shape[0]
    tm = min(ROW_TILE, n)
    widths = [(CONV_DIM, F32), (CONV_DIM, F32), (PAD_DIM, BF16), (ATTN_DIM, F32), (PAD_DIM, BF16),
              (ATTN_DIM, F32), (PAD_DIM, BF16), (SSD_DIM, F32), (XBC_DIM, F32), (LANES, F32)]
    out_shape = [jax.ShapeDtypeStruct((n, w_), dt) for w_, dt in widths]
    out_specs = [pl.BlockSpec((tm, w_), lambda i: (i, 0)) for w_, _ in widths]
    dim_major = seq >= tm
    if dim_major:
        tps = seq // tm
        kv_shape = jax.ShapeDtypeStruct((n // seq, FOX_HEADS, HEAD_DIM, seq), F32)
        kv_spec = pl.BlockSpec((None, FOX_HEADS, HEAD_DIM, tm), lambda i: (i // tps, 0, 0, i % tps))
        out_shape[3], out_shape[5] = kv_shape, kv_shape
        out_specs[3], out_specs[5] = kv_spec, kv_spec
        out_shape.append(jax.ShapeDtypeStruct((n // seq, FOX_HEADS, seq), F32))
        out_specs.append(pl.BlockSpec((None, FOX_HEADS, tm), lambda i: (i // tps, 0, i % tps)))
    return pl.pallas_call(
        functools.partial(_inproj_kernel, tm=tm, seq=seq, dim_major=dim_major),
        out_shape=out_shape,
        grid=(n // tm,),
        in_specs=[
            pl.BlockSpec((tm, D_MODEL), lambda i: (i, 0)),
            _resident((1, D_MODEL)),
            _resident_layer((PROJ_QKV, D_MODEL), layer),
            _resident_layer((SSD_DIM + XBC_DIM, D_MODEL), layer),
            _resident_layer((LANES, D_MODEL), layer),
            _resident((1, ATTN_DIM)),
            _resident((1, ATTN_DIM)),
            _resident((1, LANES)),
            _resident((LANES, PAD_DIM)),
            _resident((1, 3 * PAD_DIM)),
        ],
        out_specs=out_specs,
        scratch_shapes=[pltpu.VMEM((SUBLANES, LANES), F32)],
        compiler_params=_params(("arbitrary",)),
        name="inproj",
    )(x, g, w_t, w_zx, w_fd, qg, kg, fb, pm, pc)


def _fox_kernel(q_ref, k_ref, v_ref, o_ref, m_ref, acc_ref, *, tq):
    i = pl.program_id(2)
    m_ref[...] = jnp.full(m_ref.shape, -jnp.inf, F32)
    acc_ref[...] = jnp.zeros_like(acc_ref)

    def tile(start, width, masked):
        rows = pl.ds(pl.multiple_of(start, width), width)
        for hh in range(FOX_HEADS_PER_STEP):
            sl = slice(hh * LANES, (hh + 1) * LANES)
            s = _nt(q_ref[:, sl], k_ref[rows, sl])
            if masked:
                row = lax.broadcasted_iota(jnp.int32, s.shape, 0)
                col = lax.broadcasted_iota(jnp.int32, s.shape, 1)
                s = jnp.where(col <= row + (width - tq), s, -jnp.inf)
            m_old = m_ref[hh]
            m_new = jnp.maximum(m_old, jnp.max(s, axis=-1, keepdims=True))
            p = jnp.exp2(s - _lane_tile(m_new, width // LANES))
            acc_ref[hh] = jnp.exp2(m_old - m_new) * acc_ref[hh] + jnp.dot(
                p.astype(BF16), v_ref[rows, sl], preferred_element_type=F32)
            m_ref[hh] = m_new

    def body(j, carry):
        tile(j * (2 * tq), 2 * tq, False)
        return carry

    lax.fori_loop(0, lax.shift_right_logical(i, 1), body, 0)

    @pl.when(lax.bitwise_and(i, 1) == 1)
    def _():
        tile((i - 1) * tq, 2 * tq, True)

    @pl.when(lax.bitwise_and(i, 1) == 0)
    def _():
        tile(i * tq, tq, True)
    outs = []
    for hh in range(FOX_HEADS_PER_STEP):
        a = acc_ref[hh]
        outs.append(a[:, 0:HEAD_DIM] / a[:, HEAD_DIM:HEAD_DIM + 1])
    o_ref[...] = jnp.concatenate(outs, axis=-1)


def _fox_prompt(qa, ka, va, nseq, seq):
    tq = min(ROW_TILE, seq)
    nq = seq // tq
    nh = FOX_HEADS_PER_STEP
    return pl.pallas_call(
        functools.partial(_fox_kernel, tq=tq),
        out_shape=jax.ShapeDtypeStruct((nseq * seq, ATTN_DIM), F32),
        grid=(nseq, FOX_HEADS // nh, nq),
        in_specs=[
            pl.BlockSpec((tq, nh * LANES), lambda b, h, i: (b * nq + i, h)),
            pl.BlockSpec((seq, nh * LANES), lambda b, h, i: (b, h)),
            pl.BlockSpec((seq, nh * LANES), lambda b, h, i: (b, h)),
        ],
        out_specs=pl.BlockSpec((tq, nh * HEAD_DIM), lambda b, h, i: (b * nq + i, h)),
        scratch_shapes=[pltpu.VMEM((nh, tq, LANES), F32), pltpu.VMEM((nh, tq, LANES), F32)],
        compiler_params=_params(("arbitrary",) * 3),
        name="fox_prompt",
    )(qa, ka, va)


def _fox_paged_kernel(pt_ref, q_ref, kn_ref, vn_ref, lfn_ref, *rest, tokens):
    npg = PAGES_PER_STEP
    k_refs = rest[:npg]
    v_refs = rest[npg:2 * npg]
    lf_refs = rest[2 * npg:3 * npg]
    o_ref, m_ref, l_ref, acc_ref, carry_ref, c_ref = rest[3 * npg:]
    g = pl.program_id(1)
    rows = FOX_HEADS * tokens

    @pl.when(g == 0)
    def _():
        m_ref[...] = jnp.full(m_ref.shape, -jnp.inf, F32)
        l_ref[...] = jnp.zeros_like(l_ref)
        acc_ref[...] = jnp.zeros_like(acc_ref)
        carry_ref[...] = jnp.zeros_like(carry_ref)

    q = q_ref[...]
    tri = _upper_tri(PAGE_SIZE)

    def update(k_list, v_list, lf_list):
        n = len(k_list)
        nr = n * FOX_HEADS
        parts = _split3(jnp.concatenate(lf_list, axis=0) * LOG2E)

        def times(m_bf16):
            return sum(jnp.dot(x, m_bf16, preferred_element_type=F32) for x in parts)

        c = times(tri) + _row_tile(carry_ref[...], n)
        if n > 1:
            tot = times(jnp.ones((PAGE_SIZE, PAGE_SIZE), BF16))
            r = lax.broadcasted_iota(jnp.int32, (nr, nr), 0)
            col = lax.broadcasted_iota(jnp.int32, (nr, nr), 1)
            same_head = lax.bitwise_and(r, FOX_HEADS - 1) == lax.bitwise_and(col, FOX_HEADS - 1)
            earlier = jnp.where(col < r, jnp.where(same_head, 1.0, 0.0), 0.0).astype(BF16)
            off = _dot_exact_lhs(earlier, tot)
            c = c + off
            carry_ref[...] = carry_ref[...] + off[nr - FOX_HEADS:nr, :] + tot[nr - FOX_HEADS:nr, :]
        c_ref[0:nr, :] = c
        s_list = []
        for idx in range(n):
            c_rows = jnp.concatenate(
                [jnp.broadcast_to(c_ref[idx * FOX_HEADS + h:idx * FOX_HEADS + h + 1, :], (tokens, PAGE_SIZE))
                 for h in range(FOX_HEADS)], axis=0)
            s_list.append(jnp.dot(q, k_list[idx].astype(BF16), preferred_element_type=F32) - c_rows)
        s = jnp.concatenate(s_list, axis=-1)
        m_old = m_ref[...]
        m_new = jnp.maximum(m_old, jnp.max(s, axis=-1, keepdims=True))
        alpha = jnp.exp2(m_old - m_new)
        p = jnp.exp2(s - _lane_tile(m_new, n))
        l_ref[...] = alpha * l_ref[...] + jnp.sum(p, axis=-1, keepdims=True)
        pv = _nt(p[:, 0:PAGE_SIZE].astype(BF16), v_list[0].astype(BF16))
        for idx in range(1, n):
            pv = pv + _nt(p[:, idx * PAGE_SIZE:(idx + 1) * PAGE_SIZE].astype(BF16), v_list[idx].astype(BF16))
        acc_ref[...] = _lane_tile(alpha, ATTN_DIM // LANES) * acc_ref[...] + pv
        m_ref[...] = m_new

    update([r[...] for r in k_refs], [r[...] for r in v_refs], [r[...] for r in lf_refs])

    @pl.when(g == pl.num_programs(1) - 1)
    def _():
        c_new = _dot_exact_rhs(lfn_ref[...] * LOG2E, _upper_tri(tokens)) + carry_ref[:, 0:tokens]
        c_ref[0:FOX_HEADS, 0:tokens] = c_new
        c_rows = jnp.concatenate(
            [jnp.broadcast_to(c_ref[h:h + 1, 0:tokens], (tokens, tokens)) for h in range(FOX_HEADS)], axis=0)
        s = _nt(q, kn_ref[...].astype(BF16)) - c_rows
        t_idx = lax.bitwise_and(lax.broadcasted_iota(jnp.int32, s.shape, 0), tokens - 1)
        s = jnp.where(lax.broadcasted_iota(jnp.int32, s.shape, 1) <= t_idx, s, -jnp.inf)
        m_old = m_ref[...]
        m_new = jnp.maximum(m_old, jnp.max(s, axis=-1, keepdims=True))
        alpha = jnp.exp2(m_old - m_new)
        p = jnp.exp2(s - m_new[:, 0:tokens])
        l_new = alpha * l_ref[...] + jnp.sum(p, axis=-1, keepdims=True)
        acc = _lane_tile(alpha, ATTN_DIM // LANES) * acc_ref[...] + jnp.dot(
            p.astype(BF16), vn_ref[...].astype(BF16), preferred_element_type=F32)
        full = acc / _lane_tile(l_new, ATTN_DIM // LANES)
        lane_head = lax.shift_right_logical(
            lax.broadcasted_iota(jnp.int32, (tokens, ATTN_DIM), 1), jnp.int32(HEAD_DIM.bit_length() - 1))
        out = jnp.zeros((tokens, ATTN_DIM), F32)
        for h in range(FOX_HEADS):
            out = out + jnp.where(lane_head == h, full[h * tokens:(h + 1) * tokens, :], 0.0)
        o_ref[...] = out


def _fox_paged(layer, page_table, qbd, kn_t, vn_t, lfn_t, cache_kt, cache_vt, cache_lf_t, nseq, tokens):
    npg = PAGES_PER_STEP
    n_groups = page_table.shape[1] // npg
    rows = FOX_HEADS * tokens

    def page_spec(width2, n):
        return pl.BlockSpec((None, None, width2, PAGE_SIZE),
                            lambda b, g, pt: (layer, pt[b, g * npg + n], 0, 0))

    in_specs = [
        pl.BlockSpec((None, rows, ATTN_DIM), lambda b, g, pt: (b, 0, 0)),
        pl.BlockSpec((None, tokens, ATTN_DIM), lambda b, g, pt: (b, 0, 0)),
        pl.BlockSpec((None, tokens, ATTN_DIM), lambda b, g, pt: (b, 0, 0)),
        pl.BlockSpec((None, FOX_HEADS, tokens), lambda b, g, pt: (b, 0, 0)),
    ]
    in_specs += [page_spec(ATTN_DIM, n) for n in range(npg)]
    in_specs += [page_spec(ATTN_DIM, n) for n in range(npg)]
    in_specs += [page_spec(FOX_HEADS, n) for n in range(npg)]
    grid_spec = pltpu.PrefetchScalarGridSpec(
        num_scalar_prefetch=1,
        grid=(nseq, n_groups),
        in_specs=in_specs,
        out_specs=pl.BlockSpec((None, tokens, ATTN_DIM), lambda b, g, pt: (b, 0, 0)),
        scratch_shapes=[pltpu.VMEM((rows, LANES), F32), pltpu.VMEM((rows, LANES), F32),
                        pltpu.VMEM((rows, ATTN_DIM), F32), pltpu.VMEM((FOX_HEADS, LANES), F32),
                        pltpu.VMEM((npg * FOX_HEADS, LANES), F32)],
    )
    return pl.pallas_call(
        functools.partial(_fox_paged_kernel, tokens=tokens),
        out_shape=jax.ShapeDtypeStruct((nseq, tokens, ATTN_DIM), F32),
        grid_spec=grid_spec,
        compiler_params=_params(("arbitrary", "arbitrary")),
        name="fox_paged",
    )(page_table, qbd, kn_t, vn_t, lfn_t,
      *([cache_kt] * npg), *([cache_vt] * npg), *([cache_lf_t] * npg))


def _ssd_kernel(xbc_ref, halo_ref, st_ref, fdt_ref, z_ref, h0_ref,
                cw_ref, cb_ref, av_ref, dv_ref,
                y_ref, hout_ref, buf_ref, h_ref, *, lc, rows):
    j = pl.program_id(1)
    bs = st_ref.shape[0]

    @pl.when(j == 0)
    def _():
        h_ref[...] = h0_ref[...]
        buf_ref[:, 0:SUBLANES, :] = st_ref[...]

    @pl.when(j > 0)
    def _():
        buf_ref[0, 0:SUBLANES, :] = halo_ref[...]

    buf_ref[:, SUBLANES:SUBLANES + rows, :] = xbc_ref[...].reshape(bs, rows, XBC_DIM)
    base = SUBLANES - (SSD_CONV - 1)
    u_all = buf_ref[:, base:base + rows, :] * cw_ref[0:1, :]
    for k in range(1, SSD_CONV):
        u_all = u_all + buf_ref[:, base + k:base + k + rows, :] * cw_ref[k:k + 1, :]
    u_all = _silu(u_all + cb_ref[...]).reshape(bs * rows, XBC_DIM)

    tri_lo = _lower_tri(lc)
    tri_up = _upper_tri(lc)
    pick = (lax.broadcasted_iota(jnp.int32, (SUBLANES, LANES), 1)
            == lax.broadcasted_iota(jnp.int32, (SUBLANES, LANES), 0) + DT_LANE0).astype(BF16)

    def rows_of(x):
        return sum(_nt(pick, part) for part in _split3(x))

    row = lax.broadcasted_iota(jnp.int32, (lc, lc), 0)
    col = lax.broadcasted_iota(jnp.int32, (lc, lc), 1)
    causal = col <= row
    dvec = dv_ref[...]

    for s, c in ((s, c) for s in range(bs) for c in range(rows // lc)):
        tok = slice(s * rows + c * lc, s * rows + (c + 1) * lc)
        u = u_all[tok, :]
        z = z_ref[tok, :]
        fdt = fdt_ref[tok, :]
        da = fdt * av_ref[...]
        acum = _dot_exact_lhs(tri_lo, da)
        dtt = rows_of(fdt)
        acum_t = _dot_exact_rhs(rows_of(da), tri_up)
        outs = []
        for h in range(SSD_HEADS):
            grp = h // (SSD_HEADS // 2)
            xh = u[:, h * HEAD_DIM:(h + 1) * HEAD_DIM]
            bh = u[:, SSD_DIM + grp * SSD_STATE:SSD_DIM + (grp + 1) * SSD_STATE]
            ch = u[:, SSD_DIM + 2 * SSD_STATE + grp * SSD_STATE:SSD_DIM + 2 * SSD_STATE + (grp + 1) * SSD_STATE]
            lane = DT_LANE0 + h
            a_col = acum[:, lane:lane + 1]
            dt_col = fdt[:, lane:lane + 1]
            a_row = acum_t[h:h + 1, :]
            dt_row = dtt[h:h + 1, :]
            a_last = acum[lc - 1:lc, lane:lane + 1]
            decay = jnp.exp(jnp.where(causal, a_col - a_row, -jnp.inf))
            xb = xh.astype(BF16)
            bb = bh.astype(BF16)
            cb16 = ch.astype(BF16)
            w = _nt(cb16, bb) * decay * dt_row
            h_prev = h_ref[s, h]
            y = jnp.dot(w.astype(BF16), xb, preferred_element_type=F32)
            y = y + _nt(cb16, h_prev.astype(BF16)) * jnp.exp(a_col)
            w_end = jnp.exp(a_last - a_col) * dt_col
            h_ref[s, h] = h_prev * jnp.exp(a_last) + _tn((xh * w_end).astype(BF16), bb)
            sl = slice(h * HEAD_DIM, (h + 1) * HEAD_DIM)
            outs.append((y + dvec[:, sl] * xh) * _silu(z[:, sl]))
        y_ref[tok, :] = jnp.concatenate(outs, axis=-1)

    @pl.when(j == pl.num_programs(1) - 1)
    def _():
        hout_ref[...] = h_ref[...]


def _ssd(xbc, st_pad, fdt, z, h0, cw, cb, av, dv, nseq, seq):
    lc = min(SSD_CHUNK, seq)
    rows = min(SSD_CHUNKS_PER_STEP * lc, seq)
    nc = seq // rows
    bs = 1 if nc > 1 else min(nseq, SSD_SEQS_PER_STEP)
    nrows = bs * rows
    hb = nrows // SUBLANES
    state_spec = pl.BlockSpec((bs, SSD_HEADS, HEAD_DIM, SSD_STATE), lambda b, j: (b, 0, 0, 0))
    return pl.pallas_call(
        functools.partial(_ssd_kernel, lc=lc, rows=rows),
        out_shape=[jax.ShapeDtypeStruct((nseq * seq, SSD_DIM), F32),
                   jax.ShapeDtypeStruct((nseq, SSD_HEADS, HEAD_DIM, SSD_STATE), F32)],
        grid=(nseq // bs, nc),
        in_specs=[
            pl.BlockSpec((nrows, XBC_DIM), lambda b, j: (b * nc + j, 0)),
            pl.BlockSpec((SUBLANES, XBC_DIM), lambda b, j: (jnp.maximum((b * nc + j) * hb - 1, 0), 0)),
            pl.BlockSpec((bs, SUBLANES, XBC_DIM), lambda b, j: (b, 0, 0)),
            pl.BlockSpec((nrows, LANES), lambda b, j: (b * nc + j, 0)),
            pl.BlockSpec((nrows, SSD_DIM), lambda b, j: (b * nc + j, 0)),
            state_spec,
            _resident((SSD_CONV, XBC_DIM)),
            _resident((1, XBC_DIM)),
            _resident((1, LANES)),
            _resident((1, SSD_DIM)),
        ],
        out_specs=[pl.BlockSpec((nrows, SSD_DIM), lambda b, j: (b * nc + j, 0)), state_spec],
        scratch_shapes=[pltpu.VMEM((bs, SUBLANES + rows, XBC_DIM), F32),
                        pltpu.VMEM((bs, SSD_HEADS, HEAD_DIM, SSD_STATE), F32)],
        compiler_params=_params(("arbitrary", "arbitrary")),
        name="ssd",
    )(xbc, xbc, st_pad, fdt, z, h0, cw, cb, av, dv)


def _merge_body(x_ref, scb_ref, cin_ref, halo_ref, st_ref, attn_ref, yssd_ref,
                cw_ref, g_ref, w_ref, buf_ref, tl):
    j = pl.program_id(1)
    bs = st_ref.shape[0]

    @pl.when(j == 0)
    def _():
        buf_ref[:, 0:SUBLANES, :] = st_ref[...]

    @pl.when(j > 0)
    def _():
        buf_ref[0, 0:SUBLANES, :] = halo_ref[...]

    buf_ref[:, SUBLANES:SUBLANES + tl, :] = cin_ref[...].reshape(bs, tl, CONV_DIM)
    base = SUBLANES - (SC_WIDTH - 1)
    y = buf_ref[:, base:base + tl, :] * cw_ref[0:1, :]
    for k in range(1, SC_WIDTH):
        y = y + buf_ref[:, base + k:base + k + tl, :] * cw_ref[k:k + 1, :]
    y = y.reshape(bs * tl, CONV_DIM)
    cat = jnp.concatenate([scb_ref[...] * y, attn_ref[...], yssd_ref[...]], axis=-1)
    cat = _head_rms(cat, g_ref[...], HEAD_DIM).astype(BF16)
    return x_ref[...] + jnp.dot(cat, w_ref[...], preferred_element_type=F32)


def _merge_kernel(x_ref, scb_ref, cin_ref, halo_ref, st_ref, attn_ref, yssd_ref,
                  cw_ref, g_ref, w_ref, o_ref, buf_ref, *, tl):
    o_ref[...] = _merge_body(x_ref, scb_ref, cin_ref, halo_ref, st_ref, attn_ref, yssd_ref,
                             cw_ref, g_ref, w_ref, buf_ref, tl)


def _merge(x, scb, cin, st_pad, attn, yssd, cw, g, w, layer, nseq, seq):
    tl = min(ROW_TILE, seq)
    nt = seq // tl
    bs = 1 if nt > 1 else min(nseq, ROW_TILE // tl)
    nrows = bs * tl
    hb = nrows // SUBLANES

    def rows(width):
        return pl.BlockSpec((nrows, width), lambda b, j: (b * nt + j, 0))

    return pl.pallas_call(
        functools.partial(_merge_kernel, tl=tl),
        out_shape=jax.ShapeDtypeStruct((nseq * seq, D_MODEL), F32),
        grid=(nseq // bs, nt),
        in_specs=[
            rows(D_MODEL), rows(CONV_DIM), rows(CONV_DIM),
            pl.BlockSpec((SUBLANES, CONV_DIM), lambda b, j: (jnp.maximum((b * nt + j) * hb - 1, 0), 0)),
            pl.BlockSpec((bs, SUBLANES, CONV_DIM), lambda b, j: (b, 0, 0)),
            rows(ATTN_DIM), rows(SSD_DIM),
            _resident((SC_WIDTH, CONV_DIM)),
            _resident((1, D_MODEL)),
            _resident_layer((D_MODEL, D_MODEL), layer),
        ],
        out_specs=rows(D_MODEL),
        scratch_shapes=[pltpu.VMEM((bs, SUBLANES + tl, CONV_DIM), F32)],
        compiler_params=_params(("arbitrary", "arbitrary")),
        name="merge",
    )(x, scb, cin, cin, st_pad, attn, yssd, cw, g, w)


def _memkv_kernel(mem_ref, g_ref, wk_ref, wv_ref, kg_ref, km_ref, vm_ref):
    m = _rms(mem_ref[...], g_ref[...]).astype(BF16)
    km_ref[...] = _head_rms(jnp.dot(m, wk_ref[...], preferred_element_type=F32), kg_ref[...], MEM_HEAD_DIM)
    vm_ref[...] = jnp.dot(m, wv_ref[...], preferred_element_type=F32)


def _memkv(mem, g, wk, wv, kg):
    nb = mem.shape[0]
    out = jax.ShapeDtypeStruct((DEPTH, nb, MEM_LEN, MEM_DIM), F32)
    return pl.pallas_call(
        _memkv_kernel,
        out_shape=[out, out],
        grid=(DEPTH, nb),
        in_specs=[
            pl.BlockSpec((None, MEM_LEN, D_MODEL), lambda l, b: (b, 0, 0)),
            pl.BlockSpec((None, 1, D_MODEL), lambda l, b: (l, 0, 0)),
            pl.BlockSpec((None, D_MODEL, MEM_DIM), lambda l, b: (l, 0, 0)),
            pl.BlockSpec((None, D_MODEL, MEM_DIM), lambda l, b: (l, 0, 0)),
            pl.BlockSpec((None, 1, MEM_DIM), lambda l, b: (l, 0, 0)),
        ],
        out_specs=[pl.BlockSpec((None, None, MEM_LEN, MEM_DIM), lambda l, b: (l, b, 0, 0))] * 2,
        compiler_params=_params(("arbitrary", "arbitrary")),
        name="mem_kv",
    )(mem, g, wk, wv, kg)


def _xattn_body(x, km_ref, vm_ref, g_ref, wq_ref, qg_ref, wo_ref, q_ref, att_ref, bs, tl):
    h = _rms(x, g_ref[...]).astype(BF16)
    q = _head_rms(jnp.dot(h, wq_ref[...], preferred_element_type=F32), qg_ref[...], MEM_HEAD_DIM)
    q = q * (MEM_HEAD_DIM ** -0.5)
    for hh in range(MEM_HEADS):
        q_ref[hh] = q[:, hh * MEM_HEAD_DIM:(hh + 1) * MEM_HEAD_DIM]

    def softmax(sc):
        p = jnp.exp(sc - jnp.max(sc, axis=-1, keepdims=True))
        return (p / jnp.sum(p, axis=-1, keepdims=True)).astype(BF16)

    for s in range(bs):
        rows = slice(s * tl, (s + 1) * tl)
        if len(km_ref.shape) == 4:
            km = km_ref[s].reshape(MEM_LEN * MEM_HEADS, MEM_HEAD_DIM).astype(BF16)
            vm = vm_ref[s].reshape(MEM_LEN * MEM_HEADS, MEM_HEAD_DIM).astype(BF16)
            qs = jnp.concatenate([q_ref[hh, rows, :] for hh in range(MEM_HEADS)], axis=0).astype(BF16)
            sc = _nt(qs, km)
            row_head = lax.shift_right_logical(lax.broadcasted_iota(jnp.int32, sc.shape, 0),
                                               jnp.int32(tl.bit_length() - 1))
            col_head = lax.bitwise_and(lax.broadcasted_iota(jnp.int32, sc.shape, 1), MEM_HEADS - 1)
            out = jnp.dot(softmax(jnp.where(row_head == col_head, sc, -jnp.inf)), vm,
                          preferred_element_type=F32)
            for hh in range(MEM_HEADS):
                att_ref[hh, rows, :] = out[hh * tl:(hh + 1) * tl, :]
        else:
            for hh in range(MEM_HEADS):
                sl = slice(hh * MEM_HEAD_DIM, (hh + 1) * MEM_HEAD_DIM)
                sc = _nt(q_ref[hh, rows, :].astype(BF16), km_ref[s, :, sl].astype(BF16))
                att_ref[hh, rows, :] = jnp.dot(softmax(sc), vm_ref[s, :, sl].astype(BF16),
                                               preferred_element_type=F32)

    att = jnp.concatenate([att_ref[hh] for hh in range(MEM_HEADS)], axis=-1).astype(BF16)
    return x + jnp.dot(att, wo_ref[...], preferred_element_type=F32)


def _xattn_kernel(x_ref, km_ref, vm_ref, g_ref, wq_ref, qg_ref, wo_ref, o_ref, q_ref, att_ref, *, bs, tl):
    o_ref[...] = _xattn_body(x_ref[...], km_ref, vm_ref, g_ref, wq_ref, qg_ref, wo_ref, q_ref, att_ref, bs, tl)


def _xattn(x, km, vm, layer, g, wq, qg, wo, nseq, seq):
    tl = min(ROW_TILE, seq)
    nt = seq // tl
    bs = 1 if nt > 1 else min(nseq, XATTN_SEQS_PER_STEP)
    rows = bs * tl
    mem_zeros = (0,) * (km.ndim - 2)
    mem_spec = pl.BlockSpec((None, bs) + tuple(km.shape[2:]), lambda b, j: (layer, b) + mem_zeros)
    row_spec = pl.BlockSpec((rows, D_MODEL), lambda b, j: (b * nt + j, 0))
    return pl.pallas_call(
        functools.partial(_xattn_kernel, bs=bs, tl=tl),
        out_shape=jax.ShapeDtypeStruct((nseq * seq, D_MODEL), F32),
        grid=(nseq // bs, nt),
        in_specs=[
            row_spec, mem_spec, mem_spec,
            _resident((1, D_MODEL)),
            _resident_layer((D_MODEL, MEM_DIM), layer),
            _resident((1, MEM_DIM)),
            _resident_layer((MEM_DIM, D_MODEL), layer),
        ],
        out_specs=row_spec,
        scratch_shapes=[pltpu.VMEM((MEM_HEADS, rows, MEM_HEAD_DIM), F32),
                        pltpu.VMEM((MEM_HEADS, rows, MEM_HEAD_DIM), F32)],
        compiler_params=_params(("arbitrary", "arbitrary")),
        name="xattn",
    )(x, km, vm, g, wq, qg, wo)


def _merge_xattn_kernel(x_ref, scb_ref, cin_ref, halo_ref, st_ref, attn_ref, yssd_ref, cw_ref, go_ref,
                        wout_ref, km_ref, vm_ref, gx_ref, wq_ref, qg_ref, wo_ref,
                        o_ref, buf_ref, q_ref, att_ref, *, tl):
    x = _merge_body(x_ref, scb_ref, cin_ref, halo_ref, st_ref, attn_ref, yssd_ref,
                    cw_ref, go_ref, wout_ref, buf_ref, tl)
    o_ref[...] = _xattn_body(x, km_ref, vm_ref, gx_ref, wq_ref, qg_ref, wo_ref, q_ref, att_ref, 1, tl)


def _merge_xattn(x, scb, cin, st_pad, attn, yssd, cw, g_out, w_out, km, vm, g_xa, wq, qg, wo,
                 layer, nseq, seq):
    tl = ROW_TILE
    nt = seq // tl
    hb = tl // SUBLANES

    def rows(width):
        return pl.BlockSpec((tl, width), lambda b, j: (b * nt + j, 0))

    mem_spec = pl.BlockSpec((None, 1, MEM_LEN, MEM_DIM), lambda b, j: (layer, b, 0, 0))
    return pl.pallas_call(
        functools.partial(_merge_xattn_kernel, tl=tl),
        out_shape=jax.ShapeDtypeStruct((nseq * seq, D_MODEL), F32),
        grid=(nseq, nt),
        in_specs=[
            rows(D_MODEL), rows(CONV_DIM), rows(CONV_DIM),
            pl.BlockSpec((SUBLANES, CONV_DIM), lambda b, j: (jnp.maximum((b * nt + j) * hb - 1, 0), 0)),
            pl.BlockSpec((1, SUBLANES, CONV_DIM), lambda b, j: (b, 0, 0)),
            rows(ATTN_DIM), rows(SSD_DIM),
            _resident((SC_WIDTH, CONV_DIM)),
            _resident((1, D_MODEL)),
            _resident_layer((D_MODEL, D_MODEL), layer),
            mem_spec, mem_spec,
            _resident((1, D_MODEL)),
            _resident_layer((D_MODEL, MEM_DIM), layer),
            _resident((1, MEM_DIM)),
            _resident_layer((MEM_DIM, D_MODEL), layer),
        ],
        out_specs=rows(D_MODEL),
        scratch_shapes=[pltpu.VMEM((1, SUBLANES + tl, CONV_DIM), F32),
                        pltpu.VMEM((MEM_HEADS, tl, MEM_HEAD_DIM), F32),
                        pltpu.VMEM((MEM_HEADS, tl, MEM_HEAD_DIM), F32)],
        compiler_params=_params(("arbitrary", "arbitrary")),
        name="merge_xattn",
    )(x, scb, cin, cin, st_pad, attn, yssd, cw, g_out, w_out, km, vm, g_xa, wq, qg, wo)


def _pad_state(state, width):
    nseq, _, c = state.shape
    return jnp.concatenate([jnp.zeros((nseq, SUBLANES - (width - 1), c), F32), state], axis=1)


def _prep_w_in(w_in):
    w_t = jnp.transpose(w_in, (0, 2, 1)).astype(BF16)
    n_z0 = PROJ_QKV + FOX_HEADS
    n_dt0 = n_z0 + SSD_DIM + XBC_DIM
    pad = jnp.zeros((DEPTH, LANES - FOX_HEADS - SSD_HEADS, D_MODEL), BF16)
    w_fd = jnp.concatenate([w_t[:, PROJ_QKV:n_z0], w_t[:, n_dt0:], pad], axis=1)
    return w_t, w_t[:, n_z0:n_dt0], w_fd


def _layer_params(l, p):
    pad = jnp.zeros((LANES - FOX_HEADS - SSD_HEADS,), F32)
    a_neg = -jnp.exp(p['ssd_A_log'][l])
    return dict(
        fb=jnp.concatenate([p['fox_f_bias'][l], p['ssd_dt_bias'][l], pad])[None, :],
        qg=jnp.tile(p['fox_q_norm'][l], FOX_HEADS)[None, :],
        kg=jnp.tile(p['fox_k_norm'][l], FOX_HEADS)[None, :],
        av=jnp.concatenate([jnp.zeros((FOX_HEADS,), F32), a_neg, pad])[None, :],
        dv=jnp.repeat(p['ssd_D'][l], HEAD_DIM)[None, :],
        xa_qg=jnp.tile(p['xa_q_norm'][l], MEM_HEADS)[None, :],
    )


def _token_mix(l, x, p, lp, nseq, seq, conv_state, ssd_conv_state, ssd_state, paged, finish):
    n = nseq * seq
    scb, cin, qa, kn, ka, v, va, z, xbc, fdt, *lf_t = _inproj(
        x, p['mix_norm'][l][None, :], p['w_in_t'], lp['qg'], lp['kg'], lp['fb'], p['pm'], p['pc'], l, seq)

    if paged is None:
        attn = _fox_prompt(qa, ka, va, nseq, seq)
        k_out = jnp.transpose(kn, (0, 3, 1, 2))
        v_out = jnp.transpose(v, (0, 3, 1, 2))
        logf = jnp.transpose(lf_t[0], (0, 2, 1))
    else:
        logf = fdt[:, F_LANE0:F_LANE0 + FOX_HEADS].reshape(nseq, seq, FOX_HEADS)
        k_out = kn.reshape(nseq, seq, FOX_HEADS, HEAD_DIM)
        v_out = v.reshape(nseq, seq, FOX_HEADS, HEAD_DIM)
        page_table, cache_kt, cache_vt, cache_lf_t = paged
        eye = jnp.eye(FOX_HEADS, dtype=BF16)
        q4 = jnp.transpose(qa.reshape(nseq, seq, FOX_HEADS, LANES)[..., :HEAD_DIM], (0, 2, 1, 3))
        qbd = (q4[:, :, :, None, :] * eye[None, :, None, :, None]).reshape(nseq, FOX_HEADS * seq, ATTN_DIM)
        attn = _fox_paged(l, page_table, qbd, kn.reshape(nseq, seq, ATTN_DIM), v.reshape(nseq, seq, ATTN_DIM),
                          jnp.transpose(logf, (0, 2, 1)), cache_kt, cache_vt, cache_lf_t,
                          nseq, seq).reshape(n, ATTN_DIM)

    yssd, h_end = _ssd(xbc, _pad_state(ssd_conv_state, SSD_CONV), fdt, z, ssd_state, p['ssd_conv_w'][l],
                       p['ssd_conv_b'][l][None, :], lp['av'], lp['dv'], nseq, seq)
    x = finish(x, scb, cin, _pad_state(conv_state, SC_WIDTH), attn, yssd)

    cin3 = cin.reshape(nseq, seq, CONV_DIM)
    xbc3 = xbc.reshape(nseq, seq, XBC_DIM)
    new_conv = jnp.concatenate([conv_state, cin3], axis=1)[:, -(SC_WIDTH - 1):]
    new_ssd_conv = jnp.concatenate([ssd_conv_state, xbc3], axis=1)[:, -(SSD_CONV - 1):]
    state = (k_out, v_out, logf, new_conv, new_ssd_conv, h_end)
    return x, state


def kernel(x_prompt, x_sample, cache_fox_k, cache_fox_v, cache_fox_logf, cache_mem_k, cache_mem_v,
           state_conv, state_ssd_conv, state_ssd, page_table, mem_prompt,
           ffn1_norm, ffn1_wg, ffn1_wu, ffn1_wd, mix_norm, w_in, sc_conv_w, fox_q_norm, fox_k_norm,
           fox_f_bias, ssd_conv_w, ssd_conv_b, ssd_dt_bias, ssd_A_log, ssd_D, mix_out_norm, w_out,
           xa_norm, mem_norm, xa_wq, xa_wk, xa_wv, xa_q_norm, xa_k_norm, xa_wo,
           ffn2_norm, ffn2_wg, ffn2_wu, ffn2_wd, final_norm):
    b_p, s_p, _ = x_prompt.shape
    b_s, s_s, _ = x_sample.shape
    n_phys = cache_fox_k.shape[1]
    pm, pc = _bias_placement()
    p = dict(mix_norm=mix_norm, w_in_t=_prep_w_in(w_in), sc_conv_w=sc_conv_w, fox_q_norm=fox_q_norm,
             fox_k_norm=fox_k_norm, fox_f_bias=fox_f_bias, ssd_conv_w=ssd_conv_w, ssd_conv_b=ssd_conv_b,
             ssd_dt_bias=ssd_dt_bias, ssd_A_log=ssd_A_log, ssd_D=ssd_D, mix_out_norm=mix_out_norm,
             xa_q_norm=xa_q_norm, w_out_bf=w_out.astype(BF16), pm=pm, pc=pc)
    wg1, wu1, wd1 = ffn1_wg.astype(BF16), ffn1_wu.astype(BF16), ffn1_wd.astype(BF16)
    wg2, wu2, wd2 = ffn2_wg.astype(BF16), ffn2_wu.astype(BF16), ffn2_wd.astype(BF16)
    wq, wo = xa_wq.astype(BF16), xa_wo.astype(BF16)
    fg = final_norm[None, :]

    km_p, vm_p = _memkv(mem_prompt, mem_norm[:, None, :], xa_wk.astype(BF16), xa_wv.astype(BF16),
                        jnp.tile(xa_k_norm, (1, MEM_HEADS))[:, None, :])
    km_s, vm_s = cache_mem_k, cache_mem_v
    cache_kt = jnp.transpose(cache_fox_k, (0, 1, 3, 4, 2)).reshape(DEPTH, n_phys, ATTN_DIM, PAGE_SIZE)
    cache_vt = jnp.transpose(cache_fox_v, (0, 1, 3, 4, 2)).reshape(DEPTH, n_phys, ATTN_DIM, PAGE_SIZE)
    cache_lf_t = jnp.transpose(cache_fox_logf, (0, 1, 3, 2))

    zero_conv = jnp.zeros((b_p, SC_WIDTH - 1, CONV_DIM), F32)
    zero_ssd_conv = jnp.zeros((b_p, SSD_CONV - 1, XBC_DIM), F32)
    zero_ssd = jnp.zeros((b_p, SSD_HEADS, HEAD_DIM, SSD_STATE), F32)

    xp = x_prompt.reshape(b_p * s_p, D_MODEL)
    xs = x_sample.reshape(b_s * s_s, D_MODEL)
    st_p, st_s = [], []
    for l in range(DEPTH):
        lp = _layer_params(l, p)
        last = l == DEPTH - 1
        groups = []
        for x, nseq, seq, km, vm, cs, scs, ss, paged in (
                (xp, b_p, s_p, km_p, vm_p, zero_conv, zero_ssd_conv, zero_ssd, None),
                (xs, b_s, s_s, km_s, vm_s, state_conv[l], state_ssd_conv[l], state_ssd[l],
                 (page_table, cache_kt, cache_vt, cache_lf_t))):
            def finish(x, scb, cin, conv_pad, attn, yssd, nseq=nseq, seq=seq, km=km, vm=vm):
                mix = (p['sc_conv_w'][l], p['mix_out_norm'][l][None, :], p['w_out_bf'])
                xa = (xa_norm[l][None, :], wq, lp['xa_qg'], wo)
                f2 = (ffn2_norm[l][None, :], wg2, wu2, wd2, fg)
                if seq % ROW_TILE == 0:
                    x = _merge_xattn(x, scb, cin, conv_pad, attn, yssd, *mix, km, vm, *xa, l, nseq, seq)
                else:
                    x = _merge(x, scb, cin, conv_pad, attn, yssd, *mix, l, nseq, seq)
                    x = _xattn(x, km, vm, l, *xa, nseq, seq)
                return _ffn(x, *f2, l, last)

            x = _ffn(x, ffn1_norm[l][None, :], wg1, wu1, wd1, fg, l, False)
            x, st = _token_mix(l, x, p, lp, nseq, seq, cs, scs, ss, paged, finish)
            groups.append((x, st))
        (xp, sp), (xs, ss_) = groups
        st_p.append(sp)
        st_s.append(ss_)

    def stack(states, idx):
        return jnp.stack([s[idx] for s in states])

    return (xp.reshape(b_p, s_p, D_MODEL), xs.reshape(b_s, s_s, D_MODEL),
            stack(st_p, 0), stack(st_p, 1), stack(st_p, 2),
            km_p.reshape(DEPTH, b_p, MEM_LEN, MEM_HEADS, MEM_HEAD_DIM),
            vm_p.reshape(DEPTH, b_p, MEM_LEN, MEM_HEADS, MEM_HEAD_DIM),
            stack(st_p, 3), stack(st_p, 4), stack(st_p, 5),
            stack(st_s, 0), stack(st_s, 1), stack(st_s, 2), stack(st_s, 3), stack(st_s, 4), stack(st_s, 5))
```

```python
import functools

import numpy as np
import jax
import jax.numpy as jnp
from jax import lax
from jax.experimental import pallas as pl
from jax.experimental.pallas import tpu as pltpu

F32 = jnp.float32
BF16 = jnp.bfloat16

D_MODEL = 1024
DEPTH = 4
PAGE_SIZE = 128
HEAD_DIM = 64
CONV_DIM = 256
ATTN_DIM = 512
SSD_DIM = 256
FOX_HEADS = 8
SC_WIDTH = 3
SSD_HEADS = 4
SSD_STATE = 64
SSD_CONV = 4
SSD_CHUNK = 128
XBC_DIM = 512
FFN_DIM = 2816
MEM_LEN = 256
MEM_HEADS = 4
MEM_HEAD_DIM = 128
MEM_DIM = 512
RMS_EPS = 1e-6
LOG2E = 1.4426950408889634

CUMSUM_BLOCK = 256
LANES = 128
SUBLANES = 8
VMEM_LIMIT = 56 * 1024 * 1024
ROW_TILE = 512
PAGES_PER_STEP = 32
XATTN_SEQS_PER_STEP = 8
FOX_HEADS_PER_STEP = 4
SSD_CHUNKS_PER_STEP = 4
SSD_SEQS_PER_STEP = 8
PROJ_QKV = 3 * CONV_DIM + 3 * ATTN_DIM
F_LANE0 = 0
DT_LANE0 = FOX_HEADS
PAD_DIM = FOX_HEADS * LANES
BIAS_LANE0 = HEAD_DIM
N_SPLIT = 3


def _params(sem):
    return pltpu.CompilerParams(dimension_semantics=sem, vmem_limit_bytes=VMEM_LIMIT)


def _resident(shape):
    nd = len(shape)
    return pl.BlockSpec(shape, lambda *_: (0,) * nd, pipeline_mode=pl.Buffered(1))


def _resident_layer(shape, layer):
    nd = len(shape)
    return pl.BlockSpec((None,) + tuple(shape), lambda *_: (layer,) + (0,) * nd, pipeline_mode=pl.Buffered(1))


def _rms(x, g):
    ms = jnp.mean(x * x, axis=-1, keepdims=True)
    return x * lax.rsqrt(ms + RMS_EPS) * g


def _head_rms(x, g, head_dim):
    width = x.shape[-1]
    pieces = []
    for c in range(width // LANES):
        blk = x[:, c * LANES:(c + 1) * LANES]
        sq = blk * blk
        s_all = jnp.sum(sq, axis=-1, keepdims=True)
        if head_dim == LANES:
            ms = s_all * (1.0 / LANES)
        else:
            lo = lax.broadcasted_iota(jnp.int32, sq.shape, 1) < head_dim
            s_lo = jnp.sum(jnp.where(lo, sq, 0.0), axis=-1, keepdims=True)
            ms = jnp.where(lo, s_lo, s_all - s_lo) * (1.0 / head_dim)
        pieces.append(blk * lax.rsqrt(ms + RMS_EPS))
    return jnp.concatenate(pieces, axis=-1) * g


def _silu(x):
    return x * jax.nn.sigmoid(x)


def _softplus(x):
    return jnp.maximum(x, 0.0) + jnp.log1p(jnp.exp(-jnp.abs(x)))


def _split3(x):
    hi = x.astype(BF16)
    r1 = x - hi.astype(F32)
    mid = r1.astype(BF16)
    lo = (r1 - mid.astype(F32)).astype(BF16)
    return hi, mid, lo


def _dot_exact_rhs(x, m_bf16):
    hi, mid, lo = _split3(x)
    out = jnp.dot(hi, m_bf16, preferred_element_type=F32)
    out = out + jnp.dot(mid, m_bf16, preferred_element_type=F32)
    return out + jnp.dot(lo, m_bf16, preferred_element_type=F32)


def _dot_exact_lhs(m_bf16, x):
    hi, mid, lo = _split3(x)
    out = jnp.dot(m_bf16, hi, preferred_element_type=F32)
    out = out + jnp.dot(m_bf16, mid, preferred_element_type=F32)
    return out + jnp.dot(m_bf16, lo, preferred_element_type=F32)


def _lower_tri(n, seq=None):
    r = lax.broadcasted_iota(jnp.int32, (n, n), 0)
    c = lax.broadcasted_iota(jnp.int32, (n, n), 1)
    keep = c <= r
    if seq is not None and seq < n:
        shift = jnp.int32(seq.bit_length() - 1)
        keep = keep & (lax.shift_right_logical(r, shift) == lax.shift_right_logical(c, shift))
    return keep.astype(BF16)


def _upper_tri(n):
    r = lax.broadcasted_iota(jnp.int32, (n, n), 0)
    c = lax.broadcasted_iota(jnp.int32, (n, n), 1)
    return (r <= c).astype(BF16)


def _nt(a, b):
    return lax.dot_general(a, b, (((1,), (1,)), ((), ())), preferred_element_type=F32)


def _tn(a, b):
    return lax.dot_general(a, b, (((0,), (0,)), ((), ())), preferred_element_type=F32)


def _row_tile(x, reps):
    return x if reps == 1 else jnp.concatenate([x] * reps, axis=0)


def _lane_tile(x, reps):
    return x if reps == 1 else jnp.concatenate([x] * reps, axis=-1)


def _ffn_body(x, g_ref, wg_ref, wu_ref, wd_ref, fg_ref, final):
    h = _rms(x, g_ref[...]).astype(BF16)
    a = jnp.dot(h, wg_ref[...], preferred_element_type=F32)
    b = jnp.dot(h, wu_ref[...], preferred_element_type=F32)
    t = (_silu(a) * b).astype(BF16)
    out = x + 0.5 * jnp.dot(t, wd_ref[...], preferred_element_type=F32)
    return _rms(out, fg_ref[...]) if final else out


def _ffn_kernel(x_ref, g_ref, wg_ref, wu_ref, wd_ref, fg_ref, o_ref, *, final):
    o_ref[...] = _ffn_body(x_ref[...], g_ref, wg_ref, wu_ref, wd_ref, fg_ref, final)


def _ffn(x, g, wg, wu, wd, fg, layer, final):
    n = x.shape[0]
    tm = min(ROW_TILE, n)
    return pl.pallas_call(
        functools.partial(_ffn_kernel, final=final),
        out_shape=jax.ShapeDtypeStruct((n, D_MODEL), F32),
        grid=(n // tm,),
        in_specs=[
            pl.BlockSpec((tm, D_MODEL), lambda i: (i, 0)),
            _resident((1, D_MODEL)),
            _resident_layer((D_MODEL, FFN_DIM), layer),
            _resident_layer((D_MODEL, FFN_DIM), layer),
            _resident_layer((FFN_DIM, D_MODEL), layer),
            _resident((1, D_MODEL)),
        ],
        out_specs=pl.BlockSpec((tm, D_MODEL), lambda i: (i, 0)),
        compiler_params=_params(("arbitrary",)),
        name="ffn",
    )(x, g, wg, wu, wd, fg)


def _expand_heads(x, fill):
    lo = lax.broadcasted_iota(jnp.int32, (x.shape[0], LANES), 1) < HEAD_DIM
    blocks = []
    for c in range(ATTN_DIM // LANES):
        blk = x[:, c * LANES:(c + 1) * LANES]
        for half, src in enumerate((blk, pltpu.roll(blk, HEAD_DIM, axis=1))):
            h = 2 * c + half
            blocks.append(jnp.where(lo, src, fill[:, h * LANES:(h + 1) * LANES]))
    return jnp.concatenate(blocks, axis=-1).astype(BF16)


def _inproj_kernel(x_ref, g_ref, w_ref, wzx_ref, wfd_ref, qg_ref, kg_ref, fb_ref, pm_ref, pc_ref,
                   scb_ref, cin_ref, qa_ref, kn_ref, ka_ref, v_ref, va_ref,
                   z_ref, xbc_ref, fdt_ref, *rest, tm, seq, dim_major):
    carry_ref = rest[-1]
    h = _rms(x_ref[...], g_ref[...]).astype(BF16)

    def store_kv(ref, val):
        ref[...] = val.T.reshape(FOX_HEADS, HEAD_DIM, tm) if dim_major else val

    def proj(a, b, ref=w_ref):
        return _nt(h, ref[a:b, :])

    o = 0
    scb_ref[...] = proj(o, o + CONV_DIM)
    o += CONV_DIM
    cin_ref[...] = proj(o, o + CONV_DIM) * proj(o + CONV_DIM, o + 2 * CONV_DIM)
    o += 2 * CONV_DIM
    q = _head_rms(proj(o, o + ATTN_DIM), qg_ref[...], HEAD_DIM) * (HEAD_DIM ** -0.5 * LOG2E)
    o += ATTN_DIM
    k = _head_rms(proj(o, o + ATTN_DIM), kg_ref[...], HEAD_DIM)
    store_kv(kn_ref, k)
    o += ATTN_DIM
    v = proj(o, o + ATTN_DIM)
    store_kv(v_ref, v)
    z_ref[...] = proj(0, SSD_DIM, wzx_ref)
    xbc_ref[...] = proj(SSD_DIM, SSD_DIM + XBC_DIM, wzx_ref)
    u = proj(0, LANES, wfd_ref) + fb_ref[...]
    lane = lax.broadcasted_iota(jnp.int32, u.shape, 1)
    fdt = jnp.where(lane < DT_LANE0, -_softplus(-u), _softplus(u))
    fdt_ref[...] = fdt
    if dim_major:
        rest[0][...] = fdt.T[0:FOX_HEADS, :]

    @pl.when(pl.program_id(0) % max(seq // tm, 1) == 0)
    def _():
        carry_ref[...] = jnp.zeros_like(carry_ref)

    lf = jnp.where(lane < FOX_HEADS, fdt, 0.0) * LOG2E
    blk = min(tm, CUMSUM_BLOCK)
    tri = _lower_tri(blk, seq)
    carry = carry_ref[0:1, :]
    c_parts = []
    for r in range(tm // blk):
        c_blk = _dot_exact_lhs(tri, lf[r * blk:(r + 1) * blk, :])
        if seq >= blk and (r == 0 or (r * blk) % seq != 0):
            c_blk = c_blk + carry
        carry = c_blk[blk - 1:blk, :]
        c_parts.append(c_blk)
    c = jnp.concatenate(c_parts, axis=0)
    carry_ref[...] = jnp.broadcast_to(carry, carry_ref.shape)
    hi, mid, lo = _split3(c)
    packed = (hi.astype(F32) + pltpu.roll(mid.astype(F32), FOX_HEADS, axis=1)
              + pltpu.roll(lo.astype(F32), 2 * FOX_HEADS, axis=1)).astype(BF16)
    both = jnp.dot(packed, pm_ref[...], preferred_element_type=F32)
    lane_blk = _lane_tile(lax.broadcasted_iota(jnp.int32, (tm, LANES), 1), FOX_HEADS)
    q_side = lane_blk < BIAS_LANE0 + N_SPLIT
    qa_ref[...] = _expand_heads(q, jnp.where(q_side, both, pc_ref[:, 0:PAD_DIM]))
    ka_ref[...] = _expand_heads(k, jnp.where(q_side, pc_ref[:, PAD_DIM:2 * PAD_DIM], both))
    va_ref[...] = _expand_heads(v, pc_ref[:, 2 * PAD_DIM:3 * PAD_DIM])


def _bias_placement():
    pm = np.zeros((LANES, PAD_DIM), np.float32)
    pc = np.zeros((1, 3 * PAD_DIM), np.float32)
    for h in range(FOX_HEADS):
        for part in range(N_SPLIT):
            src = part * FOX_HEADS + h
            pm[src, h * LANES + BIAS_LANE0 + part] = 1.0
            pm[src, h * LANES + BIAS_LANE0 + N_SPLIT + part] = -1.0
            pc[0, h * LANES + BIAS_LANE0 + N_SPLIT + part] = 1.0
            pc[0, PAD_DIM + h * LANES + BIAS_LANE0 + part] = 1.0
        pc[0, 2 * PAD_DIM + h * LANES + HEAD_DIM] = 1.0
    return jnp.asarray(pm, BF16), jnp.asarray(pc, F32)


def _inproj(x, g, w, qg, kg, fb, pm, pc, layer, seq):
    w_t, w_zx, w_fd = w
    n = x.shape[0]
    tm = min(ROW_TILE, n)
    widths = [(CONV_DIM, F32), (CONV_DIM, F32), (PAD_DIM, BF16), (ATTN_DIM, F32), (PAD_DIM, BF16),
              (ATTN_DIM, F32), (PAD_DIM, BF16), (SSD_DIM, F32), (XBC_DIM, F32), (LANES, F32)]
    out_shape = [jax.ShapeDtypeStruct((n, w_), dt) for w_, dt in widths]
    out_specs = [pl.BlockSpec((tm, w_), lambda i: (i, 0)) for w_, _ in widths]
    dim_major = seq >= tm
    if dim_major:
        tps = seq // tm
        kv_shape = jax.ShapeDtypeStruct((n // seq, FOX_HEADS, HEAD_DIM, seq), F32)
        kv_spec = pl.BlockSpec((None, FOX_HEADS, HEAD_DIM, tm), lambda i: (i // tps, 0, 0, i % tps))
        out_shape[3], out_shape[5] = kv_shape, kv_shape
        out_specs[3], out_specs[5] = kv_spec, kv_spec
        out_shape.append(jax.ShapeDtypeStruct((n // seq, FOX_HEADS, seq), F32))
        out_specs.append(pl.BlockSpec((None, FOX_HEADS, tm), lambda i: (i // tps, 0, i % tps)))
    return pl.pallas_call(
        functools.partial(_inproj_kernel, tm=tm, seq=seq, dim_major=dim_major),
        out_shape=out_shape,
        grid=(n // tm,),
        in_specs=[
            pl.BlockSpec((tm, D_MODEL), lambda i: (i, 0)),
            _resident((1, D_MODEL)),
            _resident_layer((PROJ_QKV, D_MODEL), layer),
            _resident_layer((SSD_DIM + XBC_DIM, D_MODEL), layer),
            _resident_layer((LANES, D_MODEL), layer),
            _resident((1, ATTN_DIM)),
            _resident((1, ATTN_DIM)),
            _resident((1, LANES)),
            _resident((LANES, PAD_DIM)),
            _resident((1, 3 * PAD_DIM)),
        ],
        out_specs=out_specs,
        scratch_shapes=[pltpu.VMEM((SUBLANES, LANES), F32)],
        compiler_params=_params(("arbitrary",)),
        name="inproj",
    )(x, g, w_t, w_zx, w_fd, qg, kg, fb, pm, pc)


def _fox_kernel(q_ref, k_ref, v_ref, o_ref, m_ref, acc_ref, *, tq):
    i = pl.program_id(2)
    m_ref[...] = jnp.full(m_ref.shape, -jnp.inf, F32)
    acc_ref[...] = jnp.zeros_like(acc_ref)

    def tile(start, width, masked):
        rows = pl.ds(pl.multiple_of(start, tq), width)
        for hh in range(FOX_HEADS_PER_STEP):
            sl = slice(hh * LANES, (hh + 1) * LANES)
            s = _nt(q_ref[:, sl], k_ref[rows, sl])
            if masked:
                row = lax.broadcasted_iota(jnp.int32, s.shape, 0)
                col = lax.broadcasted_iota(jnp.int32, s.shape, 1)
                s = jnp.where(col <= row + (width - tq), s, -jnp.inf)
            m_old = m_ref[hh]
            m_new = jnp.maximum(m_old, jnp.max(s, axis=-1, keepdims=True))
            p = jnp.exp2(s - _lane_tile(m_new, width // LANES))
            acc_ref[hh] = jnp.exp2(m_old - m_new) * acc_ref[hh] + jnp.dot(
                p.astype(BF16), v_ref[rows, sl], preferred_element_type=F32)
            m_ref[hh] = m_new

    def body(j, carry):
        tile(j * (2 * tq), 2 * tq, False)
        return carry

    odd = lax.bitwise_and(i, 1) == 1
    n_wide = lax.shift_right_logical(i, 1) - jnp.where(odd | (i == 0), 0, 1)
    lax.fori_loop(0, n_wide, body, 0)

    @pl.when(odd)
    def _():
        tile((i - 1) * tq, 2 * tq, True)

    @pl.when(jnp.logical_not(odd) & (i > 0))
    def _():
        tile((i - 2) * tq, 3 * tq, True)

    @pl.when(i == 0)
    def _():
        tile(0, tq, True)
    outs = []
    for hh in range(FOX_HEADS_PER_STEP):
        a = acc_ref[hh]
        outs.append(a[:, 0:HEAD_DIM] / a[:, HEAD_DIM:HEAD_DIM + 1])
    o_ref[...] = jnp.concatenate(outs, axis=-1)


def _fox_prompt(qa, ka, va, nseq, seq):
    tq = min(ROW_TILE, seq)
    nq = seq // tq
    nh = FOX_HEADS_PER_STEP
    return pl.pallas_call(
        functools.partial(_fox_kernel, tq=tq),
        out_shape=jax.ShapeDtypeStruct((nseq * seq, ATTN_DIM), F32),
        grid=(nseq, FOX_HEADS // nh, nq),
        in_specs=[
            pl.BlockSpec((tq, nh * LANES), lambda b, h, i: (b * nq + i, h)),
            pl.BlockSpec((seq, nh * LANES), lambda b, h, i: (b, h)),
            pl.BlockSpec((seq, nh * LANES), lambda b, h, i: (b, h)),
        ],
        out_specs=pl.BlockSpec((tq, nh * HEAD_DIM), lambda b, h, i: (b * nq + i, h)),
        scratch_shapes=[pltpu.VMEM((nh, tq, LANES), F32), pltpu.VMEM((nh, tq, LANES), F32)],
        compiler_params=_params(("arbitrary",) * 3),
        name="fox_prompt",
    )(qa, ka, va)


def _fox_paged_kernel(pt_ref, q_ref, kn_ref, vn_ref, lfn_ref, *rest, tokens):
    npg = PAGES_PER_STEP
    k_refs = rest[:npg]
    v_refs = rest[npg:2 * npg]
    lf_refs = rest[2 * npg:3 * npg]
    o_ref, m_ref, l_ref, acc_ref, carry_ref, c_ref = rest[3 * npg:]
    g = pl.program_id(1)
    rows = FOX_HEADS * tokens

    @pl.when(g == 0)
    def _():
        m_ref[...] = jnp.full(m_ref.shape, -jnp.inf, F32)
        l_ref[...] = jnp.zeros_like(l_ref)
        acc_ref[...] = jnp.zeros_like(acc_ref)
        carry_ref[...] = jnp.zeros_like(carry_ref)

    q = q_ref[...]
    tri = _upper_tri(PAGE_SIZE)

    def update(k_list, v_list, lf_list):
        n = len(k_list)
        nr = n * FOX_HEADS
        parts = _split3(jnp.concatenate(lf_list, axis=0) * LOG2E)

        def times(m_bf16):
            return sum(jnp.dot(x, m_bf16, preferred_element_type=F32) for x in parts)

        c = times(tri) + _row_tile(carry_ref[...], n)
        if n > 1:
            tot = times(jnp.ones((PAGE_SIZE, PAGE_SIZE), BF16))
            r = lax.broadcasted_iota(jnp.int32, (nr, nr), 0)
            col = lax.broadcasted_iota(jnp.int32, (nr, nr), 1)
            same_head = lax.bitwise_and(r, FOX_HEADS - 1) == lax.bitwise_and(col, FOX_HEADS - 1)
            earlier = jnp.where(col < r, jnp.where(same_head, 1.0, 0.0), 0.0).astype(BF16)
            off = _dot_exact_lhs(earlier, tot)
            c = c + off
            carry_ref[...] = carry_ref[...] + off[nr - FOX_HEADS:nr, :] + tot[nr - FOX_HEADS:nr, :]
        c_ref[0:nr, :] = c
        s_list = []
        for idx in range(n):
            c_rows = jnp.concatenate(
                [jnp.broadcast_to(c_ref[idx * FOX_HEADS + h:idx * FOX_HEADS + h + 1, :], (tokens, PAGE_SIZE))
                 for h in range(FOX_HEADS)], axis=0)
            s_list.append(jnp.dot(q, k_list[idx].astype(BF16), preferred_element_type=F32) - c_rows)
        s = jnp.concatenate(s_list, axis=-1)
        m_old = m_ref[...]
        m_new = jnp.maximum(m_old, jnp.max(s, axis=-1, keepdims=True))
        alpha = jnp.exp2(m_old - m_new)
        p = jnp.exp2(s - _lane_tile(m_new, n))
        l_ref[...] = alpha * l_ref[...] + jnp.sum(p, axis=-1, keepdims=True)
        pv = _nt(p[:, 0:PAGE_SIZE].astype(BF16), v_list[0].astype(BF16))
        for idx in range(1, n):
            pv = pv + _nt(p[:, idx * PAGE_SIZE:(idx + 1) * PAGE_SIZE].astype(BF16), v_list[idx].astype(BF16))
        acc_ref[...] = _lane_tile(alpha, ATTN_DIM // LANES) * acc_ref[...] + pv
        m_ref[...] = m_new

    update([r[...] for r in k_refs], [r[...] for r in v_refs], [r[...] for r in lf_refs])

    @pl.when(g == pl.num_programs(1) - 1)
    def _():
        c_new = _dot_exact_rhs(lfn_ref[...] * LOG2E, _upper_tri(tokens)) + carry_ref[:, 0:tokens]
        c_ref[0:FOX_HEADS, 0:tokens] = c_new
        c_rows = jnp.concatenate(
            [jnp.broadcast_to(c_ref[h:h + 1, 0:tokens], (tokens, tokens)) for h in range(FOX_HEADS)], axis=0)
        s = _nt(q, kn_ref[...].astype(BF16)) - c_rows
        t_idx = lax.bitwise_and(lax.broadcasted_iota(jnp.int32, s.shape, 0), tokens - 1)
        s = jnp.where(lax.broadcasted_iota(jnp.int32, s.shape, 1) <= t_idx, s, -jnp.inf)
        m_old = m_ref[...]
        m_new = jnp.maximum(m_old, jnp.max(s, axis=-1, keepdims=True))
        alpha = jnp.exp2(m_old - m_new)
        p = jnp.exp2(s - m_new[:, 0:tokens])
        l_new = alpha * l_ref[...] + jnp.sum(p, axis=-1, keepdims=True)
        acc = _lane_tile(alpha, ATTN_DIM // LANES) * acc_ref[...] + jnp.dot(
            p.astype(BF16), vn_ref[...].astype(BF16), preferred_element_type=F32)
        full = acc / _lane_tile(l_new, ATTN_DIM // LANES)
        lane_head = lax.shift_right_logical(
            lax.broadcasted_iota(jnp.int32, (tokens, ATTN_DIM), 1), jnp.int32(HEAD_DIM.bit_length() - 1))
        out = jnp.zeros((tokens, ATTN_DIM), F32)
        for h in range(FOX_HEADS):
            out = out + jnp.where(lane_head == h, full[h * tokens:(h + 1) * tokens, :], 0.0)
        o_ref[...] = out


def _fox_paged(layer, page_table, qbd, kn_t, vn_t, lfn_t, cache_kt, cache_vt, cache_lf_t, nseq, tokens):
    npg = PAGES_PER_STEP
    n_groups = page_table.shape[1] // npg
    rows = FOX_HEADS * tokens

    def page_spec(width2, n):
        return pl.BlockSpec((None, None, width2, PAGE_SIZE),
                            lambda b, g, pt: (layer, pt[b, g * npg + n], 0, 0))

    in_specs = [
        pl.BlockSpec((None, rows, ATTN_DIM), lambda b, g, pt: (b, 0, 0)),
        pl.BlockSpec((None, tokens, ATTN_DIM), lambda b, g, pt: (b, 0, 0)),
        pl.BlockSpec((None, tokens, ATTN_DIM), lambda b, g, pt: (b, 0, 0)),
        pl.BlockSpec((None, FOX_HEADS, tokens), lambda b, g, pt: (b, 0, 0)),
    ]
    in_specs += [page_spec(ATTN_DIM, n) for n in range(npg)]
    in_specs += [page_spec(ATTN_DIM, n) for n in range(npg)]
    in_specs += [page_spec(FOX_HEADS, n) for n in range(npg)]
    grid_spec = pltpu.PrefetchScalarGridSpec(
        num_scalar_prefetch=1,
        grid=(nseq, n_groups),
        in_specs=in_specs,
        out_specs=pl.BlockSpec((None, tokens, ATTN_DIM), lambda b, g, pt: (b, 0, 0)),
        scratch_shapes=[pltpu.VMEM((rows, LANES), F32), pltpu.VMEM((rows, LANES), F32),
                        pltpu.VMEM((rows, ATTN_DIM), F32), pltpu.VMEM((FOX_HEADS, LANES), F32),
                        pltpu.VMEM((npg * FOX_HEADS, LANES), F32)],
    )
    return pl.pallas_call(
        functools.partial(_fox_paged_kernel, tokens=tokens),
        out_shape=jax.ShapeDtypeStruct((nseq, tokens, ATTN_DIM), F32),
        grid_spec=grid_spec,
        compiler_params=_params(("arbitrary", "arbitrary")),
        name="fox_paged",
    )(page_table, qbd, kn_t, vn_t, lfn_t,
      *([cache_kt] * npg), *([cache_vt] * npg), *([cache_lf_t] * npg))


def _ssd_kernel(xbc_ref, halo_ref, st_ref, fdt_ref, z_ref, h0_ref,
                cw_ref, cb_ref, av_ref, dv_ref,
                y_ref, hout_ref, buf_ref, h_ref, *, lc, rows):
    j = pl.program_id(1)
    bs = st_ref.shape[0]

    @pl.when(j == 0)
    def _():
        h_ref[...] = h0_ref[...]
        buf_ref[:, 0:SUBLANES, :] = st_ref[...]

    @pl.when(j > 0)
    def _():
        buf_ref[0, 0:SUBLANES, :] = halo_ref[...]

    buf_ref[:, SUBLANES:SUBLANES + rows, :] = xbc_ref[...].reshape(bs, rows, XBC_DIM)
    base = SUBLANES - (SSD_CONV - 1)
    u_all = buf_ref[:, base:base + rows, :] * cw_ref[0:1, :]
    for k in range(1, SSD_CONV):
        u_all = u_all + buf_ref[:, base + k:base + k + rows, :] * cw_ref[k:k + 1, :]
    u_all = _silu(u_all + cb_ref[...]).reshape(bs * rows, XBC_DIM)

    tri_lo = _lower_tri(lc)
    tri_up = _upper_tri(lc)
    pick = (lax.broadcasted_iota(jnp.int32, (SUBLANES, LANES), 1)
            == lax.broadcasted_iota(jnp.int32, (SUBLANES, LANES), 0) + DT_LANE0).astype(BF16)

    def rows_of(x):
        return sum(_nt(pick, part) for part in _split3(x))

    row = lax.broadcasted_iota(jnp.int32, (lc, lc), 0)
    col = lax.broadcasted_iota(jnp.int32, (lc, lc), 1)
    causal = col <= row
    dvec = dv_ref[...]

    for s, c in ((s, c) for s in range(bs) for c in range(rows // lc)):
        tok = slice(s * rows + c * lc, s * rows + (c + 1) * lc)
        u = u_all[tok, :]
        z = z_ref[tok, :]
        fdt = fdt_ref[tok, :]
        da = fdt * av_ref[...]
        acum = _dot_exact_lhs(tri_lo, da)
        dtt = rows_of(fdt)
        acum_t = _dot_exact_rhs(rows_of(da), tri_up)
        outs = []
        for h in range(SSD_HEADS):
            grp = h // (SSD_HEADS // 2)
            xh = u[:, h * HEAD_DIM:(h + 1) * HEAD_DIM]
            bh = u[:, SSD_DIM + grp * SSD_STATE:SSD_DIM + (grp + 1) * SSD_STATE]
            ch = u[:, SSD_DIM + 2 * SSD_STATE + grp * SSD_STATE:SSD_DIM + 2 * SSD_STATE + (grp + 1) * SSD_STATE]
            lane = DT_LANE0 + h
            a_col = acum[:, lane:lane + 1]
            dt_col = fdt[:, lane:lane + 1]
            a_row = acum_t[h:h + 1, :]
            dt_row = dtt[h:h + 1, :]
            a_last = acum[lc - 1:lc, lane:lane + 1]
            decay = jnp.exp(jnp.where(causal, a_col - a_row, -jnp.inf))
            xb = xh.astype(BF16)
            bb = bh.astype(BF16)
            cb16 = ch.astype(BF16)
            w = _nt(cb16, bb) * decay * dt_row
            h_prev = h_ref[s, h]
            y = jnp.dot(w.astype(BF16), xb, preferred_element_type=F32)
            y = y + _nt(cb16, h_prev.astype(BF16)) * jnp.exp(a_col)
            w_end = jnp.exp(a_last - a_col) * dt_col
            h_ref[s, h] = h_prev * jnp.exp(a_last) + _tn((xh * w_end).astype(BF16), bb)
            sl = slice(h * HEAD_DIM, (h + 1) * HEAD_DIM)
            outs.append((y + dvec[:, sl] * xh) * _silu(z[:, sl]))
        y_ref[tok, :] = jnp.concatenate(outs, axis=-1)

    @pl.when(j == pl.num_programs(1) - 1)
    def _():
        hout_ref[...] = h_ref[...]


def _ssd(xbc, st_pad, fdt, z, h0, cw, cb, av, dv, nseq, seq):
    lc = min(SSD_CHUNK, seq)
    rows = min(SSD_CHUNKS_PER_STEP * lc, seq)
    nc = seq // rows
    bs = 1 if nc > 1 else min(nseq, SSD_SEQS_PER_STEP)
    nrows = bs * rows
    hb = nrows // SUBLANES
    state_spec = pl.BlockSpec((bs, SSD_HEADS, HEAD_DIM, SSD_STATE), lambda b, j: (b, 0, 0, 0))
    return pl.pallas_call(
        functools.partial(_ssd_kernel, lc=lc, rows=rows),
        out_shape=[jax.ShapeDtypeStruct((nseq * seq, SSD_DIM), F32),
                   jax.ShapeDtypeStruct((nseq, SSD_HEADS, HEAD_DIM, SSD_STATE), F32)],
        grid=(nseq // bs, nc),
        in_specs=[
            pl.BlockSpec((nrows, XBC_DIM), lambda b, j: (b * nc + j, 0)),
            pl.BlockSpec((SUBLANES, XBC_DIM), lambda b, j: (jnp.maximum((b * nc + j) * hb - 1, 0), 0)),
            pl.BlockSpec((bs, SUBLANES, XBC_DIM), lambda b, j: (b, 0, 0)),
            pl.BlockSpec((nrows, LANES), lambda b, j: (b * nc + j, 0)),
            pl.BlockSpec((nrows, SSD_DIM), lambda b, j: (b * nc + j, 0)),
            state_spec,
            _resident((SSD_CONV, XBC_DIM)),
            _resident((1, XBC_DIM)),
            _resident((1, LANES)),
            _resident((1, SSD_DIM)),
        ],
        out_specs=[pl.BlockSpec((nrows, SSD_DIM), lambda b, j: (b * nc + j, 0)), state_spec],
        scratch_shapes=[pltpu.VMEM((bs, SUBLANES + rows, XBC_DIM), F32),
                        pltpu.VMEM((bs, SSD_HEADS, HEAD_DIM, SSD_STATE), F32)],
        compiler_params=_params(("arbitrary", "arbitrary")),
        name="ssd",
    )(xbc, xbc, st_pad, fdt, z, h0, cw, cb, av, dv)


def _merge_body(x_ref, scb_ref, cin_ref, halo_ref, st_ref, attn_ref, yssd_ref,
                cw_ref, g_ref, w_ref, buf_ref, tl):
    j = pl.program_id(1)
    bs = st_ref.shape[0]

    @pl.when(j == 0)
    def _():
        buf_ref[:, 0:SUBLANES, :] = st_ref[...]

    @pl.when(j > 0)
    def _():
        buf_ref[0, 0:SUBLANES, :] = halo_ref[...]

    buf_ref[:, SUBLANES:SUBLANES + tl, :] = cin_ref[...].reshape(bs, tl, CONV_DIM)
    base = SUBLANES - (SC_WIDTH - 1)
    y = buf_ref[:, base:base + tl, :] * cw_ref[0:1, :]
    for k in range(1, SC_WIDTH):
        y = y + buf_ref[:, base + k:base + k + tl, :] * cw_ref[k:k + 1, :]
    y = y.reshape(bs * tl, CONV_DIM)
    cat = jnp.concatenate([scb_ref[...] * y, attn_ref[...], yssd_ref[...]], axis=-1)
    cat = _head_rms(cat, g_ref[...], HEAD_DIM).astype(BF16)
    return x_ref[...] + jnp.dot(cat, w_ref[...], preferred_element_type=F32)


def _merge_kernel(x_ref, scb_ref, cin_ref, halo_ref, st_ref, attn_ref, yssd_ref,
                  cw_ref, g_ref, w_ref, o_ref, buf_ref, *, tl):
    o_ref[...] = _merge_body(x_ref, scb_ref, cin_ref, halo_ref, st_ref, attn_ref, yssd_ref,
                             cw_ref, g_ref, w_ref, buf_ref, tl)


def _merge(x, scb, cin, st_pad, attn, yssd, cw, g, w, layer, nseq, seq):
    tl = min(ROW_TILE, seq)
    nt = seq // tl
    bs = 1 if nt > 1 else min(nseq, ROW_TILE // tl)
    nrows = bs * tl
    hb = nrows // SUBLANES

    def rows(width):
        return pl.BlockSpec((nrows, width), lambda b, j: (b * nt + j, 0))

    return pl.pallas_call(
        functools.partial(_merge_kernel, tl=tl),
        out_shape=jax.ShapeDtypeStruct((nseq * seq, D_MODEL), F32),
        grid=(nseq // bs, nt),
        in_specs=[
            rows(D_MODEL), rows(CONV_DIM), rows(CONV_DIM),
            pl.BlockSpec((SUBLANES, CONV_DIM), lambda b, j: (jnp.maximum((b * nt + j) * hb - 1, 0), 0)),
            pl.BlockSpec((bs, SUBLANES, CONV_DIM), lambda b, j: (b, 0, 0)),
            rows(ATTN_DIM), rows(SSD_DIM),
            _resident((SC_WIDTH, CONV_DIM)),
            _resident((1, D_MODEL)),
            _resident_layer((D_MODEL, D_MODEL), layer),
        ],
        out_specs=rows(D_MODEL),
        scratch_shapes=[pltpu.VMEM((bs, SUBLANES + tl, CONV_DIM), F32)],
        compiler_params=_params(("arbitrary", "arbitrary")),
        name="merge",
    )(x, scb, cin, cin, st_pad, attn, yssd, cw, g, w)


def _memkv_kernel(mem_ref, g_ref, wk_ref, wv_ref, kg_ref, km_ref, vm_ref):
    m = _rms(mem_ref[...], g_ref[...]).astype(BF16)
    km_ref[...] = _head_rms(jnp.dot(m, wk_ref[...], preferred_element_type=F32), kg_ref[...], MEM_HEAD_DIM)
    vm_ref[...] = jnp.dot(m, wv_ref[...], preferred_element_type=F32)


def _memkv(mem, g, wk, wv, kg):
    nb = mem.shape[0]
    out = jax.ShapeDtypeStruct((DEPTH, nb, MEM_LEN, MEM_DIM), F32)
    return pl.pallas_call(
        _memkv_kernel,
        out_shape=[out, out],
        grid=(DEPTH, nb),
        in_specs=[
            pl.BlockSpec((None, MEM_LEN, D_MODEL), lambda l, b: (b, 0, 0)),
            pl.BlockSpec((None, 1, D_MODEL), lambda l, b: (l, 0, 0)),
            pl.BlockSpec((None, D_MODEL, MEM_DIM), lambda l, b: (l, 0, 0)),
            pl.BlockSpec((None, D_MODEL, MEM_DIM), lambda l, b: (l, 0, 0)),
            pl.BlockSpec((None, 1, MEM_DIM), lambda l, b: (l, 0, 0)),
        ],
        out_specs=[pl.BlockSpec((None, None, MEM_LEN, MEM_DIM), lambda l, b: (l, b, 0, 0))] * 2,
        compiler_params=_params(("arbitrary", "arbitrary")),
        name="mem_kv",
    )(mem, g, wk, wv, kg)


def _xattn_body(x, km_ref, vm_ref, g_ref, wq_ref, qg_ref, wo_ref, q_ref, att_ref, bs, tl):
    h = _rms(x, g_ref[...]).astype(BF16)
    q = _head_rms(jnp.dot(h, wq_ref[...], preferred_element_type=F32), qg_ref[...], MEM_HEAD_DIM)
    q = q * (MEM_HEAD_DIM ** -0.5)
    for hh in range(MEM_HEADS):
        q_ref[hh] = q[:, hh * MEM_HEAD_DIM:(hh + 1) * MEM_HEAD_DIM]

    def softmax(sc):
        p = jnp.exp(sc - jnp.max(sc, axis=-1, keepdims=True))
        return (p / jnp.sum(p, axis=-1, keepdims=True)).astype(BF16)

    for s in range(bs):
        rows = slice(s * tl, (s + 1) * tl)
        if len(km_ref.shape) == 4:
            km = km_ref[s].reshape(MEM_LEN * MEM_HEADS, MEM_HEAD_DIM).astype(BF16)
            vm = vm_ref[s].reshape(MEM_LEN * MEM_HEADS, MEM_HEAD_DIM).astype(BF16)
            qs = jnp.concatenate([q_ref[hh, rows, :] for hh in range(MEM_HEADS)], axis=0).astype(BF16)
            sc = _nt(qs, km)
            row_head = lax.shift_right_logical(lax.broadcasted_iota(jnp.int32, sc.shape, 0),
                                               jnp.int32(tl.bit_length() - 1))
            col_head = lax.bitwise_and(lax.broadcasted_iota(jnp.int32, sc.shape, 1), MEM_HEADS - 1)
            out = jnp.dot(softmax(jnp.where(row_head == col_head, sc, -jnp.inf)), vm,
                          preferred_element_type=F32)
            for hh in range(MEM_HEADS):
                att_ref[hh, rows, :] = out[hh * tl:(hh + 1) * tl, :]
        else:
            for hh in range(MEM_HEADS):
                sl = slice(hh * MEM_HEAD_DIM, (hh + 1) * MEM_HEAD_DIM)
                sc = _nt(q_ref[hh, rows, :].astype(BF16), km_ref[s, :, sl].astype(BF16))
                att_ref[hh, rows, :] = jnp.dot(softmax(sc), vm_ref[s, :, sl].astype(BF16),
                                               preferred_element_type=F32)

    att = jnp.concatenate([att_ref[hh] for hh in range(MEM_HEADS)], axis=-1).astype(BF16)
    return x + jnp.dot(att, wo_ref[...], preferred_element_type=F32)


def _xattn_kernel(x_ref, km_ref, vm_ref, g_ref, wq_ref, qg_ref, wo_ref, o_ref, q_ref, att_ref, *, bs, tl):
    o_ref[...] = _xattn_body(x_ref[...], km_ref, vm_ref, g_ref, wq_ref, qg_ref, wo_ref, q_ref, att_ref, bs, tl)


def _xattn(x, km, vm, layer, g, wq, qg, wo, nseq, seq):
    tl = min(ROW_TILE, seq)
    nt = seq // tl
    bs = 1 if nt > 1 else min(nseq, XATTN_SEQS_PER_STEP)
    rows = bs * tl
    mem_zeros = (0,) * (km.ndim - 2)
    mem_spec = pl.BlockSpec((None, bs) + tuple(km.shape[2:]), lambda b, j: (layer, b) + mem_zeros)
    row_spec = pl.BlockSpec((rows, D_MODEL), lambda b, j: (b * nt + j, 0))
    return pl.pallas_call(
        functools.partial(_xattn_kernel, bs=bs, tl=tl),
        out_shape=jax.ShapeDtypeStruct((nseq * seq, D_MODEL), F32),
        grid=(nseq // bs, nt),
        in_specs=[
            row_spec, mem_spec, mem_spec,
            _resident((1, D_MODEL)),
            _resident_layer((D_MODEL, MEM_DIM), layer),
            _resident((1, MEM_DIM)),
            _resident_layer((MEM_DIM, D_MODEL), layer),
        ],
        out_specs=row_spec,
        scratch_shapes=[pltpu.VMEM((MEM_HEADS, rows, MEM_HEAD_DIM), F32),
                        pltpu.VMEM((MEM_HEADS, rows, MEM_HEAD_DIM), F32)],
        compiler_params=_params(("arbitrary", "arbitrary")),
        name="xattn",
    )(x, km, vm, g, wq, qg, wo)


def _merge_xattn_kernel(x_ref, scb_ref, cin_ref, halo_ref, st_ref, attn_ref, yssd_ref, cw_ref, go_ref,
                        wout_ref, km_ref, vm_ref, gx_ref, wq_ref, qg_ref, wo_ref,
                        o_ref, buf_ref, q_ref, att_ref, *, tl):
    x = _merge_body(x_ref, scb_ref, cin_ref, halo_ref, st_ref, attn_ref, yssd_ref,
                    cw_ref, go_ref, wout_ref, buf_ref, tl)
    o_ref[...] = _xattn_body(x, km_ref, vm_ref, gx_ref, wq_ref, qg_ref, wo_ref, q_ref, att_ref, 1, tl)


def _merge_xattn(x, scb, cin, st_pad, attn, yssd, cw, g_out, w_out, km, vm, g_xa, wq, qg, wo,
                 layer, nseq, seq):
    tl = ROW_TILE
    nt = seq // tl
    hb = tl // SUBLANES

    def rows(width):
        return pl.BlockSpec((tl, width), lambda b, j: (b * nt + j, 0))

    mem_spec = pl.BlockSpec((None, 1, MEM_LEN, MEM_DIM), lambda b, j: (layer, b, 0, 0))
    return pl.pallas_call(
        functools.partial(_merge_xattn_kernel, tl=tl),
        out_shape=jax.ShapeDtypeStruct((nseq * seq, D_MODEL), F32),
        grid=(nseq, nt),
        in_specs=[
            rows(D_MODEL), rows(CONV_DIM), rows(CONV_DIM),
            pl.BlockSpec((SUBLANES, CONV_DIM), lambda b, j: (jnp.maximum((b * nt + j) * hb - 1, 0), 0)),
            pl.BlockSpec((1, SUBLANES, CONV_DIM), lambda b, j: (b, 0, 0)),
            rows(ATTN_DIM), rows(SSD_DIM),
            _resident((SC_WIDTH, CONV_DIM)),
            _resident((1, D_MODEL)),
            _resident_layer((D_MODEL, D_MODEL), layer),
            mem_spec, mem_spec,
            _resident((1, D_MODEL)),
            _resident_layer((D_MODEL, MEM_DIM), layer),
            _resident((1, MEM_DIM)),
            _resident_layer((MEM_DIM, D_MODEL), layer),
        ],
        out_specs=rows(D_MODEL),
        scratch_shapes=[pltpu.VMEM((1, SUBLANES + tl, CONV_DIM), F32),
                        pltpu.VMEM((MEM_HEADS, tl, MEM_HEAD_DIM), F32),
                        pltpu.VMEM((MEM_HEADS, tl, MEM_HEAD_DIM), F32)],
        compiler_params=_params(("arbitrary", "arbitrary")),
        name="merge_xattn",
    )(x, scb, cin, cin, st_pad, attn, yssd, cw, g_out, w_out, km, vm, g_xa, wq, qg, wo)


def _pad_state(state, width):
    nseq, _, c = state.shape
    return jnp.concatenate([jnp.zeros((nseq, SUBLANES - (width - 1), c), F32), state], axis=1)


def _prep_w_in(w_in):
    w_t = jnp.transpose(w_in, (0, 2, 1)).astype(BF16)
    n_z0 = PROJ_QKV + FOX_HEADS
    n_dt0 = n_z0 + SSD_DIM + XBC_DIM
    pad = jnp.zeros((DEPTH, LANES - FOX_HEADS - SSD_HEADS, D_MODEL), BF16)
    w_fd = jnp.concatenate([w_t[:, PROJ_QKV:n_z0], w_t[:, n_dt0:], pad], axis=1)
    return w_t, w_t[:, n_z0:n_dt0], w_fd


def _layer_params(l, p):
    pad = jnp.zeros((LANES - FOX_HEADS - SSD_HEADS,), F32)
    a_neg = -jnp.exp(p['ssd_A_log'][l])
    return dict(
        fb=jnp.concatenate([p['fox_f_bias'][l], p['ssd_dt_bias'][l], pad])[None, :],
        qg=jnp.tile(p['fox_q_norm'][l], FOX_HEADS)[None, :],
        kg=jnp.tile(p['fox_k_norm'][l], FOX_HEADS)[None, :],
        av=jnp.concatenate([jnp.zeros((FOX_HEADS,), F32), a_neg, pad])[None, :],
        dv=jnp.repeat(p['ssd_D'][l], HEAD_DIM)[None, :],
        xa_qg=jnp.tile(p['xa_q_norm'][l], MEM_HEADS)[None, :],
    )


def _token_mix(l, x, p, lp, nseq, seq, conv_state, ssd_conv_state, ssd_state, paged, finish):
    n = nseq * seq
    scb, cin, qa, kn, ka, v, va, z, xbc, fdt, *lf_t = _inproj(
        x, p['mix_norm'][l][None, :], p['w_in_t'], lp['qg'], lp['kg'], lp['fb'], p['pm'], p['pc'], l, seq)

    if paged is None:
        attn = _fox_prompt(qa, ka, va, nseq, seq)
        k_out = jnp.transpose(kn, (0, 3, 1, 2))
        v_out = jnp.transpose(v, (0, 3, 1, 2))
        logf = jnp.transpose(lf_t[0], (0, 2, 1))
    else:
        logf = fdt[:, F_LANE0:F_LANE0 + FOX_HEADS].reshape(nseq, seq, FOX_HEADS)
        k_out = kn.reshape(nseq, seq, FOX_HEADS, HEAD_DIM)
        v_out = v.reshape(nseq, seq, FOX_HEADS, HEAD_DIM)
        page_table, cache_kt, cache_vt, cache_lf_t = paged
        eye = jnp.eye(FOX_HEADS, dtype=BF16)
        q4 = jnp.transpose(qa.reshape(nseq, seq, FOX_HEADS, LANES)[..., :HEAD_DIM], (0, 2, 1, 3))
        qbd = (q4[:, :, :, None, :] * eye[None, :, None, :, None]).reshape(nseq, FOX_HEADS * seq, ATTN_DIM)
        attn = _fox_paged(l, page_table, qbd, kn.reshape(nseq, seq, ATTN_DIM), v.reshape(nseq, seq, ATTN_DIM),
                          jnp.transpose(logf, (0, 2, 1)), cache_kt, cache_vt, cache_lf_t,
                          nseq, seq).reshape(n, ATTN_DIM)

    yssd, h_end = _ssd(xbc, _pad_state(ssd_conv_state, SSD_CONV), fdt, z, ssd_state, p['ssd_conv_w'][l],
                       p['ssd_conv_b'][l][None, :], lp['av'], lp['dv'], nseq, seq)
    x = finish(x, scb, cin, _pad_state(conv_state, SC_WIDTH), attn, yssd)

    cin3 = cin.reshape(nseq, seq, CONV_DIM)
    xbc3 = xbc.reshape(nseq, seq, XBC_DIM)
    new_conv = jnp.concatenate([conv_state, cin3], axis=1)[:, -(SC_WIDTH - 1):]
    new_ssd_conv = jnp.concatenate([ssd_conv_state, xbc3], axis=1)[:, -(SSD_CONV - 1):]
    state = (k_out, v_out, logf, new_conv, new_ssd_conv, h_end)
    return x, state


def kernel(x_prompt, x_sample, cache_fox_k, cache_fox_v, cache_fox_logf, cache_mem_k, cache_mem_v,
           state_conv, state_ssd_conv, state_ssd, page_table, mem_prompt,
           ffn1_norm, ffn1_wg, ffn1_wu, ffn1_wd, mix_norm, w_in, sc_conv_w, fox_q_norm, fox_k_norm,
           fox_f_bias, ssd_conv_w, ssd_conv_b, ssd_dt_bias, ssd_A_log, ssd_D, mix_out_norm, w_out,
           xa_norm, mem_norm, xa_wq, xa_wk, xa_wv, xa_q_norm, xa_k_norm, xa_wo,
           ffn2_norm, ffn2_wg, ffn2_wu, ffn2_wd, final_norm):
    b_p, s_p, _ = x_prompt.shape
    b_s, s_s, _ = x_sample.shape
    n_phys = cache_fox_k.shape[1]
    pm, pc = _bias_placement()
    p = dict(mix_norm=mix_norm, w_in_t=_prep_w_in(w_in), sc_conv_w=sc_conv_w, fox_q_norm=fox_q_norm,
             fox_k_norm=fox_k_norm, fox_f_bias=fox_f_bias, ssd_conv_w=ssd_conv_w, ssd_conv_b=ssd_conv_b,
             ssd_dt_bias=ssd_dt_bias, ssd_A_log=ssd_A_log, ssd_D=ssd_D, mix_out_norm=mix_out_norm,
             xa_q_norm=xa_q_norm, w_out_bf=w_out.astype(BF16), pm=pm, pc=pc)
    wg1, wu1, wd1 = ffn1_wg.astype(BF16), ffn1_wu.astype(BF16), ffn1_wd.astype(BF16)
    wg2, wu2, wd2 = ffn2_wg.astype(BF16), ffn2_wu.astype(BF16), ffn2_wd.astype(BF16)
    wq, wo = xa_wq.astype(BF16), xa_wo.astype(BF16)
    fg = final_norm[None, :]

    km_p, vm_p = _memkv(mem_prompt, mem_norm[:, None, :], xa_wk.astype(BF16), xa_wv.astype(BF16),
                        jnp.tile(xa_k_norm, (1, MEM_HEADS))[:, None, :])
    km_s, vm_s = cache_mem_k, cache_mem_v
    cache_kt = jnp.transpose(cache_fox_k, (0, 1, 3, 4, 2)).reshape(DEPTH, n_phys, ATTN_DIM, PAGE_SIZE)
    cache_vt = jnp.transpose(cache_fox_v, (0, 1, 3, 4, 2)).reshape(DEPTH, n_phys, ATTN_DIM, PAGE_SIZE)
    cache_lf_t = jnp.transpose(cache_fox_logf, (0, 1, 3, 2))

    zero_conv = jnp.zeros((b_p, SC_WIDTH - 1, CONV_DIM), F32)
    zero_ssd_conv = jnp.zeros((b_p, SSD_CONV - 1, XBC_DIM), F32)
    zero_ssd = jnp.zeros((b_p, SSD_HEADS, HEAD_DIM, SSD_STATE), F32)

    xp = x_prompt.reshape(b_p * s_p, D_MODEL)
    xs = x_sample.reshape(b_s * s_s, D_MODEL)
    st_p, st_s = [], []
    for l in range(DEPTH):
        lp = _layer_params(l, p)
        last = l == DEPTH - 1
        groups = []
        for x, nseq, seq, km, vm, cs, scs, ss, paged in (
                (xp, b_p, s_p, km_p, vm_p, zero_conv, zero_ssd_conv, zero_ssd, None),
                (xs, b_s, s_s, km_s, vm_s, state_conv[l], state_ssd_conv[l], state_ssd[l],
                 (page_table, cache_kt, cache_vt, cache_lf_t))):
            def finish(x, scb, cin, conv_pad, attn, yssd, nseq=nseq, seq=seq, km=km, vm=vm):
                mix = (p['sc_conv_w'][l], p['mix_out_norm'][l][None, :], p['w_out_bf'])
                xa = (xa_norm[l][None, :], wq, lp['xa_qg'], wo)
                f2 = (ffn2_norm[l][None, :], wg2, wu2, wd2, fg)
                if seq % ROW_TILE == 0:
                    x = _merge_xattn(x, scb, cin, conv_pad, attn, yssd, *mix, km, vm, *xa, l, nseq, seq)
                else:
                    x = _merge(x, scb, cin, conv_pad, attn, yssd, *mix, l, nseq, seq)
                    x = _xattn(x, km, vm, l, *xa, nseq, seq)
                return _ffn(x, *f2, l, last)

            x = _ffn(x, ffn1_norm[l][None, :], wg1, wu1, wd1, fg, l, False)
            x, st = _token_mix(l, x, p, lp, nseq, seq, cs, scs, ss, paged, finish)
            groups.append((x, st))
        (xp, sp), (xs, ss_) = groups
        st_p.append(sp)
        st_s.append(ss_)

    def stack(states, idx):
        return jnp.stack([s[idx] for s in states])

    return (xp.reshape(b_p, s_p, D_MODEL), xs.reshape(b_s, s_s, D_MODEL),
            stack(st_p, 0), stack(st_p, 1), stack(st_p, 2),
            km_p.reshape(DEPTH, b_p, MEM_LEN, MEM_HEADS, MEM_HEAD_DIM),
            vm_p.reshape(DEPTH, b_p, MEM_LEN, MEM_HEADS, MEM_HEAD_DIM),
            stack(st_p, 3), stack(st_p, 4), stack(st_p, 5),
            stack(st_s, 0), stack(st_s, 1), stack(st_s, 2), stack(st_s, 3), stack(st_s, 4), stack(st_s, 5))
```

```python
import functools

import numpy as np
import jax
import jax.numpy as jnp
from jax import lax
from jax.experimental import pallas as pl
from jax.experimental.pallas import tpu as pltpu

F32 = jnp.float32
BF16 = jnp.bfloat16

D_MODEL = 1024
DEPTH = 4
PAGE_SIZE = 128
HEAD_DIM = 64
CONV_DIM = 256
ATTN_DIM = 512
SSD_DIM = 256
FOX_HEADS = 8
SC_WIDTH = 3
SSD_HEADS = 4
SSD_STATE = 64
SSD_CONV = 4
SSD_CHUNK = 128
XBC_DIM = 512
FFN_DIM = 2816
MEM_LEN = 256
MEM_HEADS = 4
MEM_HEAD_DIM = 128
MEM_DIM = 512
RMS_EPS = 1e-6
LOG2E = 1.4426950408889634

CUMSUM_BLOCK = 256
LANES = 128
SUBLANES = 8
VMEM_LIMIT = 56 * 1024 * 1024
ROW_TILE = 512
PAGES_PER_STEP = 32
XATTN_SEQS_PER_STEP = 16
FOX_HEADS_PER_STEP = 4
SSD_CHUNKS_PER_STEP = 4
SSD_SEQS_PER_STEP = 8
PROJ_QKV = 3 * CONV_DIM + 3 * ATTN_DIM
F_LANE0 = 0
DT_LANE0 = FOX_HEADS
PAD_DIM = FOX_HEADS * LANES
BIAS_LANE0 = HEAD_DIM
N_SPLIT = 3


def _params(sem):
    return pltpu.CompilerParams(dimension_semantics=sem, vmem_limit_bytes=VMEM_LIMIT)


def _resident(shape):
    nd = len(shape)
    return pl.BlockSpec(shape, lambda *_: (0,) * nd, pipeline_mode=pl.Buffered(1))


def _resident_layer(shape, layer):
    nd = len(shape)
    return pl.BlockSpec((None,) + tuple(shape), lambda *_: (layer,) + (0,) * nd, pipeline_mode=pl.Buffered(1))


def _rms(x, g):
    ms = jnp.mean(x * x, axis=-1, keepdims=True)
    return x * lax.rsqrt(ms + RMS_EPS) * g


def _head_rms(x, g, head_dim):
    width = x.shape[-1]
    pieces = []
    for c in range(width // LANES):
        blk = x[:, c * LANES:(c + 1) * LANES]
        sq = blk * blk
        s_all = jnp.sum(sq, axis=-1, keepdims=True)
        if head_dim == LANES:
            ms = s_all * (1.0 / LANES)
        else:
            lo = lax.broadcasted_iota(jnp.int32, sq.shape, 1) < head_dim
            s_lo = jnp.sum(jnp.where(lo, sq, 0.0), axis=-1, keepdims=True)
            ms = jnp.where(lo, s_lo, s_all - s_lo) * (1.0 / head_dim)
        pieces.append(blk * lax.rsqrt(ms + RMS_EPS))
    return jnp.concatenate(pieces, axis=-1) * g


def _silu(x):
    return x * jax.nn.sigmoid(x)


def _softplus(x):
    return jnp.maximum(x, 0.0) + jnp.log1p(jnp.exp(-jnp.abs(x)))


def _split3(x):
    hi = x.astype(BF16)
    r1 = x - hi.astype(F32)
    mid = r1.astype(BF16)
    lo = (r1 - mid.astype(F32)).astype(BF16)
    return hi, mid, lo


def _dot_exact_rhs(x, m_bf16):
    hi, mid, lo = _split3(x)
    out = jnp.dot(hi, m_bf16, preferred_element_type=F32)
    out = out + jnp.dot(mid, m_bf16, preferred_element_type=F32)
    return out + jnp.dot(lo, m_bf16, preferred_element_type=F32)


def _dot_exact_lhs(m_bf16, x):
    hi, mid, lo = _split3(x)
    out = jnp.dot(m_bf16, hi, preferred_element_type=F32)
    out = out + jnp.dot(m_bf16, mid, preferred_element_type=F32)
    return out + jnp.dot(m_bf16, lo, preferred_element_type=F32)


def _lower_tri(n, seq=None):
    r = lax.broadcasted_iota(jnp.int32, (n, n), 0)
    c = lax.broadcasted_iota(jnp.int32, (n, n), 1)
    keep = c <= r
    if seq is not None and seq < n:
        shift = jnp.int32(seq.bit_length() - 1)
        keep = keep & (lax.shift_right_logical(r, shift) == lax.shift_right_logical(c, shift))
    return keep.astype(BF16)


def _upper_tri(n):
    r = lax.broadcasted_iota(jnp.int32, (n, n), 0)
    c = lax.broadcasted_iota(jnp.int32, (n, n), 1)
    return (r <= c).astype(BF16)


def _nt(a, b):
    return lax.dot_general(a, b, (((1,), (1,)), ((), ())), preferred_element_type=F32)


def _tn(a, b):
    return lax.dot_general(a, b, (((0,), (0,)), ((), ())), preferred_element_type=F32)


def _row_tile(x, reps):
    return x if reps == 1 else jnp.concatenate([x] * reps, axis=0)


def _lane_tile(x, reps):
    return x if reps == 1 else jnp.concatenate([x] * reps, axis=-1)


def _ffn_body(x, g_ref, wg_ref, wu_ref, wd_ref, fg_ref, final):
    h = _rms(x, g_ref[...]).astype(BF16)
    a = jnp.dot(h, wg_ref[...], preferred_element_type=F32)
    b = jnp.dot(h, wu_ref[...], preferred_element_type=F32)
    t = (_silu(a) * b).astype(BF16)
    out = x + 0.5 * jnp.dot(t, wd_ref[...], preferred_element_type=F32)
    return _rms(out, fg_ref[...]) if final else out


def _ffn_kernel(x_ref, g_ref, wg_ref, wu_ref, wd_ref, fg_ref, o_ref, *, final):
    o_ref[...] = _ffn_body(x_ref[...], g_ref, wg_ref, wu_ref, wd_ref, fg_ref, final)


def _ffn(x, g, wg, wu, wd, fg, layer, final):
    n = x.shape[0]
    tm = min(ROW_TILE, n)
    return pl.pallas_call(
        functools.partial(_ffn_kernel, final=final),
        out_shape=jax.ShapeDtypeStruct((n, D_MODEL), F32),
        grid=(n // tm,),
        in_specs=[
            pl.BlockSpec((tm, D_MODEL), lambda i: (i, 0)),
            _resident((1, D_MODEL)),
            _resident_layer((D_MODEL, FFN_DIM), layer),
            _resident_layer((D_MODEL, FFN_DIM), layer),
            _resident_layer((FFN_DIM, D_MODEL), layer),
            _resident((1, D_MODEL)),
        ],
        out_specs=pl.BlockSpec((tm, D_MODEL), lambda i: (i, 0)),
        compiler_params=_params(("arbitrary",)),
        name="ffn",
    )(x, g, wg, wu, wd, fg)


def _expand_heads(x, fill):
    lo = lax.broadcasted_iota(jnp.int32, (x.shape[0], LANES), 1) < HEAD_DIM
    blocks = []
    for c in range(ATTN_DIM // LANES):
        blk = x[:, c * LANES:(c + 1) * LANES]
        for half, src in enumerate((blk, pltpu.roll(blk, HEAD_DIM, axis=1))):
            h = 2 * c + half
            blocks.append(jnp.where(lo, src, fill[:, h * LANES:(h + 1) * LANES]))
    return jnp.concatenate(blocks, axis=-1).astype(BF16)


def _inproj_kernel(x_ref, g_ref, w_ref, wzx_ref, wfd_ref, qg_ref, kg_ref, fb_ref, pm_ref, pc_ref,
                   scb_ref, cin_ref, qa_ref, kn_ref, ka_ref, v_ref, va_ref,
                   z_ref, xbc_ref, fdt_ref, *rest, tm, seq, dim_major):
    carry_ref = rest[-1]
    h = _rms(x_ref[...], g_ref[...]).astype(BF16)

    def store_kv(ref, val):
        ref[...] = val.T.reshape(FOX_HEADS, HEAD_DIM, tm) if dim_major else val

    def proj(a, b, ref=w_ref):
        return _nt(h, ref[a:b, :])

    o = 0
    scb_ref[...] = proj(o, o + CONV_DIM)
    o += CONV_DIM
    cin_ref[...] = proj(o, o + CONV_DIM) * proj(o + CONV_DIM, o + 2 * CONV_DIM)
    o += 2 * CONV_DIM
    q = _head_rms(proj(o, o + ATTN_DIM), qg_ref[...], HEAD_DIM) * (HEAD_DIM ** -0.5 * LOG2E)
    o += ATTN_DIM
    k = _head_rms(proj(o, o + ATTN_DIM), kg_ref[...], HEAD_DIM)
    store_kv(kn_ref, k)
    o += ATTN_DIM
    v = proj(o, o + ATTN_DIM)
    store_kv(v_ref, v)
    z_ref[...] = proj(0, SSD_DIM, wzx_ref)
    xbc_ref[...] = proj(SSD_DIM, SSD_DIM + XBC_DIM, wzx_ref)
    u = proj(0, LANES, wfd_ref) + fb_ref[...]
    lane = lax.broadcasted_iota(jnp.int32, u.shape, 1)
    fdt = jnp.where(lane < DT_LANE0, -_softplus(-u), _softplus(u))
    fdt_ref[...] = fdt
    if dim_major:
        rest[0][...] = fdt.T[0:FOX_HEADS, :]

    @pl.when(pl.program_id(0) % max(seq // tm, 1) == 0)
    def _():
        carry_ref[...] = jnp.zeros_like(carry_ref)

    lf = jnp.where(lane < FOX_HEADS, fdt, 0.0) * LOG2E
    blk = min(tm, CUMSUM_BLOCK)
    tri = _lower_tri(blk, seq)
    carry = carry_ref[0:1, :]
    c_parts = []
    for r in range(tm // blk):
        c_blk = _dot_exact_lhs(tri, lf[r * blk:(r + 1) * blk, :])
        if seq >= blk and (r == 0 or (r * blk) % seq != 0):
            c_blk = c_blk + carry
        carry = c_blk[blk - 1:blk, :]
        c_parts.append(c_blk)
    c = jnp.concatenate(c_parts, axis=0)
    carry_ref[...] = jnp.broadcast_to(carry, carry_ref.shape)
    hi, mid, lo = _split3(c)
    packed = (hi.astype(F32) + pltpu.roll(mid.astype(F32), FOX_HEADS, axis=1)
              + pltpu.roll(lo.astype(F32), 2 * FOX_HEADS, axis=1)).astype(BF16)
    both = jnp.dot(packed, pm_ref[...], preferred_element_type=F32)
    lane_blk = _lane_tile(lax.broadcasted_iota(jnp.int32, (tm, LANES), 1), FOX_HEADS)
    q_side = lane_blk < BIAS_LANE0 + N_SPLIT
    qa_ref[...] = _expand_heads(q, jnp.where(q_side, both, pc_ref[:, 0:PAD_DIM]))
    ka_ref[...] = _expand_heads(k, jnp.where(q_side, pc_ref[:, PAD_DIM:2 * PAD_DIM], both))
    va_ref[...] = _expand_heads(v, pc_ref[:, 2 * PAD_DIM:3 * PAD_DIM])


def _bias_placement():
    pm = np.zeros((LANES, PAD_DIM), np.float32)
    pc = np.zeros((1, 3 * PAD_DIM), np.float32)
    for h in range(FOX_HEADS):
        for part in range(N_SPLIT):
            src = part * FOX_HEADS + h
            pm[src, h * LANES + BIAS_LANE0 + part] = 1.0
            pm[src, h * LANES + BIAS_LANE0 + N_SPLIT + part] = -1.0
            pc[0, h * LANES + BIAS_LANE0 + N_SPLIT + part] = 1.0
            pc[0, PAD_DIM + h * LANES + BIAS_LANE0 + part] = 1.0
        pc[0, 2 * PAD_DIM + h * LANES + HEAD_DIM] = 1.0
    return jnp.asarray(pm, BF16), jnp.asarray(pc, F32)


def _inproj(x, g, w, qg, kg, fb, pm, pc, layer, seq):
    w_t, w_zx, w_fd = w
    n = x.shape[0]
    tm = min(ROW_TILE, n)
    widths = [(CONV_DIM, F32), (CONV_DIM, F32), (PAD_DIM, BF16), (ATTN_DIM, F32), (PAD_DIM, BF16),
              (ATTN_DIM, F32), (PAD_DIM, BF16), (SSD_DIM, F32), (XBC_DIM, F32), (LANES, F32)]
    out_shape = [jax.ShapeDtypeStruct((n, w_), dt) for w_, dt in widths]
    out_specs = [pl.BlockSpec((tm, w_), lambda i: (i, 0)) for w_, _ in widths]
    dim_major = seq >= tm
    if dim_major:
        tps = seq // tm
        kv_shape = jax.ShapeDtypeStruct((n // seq, FOX_HEADS, HEAD_DIM, seq), F32)
        kv_spec = pl.BlockSpec((None, FOX_HEADS, HEAD_DIM, tm), lambda i: (i // tps, 0, 0, i % tps))
        out_shape[3], out_shape[5] = kv_shape, kv_shape
        out_specs[3], out_specs[5] = kv_spec, kv_spec
        out_shape.append(jax.ShapeDtypeStruct((n // seq, FOX_HEADS, seq), F32))
        out_specs.append(pl.BlockSpec((None, FOX_HEADS, tm), lambda i: (i // tps, 0, i % tps)))
    return pl.pallas_call(
        functools.partial(_inproj_kernel, tm=tm, seq=seq, dim_major=dim_major),
        out_shape=out_shape,
        grid=(n // tm,),
        in_specs=[
            pl.BlockSpec((tm, D_MODEL), lambda i: (i, 0)),
            _resident((1, D_MODEL)),
            _resident_layer((PROJ_QKV, D_MODEL), layer),
            _resident_layer((SSD_DIM + XBC_DIM, D_MODEL), layer),
            _resident_layer((LANES, D_MODEL), layer),
            _resident((1, ATTN_DIM)),
            _resident((1, ATTN_DIM)),
            _resident((1, LANES)),
            _resident((LANES, PAD_DIM)),
            _resident((1, 3 * PAD_DIM)),
        ],
        out_specs=out_specs,
        scratch_shapes=[pltpu.VMEM((SUBLANES, LANES), F32)],
        compiler_params=_params(("arbitrary",)),
        name="inproj",
    )(x, g, w_t, w_zx, w_fd, qg, kg, fb, pm, pc)


def _fox_kernel(q_ref, k_ref, v_ref, o_ref, m_ref, acc_ref, *, tq):
    i = pl.program_id(2)
    m_ref[...] = jnp.full(m_ref.shape, -jnp.inf, F32)
    acc_ref[...] = jnp.zeros_like(acc_ref)

    def tile(start, width, masked):
        rows = pl.ds(pl.multiple_of(start, tq), width)
        for hh in range(FOX_HEADS_PER_STEP):
            sl = slice(hh * LANES, (hh + 1) * LANES)
            s = _nt(q_ref[:, sl], k_ref[rows, sl])
            if masked:
                row = lax.broadcasted_iota(jnp.int32, s.shape, 0)
                col = lax.broadcasted_iota(jnp.int32, s.shape, 1)
                s = jnp.where(col <= row + (width - tq), s, -jnp.inf)
            m_old = m_ref[hh]
            m_new = jnp.maximum(m_old, jnp.max(s, axis=-1, keepdims=True))
            p = jnp.exp2(s - _lane_tile(m_new, width // LANES))
            acc_ref[hh] = jnp.exp2(m_old - m_new) * acc_ref[hh] + jnp.dot(
                p.astype(BF16), v_ref[rows, sl], preferred_element_type=F32)
            m_ref[hh] = m_new

    def body(j, carry):
        tile(j * (2 * tq), 2 * tq, False)
        return carry

    odd = lax.bitwise_and(i, 1) == 1
    n_wide = lax.shift_right_logical(i, 1) - jnp.where(odd | (i == 0), 0, 1)
    lax.fori_loop(0, n_wide, body, 0)

    @pl.when(odd)
    def _():
        tile((i - 1) * tq, 2 * tq, True)

    @pl.when(jnp.logical_not(odd) & (i > 0))
    def _():
        tile((i - 2) * tq, 3 * tq, True)

    @pl.when(i == 0)
    def _():
        tile(0, tq, True)
    outs = []
    for hh in range(FOX_HEADS_PER_STEP):
        a = acc_ref[hh]
        outs.append(a[:, 0:HEAD_DIM] / a[:, HEAD_DIM:HEAD_DIM + 1])
    o_ref[...] = jnp.concatenate(outs, axis=-1)


def _fox_prompt(qa, ka, va, nseq, seq):
    tq = min(ROW_TILE, seq)
    nq = seq // tq
    nh = FOX_HEADS_PER_STEP
    return pl.pallas_call(
        functools.partial(_fox_kernel, tq=tq),
        out_shape=jax.ShapeDtypeStruct((nseq * seq, ATTN_DIM), F32),
        grid=(nseq, FOX_HEADS // nh, nq),
        in_specs=[
            pl.BlockSpec((tq, nh * LANES), lambda b, h, i: (b * nq + i, h)),
            pl.BlockSpec((seq, nh * LANES), lambda b, h, i: (b, h)),
            pl.BlockSpec((seq, nh * LANES), lambda b, h, i: (b, h)),
        ],
        out_specs=pl.BlockSpec((tq, nh * HEAD_DIM), lambda b, h, i: (b * nq + i, h)),
        scratch_shapes=[pltpu.VMEM((nh, tq, LANES), F32), pltpu.VMEM((nh, tq, LANES), F32)],
        compiler_params=_params(("arbitrary",) * 3),
        name="fox_prompt",
    )(qa, ka, va)


def _fox_paged_kernel(pt_ref, q_ref, kn_ref, vn_ref, lfn_ref, *rest, tokens):
    npg = PAGES_PER_STEP
    k_refs = rest[:npg]
    v_refs = rest[npg:2 * npg]
    lf_refs = rest[2 * npg:3 * npg]
    o_ref, m_ref, l_ref, acc_ref, carry_ref, c_ref = rest[3 * npg:]
    g = pl.program_id(1)
    rows = FOX_HEADS * tokens

    @pl.when(g == 0)
    def _():
        m_ref[...] = jnp.full(m_ref.shape, -jnp.inf, F32)
        l_ref[...] = jnp.zeros_like(l_ref)
        acc_ref[...] = jnp.zeros_like(acc_ref)
        carry_ref[...] = jnp.zeros_like(carry_ref)

    q = q_ref[...]
    tri = _upper_tri(PAGE_SIZE)

    def update(k_list, v_list, lf_list):
        n = len(k_list)
        nr = n * FOX_HEADS
        parts = _split3(jnp.concatenate(lf_list, axis=0) * LOG2E)

        def times(m_bf16):
            return sum(jnp.dot(x, m_bf16, preferred_element_type=F32) for x in parts)

        c = times(tri) + _row_tile(carry_ref[...], n)
        if n > 1:
            tot = times(jnp.ones((PAGE_SIZE, PAGE_SIZE), BF16))
            r = lax.broadcasted_iota(jnp.int32, (nr, nr), 0)
            col = lax.broadcasted_iota(jnp.int32, (nr, nr), 1)
            same_head = lax.bitwise_and(r, FOX_HEADS - 1) == lax.bitwise_and(col, FOX_HEADS - 1)
            earlier = jnp.where(col < r, jnp.where(same_head, 1.0, 0.0), 0.0).astype(BF16)
            off = _dot_exact_lhs(earlier, tot)
            c = c + off
            carry_ref[...] = carry_ref[...] + off[nr - FOX_HEADS:nr, :] + tot[nr - FOX_HEADS:nr, :]
        c_ref[0:nr, :] = c
        s_list = []
        for idx in range(n):
            c_rows = jnp.concatenate(
                [jnp.broadcast_to(c_ref[idx * FOX_HEADS + h:idx * FOX_HEADS + h + 1, :], (tokens, PAGE_SIZE))
                 for h in range(FOX_HEADS)], axis=0)
            s_list.append(jnp.dot(q, k_list[idx].astype(BF16), preferred_element_type=F32) - c_rows)
        s = jnp.concatenate(s_list, axis=-1)
        m_old = m_ref[...]
        m_new = jnp.maximum(m_old, jnp.max(s, axis=-1, keepdims=True))
        alpha = jnp.exp2(m_old - m_new)
        p = jnp.exp2(s - _lane_tile(m_new, n))
        l_ref[...] = alpha * l_ref[...] + jnp.sum(p, axis=-1, keepdims=True)
        pv = _nt(p[:, 0:PAGE_SIZE].astype(BF16), v_list[0].astype(BF16))
        for idx in range(1, n):
            pv = pv + _nt(p[:, idx * PAGE_SIZE:(idx + 1) * PAGE_SIZE].astype(BF16), v_list[idx].astype(BF16))
        acc_ref[...] = _lane_tile(alpha, ATTN_DIM // LANES) * acc_ref[...] + pv
        m_ref[...] = m_new

    update([r[...] for r in k_refs], [r[...] for r in v_refs], [r[...] for r in lf_refs])

    @pl.when(g == pl.num_programs(1) - 1)
    def _():
        c_new = _dot_exact_rhs(lfn_ref[...] * LOG2E, _upper_tri(tokens)) + carry_ref[:, 0:tokens]
        c_ref[0:FOX_HEADS, 0:tokens] = c_new
        c_rows = jnp.concatenate(
            [jnp.broadcast_to(c_ref[h:h + 1, 0:tokens], (tokens, tokens)) for h in range(FOX_HEADS)], axis=0)
        s = _nt(q, kn_ref[...].astype(BF16)) - c_rows
        t_idx = lax.bitwise_and(lax.broadcasted_iota(jnp.int32, s.shape, 0), tokens - 1)
        s = jnp.where(lax.broadcasted_iota(jnp.int32, s.shape, 1) <= t_idx, s, -jnp.inf)
        m_old = m_ref[...]
        m_new = jnp.maximum(m_old, jnp.max(s, axis=-1, keepdims=True))
        alpha = jnp.exp2(m_old - m_new)
        p = jnp.exp2(s - m_new[:, 0:tokens])
        l_new = alpha * l_ref[...] + jnp.sum(p, axis=-1, keepdims=True)
        acc = _lane_tile(alpha, ATTN_DIM // LANES) * acc_ref[...] + jnp.dot(
            p.astype(BF16), vn_ref[...].astype(BF16), preferred_element_type=F32)
        full = acc / _lane_tile(l_new, ATTN_DIM // LANES)
        lane_head = lax.shift_right_logical(
            lax.broadcasted_iota(jnp.int32, (tokens, ATTN_DIM), 1), jnp.int32(HEAD_DIM.bit_length() - 1))
        out = jnp.zeros((tokens, ATTN_DIM), F32)
        for h in range(FOX_HEADS):
            out = out + jnp.where(lane_head == h, full[h * tokens:(h + 1) * tokens, :], 0.0)
        o_ref[...] = out


def _fox_paged(layer, page_table, qbd, kn_t, vn_t, lfn_t, cache_kt, cache_vt, cache_lf_t, nseq, tokens):
    npg = PAGES_PER_STEP
    n_groups = page_table.shape[1] // npg
    rows = FOX_HEADS * tokens

    def page_spec(width2, n):
        return pl.BlockSpec((None, None, width2, PAGE_SIZE),
                            lambda b, g, pt: (layer, pt[b, g * npg + n], 0, 0))

    in_specs = [
        pl.BlockSpec((None, rows, ATTN_DIM), lambda b, g, pt: (b, 0, 0)),
        pl.BlockSpec((None, tokens, ATTN_DIM), lambda b, g, pt: (b, 0, 0)),
        pl.BlockSpec((None, tokens, ATTN_DIM), lambda b, g, pt: (b, 0, 0)),
        pl.BlockSpec((None, FOX_HEADS, tokens), lambda b, g, pt: (b, 0, 0)),
    ]
    in_specs += [page_spec(ATTN_DIM, n) for n in range(npg)]
    in_specs += [page_spec(ATTN_DIM, n) for n in range(npg)]
    in_specs += [page_spec(FOX_HEADS, n) for n in range(npg)]
    grid_spec = pltpu.PrefetchScalarGridSpec(
        num_scalar_prefetch=1,
        grid=(nseq, n_groups),
        in_specs=in_specs,
        out_specs=pl.BlockSpec((None, tokens, ATTN_DIM), lambda b, g, pt: (b, 0, 0)),
        scratch_shapes=[pltpu.VMEM((rows, LANES), F32), pltpu.VMEM((rows, LANES), F32),
                        pltpu.VMEM((rows, ATTN_DIM), F32), pltpu.VMEM((FOX_HEADS, LANES), F32),
                        pltpu.VMEM((npg * FOX_HEADS, LANES), F32)],
    )
    return pl.pallas_call(
        functools.partial(_fox_paged_kernel, tokens=tokens),
        out_shape=jax.ShapeDtypeStruct((nseq, tokens, ATTN_DIM), F32),
        grid_spec=grid_spec,
        compiler_params=_params(("arbitrary", "arbitrary")),
        name="fox_paged",
    )(page_table, qbd, kn_t, vn_t, lfn_t,
      *([cache_kt] * npg), *([cache_vt] * npg), *([cache_lf_t] * npg))


def _ssd_kernel(xbc_ref, halo_ref, st_ref, fdt_ref, z_ref, h0_ref,
                cw_ref, cb_ref, av_ref, dv_ref,
                y_ref, hout_ref, buf_ref, h_ref, *, lc, rows):
    j = pl.program_id(1)
    bs = st_ref.shape[0]

    @pl.when(j == 0)
    def _():
        h_ref[...] = h0_ref[...]
        buf_ref[:, 0:SUBLANES, :] = st_ref[...]

    @pl.when(j > 0)
    def _():
        buf_ref[0, 0:SUBLANES, :] = halo_ref[...]

    buf_ref[:, SUBLANES:SUBLANES + rows, :] = xbc_ref[...].reshape(bs, rows, XBC_DIM)
    base = SUBLANES - (SSD_CONV - 1)
    u_all = buf_ref[:, base:base + rows, :] * cw_ref[0:1, :]
    for k in range(1, SSD_CONV):
        u_all = u_all + buf_ref[:, base + k:base + k + rows, :] * cw_ref[k:k + 1, :]
    u_all = _silu(u_all + cb_ref[...]).reshape(bs * rows, XBC_DIM)

    tri_lo = _lower_tri(lc)
    tri_up = _upper_tri(lc)
    pick = (lax.broadcasted_iota(jnp.int32, (SUBLANES, LANES), 1)
            == lax.broadcasted_iota(jnp.int32, (SUBLANES, LANES), 0) + DT_LANE0).astype(BF16)

    def rows_of(x):
        return sum(_nt(pick, part) for part in _split3(x))

    row = lax.broadcasted_iota(jnp.int32, (lc, lc), 0)
    col = lax.broadcasted_iota(jnp.int32, (lc, lc), 1)
    causal = col <= row
    dvec = dv_ref[...]

    for s, c in ((s, c) for s in range(bs) for c in range(rows // lc)):
        tok = slice(s * rows + c * lc, s * rows + (c + 1) * lc)
        u = u_all[tok, :]
        z = z_ref[tok, :]
        fdt = fdt_ref[tok, :]
        da = fdt * av_ref[...]
        acum = _dot_exact_lhs(tri_lo, da)
        dtt = rows_of(fdt)
        acum_t = _dot_exact_rhs(rows_of(da), tri_up)
        outs = []
        for h in range(SSD_HEADS):
            grp = h // (SSD_HEADS // 2)
            xh = u[:, h * HEAD_DIM:(h + 1) * HEAD_DIM]
            bh = u[:, SSD_DIM + grp * SSD_STATE:SSD_DIM + (grp + 1) * SSD_STATE]
            ch = u[:, SSD_DIM + 2 * SSD_STATE + grp * SSD_STATE:SSD_DIM + 2 * SSD_STATE + (grp + 1) * SSD_STATE]
            lane = DT_LANE0 + h
            a_col = acum[:, lane:lane + 1]
            dt_col = fdt[:, lane:lane + 1]
            a_row = acum_t[h:h + 1, :]
            dt_row = dtt[h:h + 1, :]
            a_last = acum[lc - 1:lc, lane:lane + 1]
            decay = jnp.exp(jnp.where(causal, a_col - a_row, -jnp.inf))
            xb = xh.astype(BF16)
            bb = bh.astype(BF16)
            cb16 = ch.astype(BF16)
            w = _nt(cb16, bb) * decay * dt_row
            h_prev = h_ref[s, h]
            y = jnp.dot(w.astype(BF16), xb, preferred_element_type=F32)
            y = y + _nt(cb16, h_prev.astype(BF16)) * jnp.exp(a_col)
            w_end = jnp.exp(a_last - a_col) * dt_col
            h_ref[s, h] = h_prev * jnp.exp(a_last) + _tn((xh * w_end).astype(BF16), bb)
            sl = slice(h * HEAD_DIM, (h + 1) * HEAD_DIM)
            outs.append((y + dvec[:, sl] * xh) * _silu(z[:, sl]))
        y_ref[tok, :] = jnp.concatenate(outs, axis=-1)

    @pl.when(j == pl.num_programs(1) - 1)
    def _():
        hout_ref[...] = h_ref[...]


def _ssd(xbc, st_pad, fdt, z, h0, cw, cb, av, dv, nseq, seq):
    lc = min(SSD_CHUNK, seq)
    rows = min(SSD_CHUNKS_PER_STEP * lc, seq)
    nc = seq // rows
    bs = 1 if nc > 1 else min(nseq, SSD_SEQS_PER_STEP)
    nrows = bs * rows
    hb = nrows // SUBLANES
    state_spec = pl.BlockSpec((bs, SSD_HEADS, HEAD_DIM, SSD_STATE), lambda b, j: (b, 0, 0, 0))
    return pl.pallas_call(
        functools.partial(_ssd_kernel, lc=lc, rows=rows),
        out_shape=[jax.ShapeDtypeStruct((nseq * seq, SSD_DIM), F32),
                   jax.ShapeDtypeStruct((nseq, SSD_HEADS, HEAD_DIM, SSD_STATE), F32)],
        grid=(nseq // bs, nc),
        in_specs=[
            pl.BlockSpec((nrows, XBC_DIM), lambda b, j: (b * nc + j, 0)),
            pl.BlockSpec((SUBLANES, XBC_DIM), lambda b, j: (jnp.maximum((b * nc + j) * hb - 1, 0), 0)),
            pl.BlockSpec((bs, SUBLANES, XBC_DIM), lambda b, j: (b, 0, 0)),
            pl.BlockSpec((nrows, LANES), lambda b, j: (b * nc + j, 0)),
            pl.BlockSpec((nrows, SSD_DIM), lambda b, j: (b * nc + j, 0)),
            state_spec,
            _resident((SSD_CONV, XBC_DIM)),
            _resident((1, XBC_DIM)),
            _resident((1, LANES)),
            _resident((1, SSD_DIM)),
        ],
        out_specs=[pl.BlockSpec((nrows, SSD_DIM), lambda b, j: (b * nc + j, 0)), state_spec],
        scratch_shapes=[pltpu.VMEM((bs, SUBLANES + rows, XBC_DIM), F32),
                        pltpu.VMEM((bs, SSD_HEADS, HEAD_DIM, SSD_STATE), F32)],
        compiler_params=_params(("arbitrary", "arbitrary")),
        name="ssd",
    )(xbc, xbc, st_pad, fdt, z, h0, cw, cb, av, dv)


def _merge_body(x_ref, scb_ref, cin_ref, halo_ref, st_ref, attn_ref, yssd_ref,
                cw_ref, g_ref, w_ref, buf_ref, tl):
    j = pl.program_id(1)
    bs = st_ref.shape[0]

    @pl.when(j == 0)
    def _():
        buf_ref[:, 0:SUBLANES, :] = st_ref[...]

    @pl.when(j > 0)
    def _():
        buf_ref[0, 0:SUBLANES, :] = halo_ref[...]

    buf_ref[:, SUBLANES:SUBLANES + tl, :] = cin_ref[...].reshape(bs, tl, CONV_DIM)
    base = SUBLANES - (SC_WIDTH - 1)
    y = buf_ref[:, base:base + tl, :] * cw_ref[0:1, :]
    for k in range(1, SC_WIDTH):
        y = y + buf_ref[:, base + k:base + k + tl, :] * cw_ref[k:k + 1, :]
    y = y.reshape(bs * tl, CONV_DIM)
    cat = jnp.concatenate([scb_ref[...] * y, attn_ref[...], yssd_ref[...]], axis=-1)
    cat = _head_rms(cat, g_ref[...], HEAD_DIM).astype(BF16)
    return x_ref[...] + jnp.dot(cat, w_ref[...], preferred_element_type=F32)


def _merge_kernel(x_ref, scb_ref, cin_ref, halo_ref, st_ref, attn_ref, yssd_ref,
                  cw_ref, g_ref, w_ref, o_ref, buf_ref, *, tl):
    o_ref[...] = _merge_body(x_ref, scb_ref, cin_ref, halo_ref, st_ref, attn_ref, yssd_ref,
                             cw_ref, g_ref, w_ref, buf_ref, tl)


def _merge(x, scb, cin, st_pad, attn, yssd, cw, g, w, layer, nseq, seq):
    tl = min(ROW_TILE, seq)
    nt = seq // tl
    bs = 1 if nt > 1 else min(nseq, ROW_TILE // tl)
    nrows = bs * tl
    hb = nrows // SUBLANES

    def rows(width):
        return pl.BlockSpec((nrows, width), lambda b, j: (b * nt + j, 0))

    return pl.pallas_call(
        functools.partial(_merge_kernel, tl=tl),
        out_shape=jax.ShapeDtypeStruct((nseq * seq, D_MODEL), F32),
        grid=(nseq // bs, nt),
        in_specs=[
            rows(D_MODEL), rows(CONV_DIM), rows(CONV_DIM),
            pl.BlockSpec((SUBLANES, CONV_DIM), lambda b, j: (jnp.maximum((b * nt + j) * hb - 1, 0), 0)),
            pl.BlockSpec((bs, SUBLANES, CONV_DIM), lambda b, j: (b, 0, 0)),
            rows(ATTN_DIM), rows(SSD_DIM),
            _resident((SC_WIDTH, CONV_DIM)),
            _resident((1, D_MODEL)),
            _resident_layer((D_MODEL, D_MODEL), layer),
        ],
        out_specs=rows(D_MODEL),
        scratch_shapes=[pltpu.VMEM((bs, SUBLANES + tl, CONV_DIM), F32)],
        compiler_params=_params(("arbitrary", "arbitrary")),
        name="merge",
    )(x, scb, cin, cin, st_pad, attn, yssd, cw, g, w)


def _memkv_kernel(mem_ref, g_ref, wk_ref, wv_ref, kg_ref, km_ref, vm_ref):
    m = _rms(mem_ref[...], g_ref[...]).astype(BF16)
    km_ref[...] = _head_rms(jnp.dot(m, wk_ref[...], preferred_element_type=F32), kg_ref[...], MEM_HEAD_DIM)
    vm_ref[...] = jnp.dot(m, wv_ref[...], preferred_element_type=F32)


def _memkv(mem, g, wk, wv, kg):
    nb = mem.shape[0]
    out = jax.ShapeDtypeStruct((DEPTH, nb, MEM_LEN, MEM_DIM), F32)
    return pl.pallas_call(
        _memkv_kernel,
        out_shape=[out, out],
        grid=(DEPTH, nb),
        in_specs=[
            pl.BlockSpec((None, MEM_LEN, D_MODEL), lambda l, b: (b, 0, 0)),
            pl.BlockSpec((None, 1, D_MODEL), lambda l, b: (l, 0, 0)),
            pl.BlockSpec((None, D_MODEL, MEM_DIM), lambda l, b: (l, 0, 0)),
            pl.BlockSpec((None, D_MODEL, MEM_DIM), lambda l, b: (l, 0, 0)),
            pl.BlockSpec((None, 1, MEM_DIM), lambda l, b: (l, 0, 0)),
        ],
        out_specs=[pl.BlockSpec((None, None, MEM_LEN, MEM_DIM), lambda l, b: (l, b, 0, 0))] * 2,
        compiler_params=_params(("arbitrary", "arbitrary")),
        name="mem_kv",
    )(mem, g, wk, wv, kg)


def _xattn_body(x, km_ref, vm_ref, g_ref, wq_ref, qg_ref, wo_ref, q_ref, att_ref, bs, tl):
    h = _rms(x, g_ref[...]).astype(BF16)
    q = _head_rms(jnp.dot(h, wq_ref[...], preferred_element_type=F32), qg_ref[...], MEM_HEAD_DIM)
    q = q * (MEM_HEAD_DIM ** -0.5)
    for hh in range(MEM_HEADS):
        q_ref[hh] = q[:, hh * MEM_HEAD_DIM:(hh + 1) * MEM_HEAD_DIM]

    def softmax(sc):
        p = jnp.exp(sc - jnp.max(sc, axis=-1, keepdims=True))
        return (p / jnp.sum(p, axis=-1, keepdims=True)).astype(BF16)

    for s in range(bs):
        rows = slice(s * tl, (s + 1) * tl)
        if len(km_ref.shape) == 4:
            km = km_ref[s].reshape(MEM_LEN * MEM_HEADS, MEM_HEAD_DIM).astype(BF16)
            vm = vm_ref[s].reshape(MEM_LEN * MEM_HEADS, MEM_HEAD_DIM).astype(BF16)
            qs = jnp.concatenate([q_ref[hh, rows, :] for hh in range(MEM_HEADS)], axis=0).astype(BF16)
            sc = _nt(qs, km)
            row_head = lax.shift_right_logical(lax.broadcasted_iota(jnp.int32, sc.shape, 0),
                                               jnp.int32(tl.bit_length() - 1))
            col_head = lax.bitwise_and(lax.broadcasted_iota(jnp.int32, sc.shape, 1), MEM_HEADS - 1)
            out = jnp.dot(softmax(jnp.where(row_head == col_head, sc, -jnp.inf)), vm,
                          preferred_element_type=F32)
            for hh in range(MEM_HEADS):
                att_ref[hh, rows, :] = out[hh * tl:(hh + 1) * tl, :]
        else:
            for hh in range(MEM_HEADS):
                sl = slice(hh * MEM_HEAD_DIM, (hh + 1) * MEM_HEAD_DIM)
                sc = _nt(q_ref[hh, rows, :].astype(BF16), km_ref[s, :, sl].astype(BF16))
                att_ref[hh, rows, :] = jnp.dot(softmax(sc), vm_ref[s, :, sl].astype(BF16),
                                               preferred_element_type=F32)

    att = jnp.concatenate([att_ref[hh] for hh in range(MEM_HEADS)], axis=-1).astype(BF16)
    return x + jnp.dot(att, wo_ref[...], preferred_element_type=F32)


def _xattn_kernel(x_ref, km_ref, vm_ref, g_ref, wq_ref, qg_ref, wo_ref, o_ref, q_ref, att_ref, *, bs, tl):
    o_ref[...] = _xattn_body(x_ref[...], km_ref, vm_ref, g_ref, wq_ref, qg_ref, wo_ref, q_ref, att_ref, bs, tl)


def _xattn(x, km, vm, layer, g, wq, qg, wo, nseq, seq):
    tl = min(ROW_TILE, seq)
    nt = seq // tl
    bs = 1 if nt > 1 else min(nseq, XATTN_SEQS_PER_STEP)
    rows = bs * tl
    mem_zeros = (0,) * (km.ndim - 2)
    mem_spec = pl.BlockSpec((None, bs) + tuple(km.shape[2:]), lambda b, j: (layer, b) + mem_zeros)
    row_spec = pl.BlockSpec((rows, D_MODEL), lambda b, j: (b * nt + j, 0))
    return pl.pallas_call(
        functools.partial(_xattn_kernel, bs=bs, tl=tl),
        out_shape=jax.ShapeDtypeStruct((nseq * seq, D_MODEL), F32),
        grid=(nseq // bs, nt),
        in_specs=[
            row_spec, mem_spec, mem_spec,
            _resident((1, D_MODEL)),
            _resident_layer((D_MODEL, MEM_DIM), layer),
            _resident((1, MEM_DIM)),
            _resident_layer((MEM_DIM, D_MODEL), layer),
        ],
        out_specs=row_spec,
        scratch_shapes=[pltpu.VMEM((MEM_HEADS, rows, MEM_HEAD_DIM), F32),
                        pltpu.VMEM((MEM_HEADS, rows, MEM_HEAD_DIM), F32)],
        compiler_params=_params(("arbitrary", "arbitrary")),
        name="xattn",
    )(x, km, vm, g, wq, qg, wo)


def _merge_xattn_kernel(x_ref, scb_ref, cin_ref, halo_ref, st_ref, attn_ref, yssd_ref, cw_ref, go_ref,
                        wout_ref, km_ref, vm_ref, gx_ref, wq_ref, qg_ref, wo_ref,
                        o_ref, buf_ref, q_ref, att_ref, *, tl):
    x = _merge_body(x_ref, scb_ref, cin_ref, halo_ref, st_ref, attn_ref, yssd_ref,
                    cw_ref, go_ref, wout_ref, buf_ref, tl)
    o_ref[...] = _xattn_body(x, km_ref, vm_ref, gx_ref, wq_ref, qg_ref, wo_ref, q_ref, att_ref, 1, tl)


def _merge_xattn(x, scb, cin, st_pad, attn, yssd, cw, g_out, w_out, km, vm, g_xa, wq, qg, wo,
                 layer, nseq, seq):
    tl = ROW_TILE
    nt = seq // tl
    hb = tl // SUBLANES

    def rows(width):
        return pl.BlockSpec((tl, width), lambda b, j: (b * nt + j, 0))

    mem_spec = pl.BlockSpec((None, 1, MEM_LEN, MEM_DIM), lambda b, j: (layer, b, 0, 0))
    return pl.pallas_call(
        functools.partial(_merge_xattn_kernel, tl=tl),
        out_shape=jax.ShapeDtypeStruct((nseq * seq, D_MODEL), F32),
        grid=(nseq, nt),
        in_specs=[
            rows(D_MODEL), rows(CONV_DIM), rows(CONV_DIM),
            pl.BlockSpec((SUBLANES, CONV_DIM), lambda b, j: (jnp.maximum((b * nt + j) * hb - 1, 0), 0)),
            pl.BlockSpec((1, SUBLANES, CONV_DIM), lambda b, j: (b, 0, 0)),
            rows(ATTN_DIM), rows(SSD_DIM),
            _resident((SC_WIDTH, CONV_DIM)),
            _resident((1, D_MODEL)),
            _resident_layer((D_MODEL, D_MODEL), layer),
            mem_spec, mem_spec,
            _resident((1, D_MODEL)),
            _resident_layer((D_MODEL, MEM_DIM), layer),
            _resident((1, MEM_DIM)),
            _resident_layer((MEM_DIM, D_MODEL), layer),
        ],
        out_specs=rows(D_MODEL),
        scratch_shapes=[pltpu.VMEM((1, SUBLANES + tl, CONV_DIM), F32),
                        pltpu.VMEM((MEM_HEADS, tl, MEM_HEAD_DIM), F32),
                        pltpu.VMEM((MEM_HEADS, tl, MEM_HEAD_DIM), F32)],
        compiler_params=_params(("arbitrary", "arbitrary")),
        name="merge_xattn",
    )(x, scb, cin, cin, st_pad, attn, yssd, cw, g_out, w_out, km, vm, g_xa, wq, qg, wo)


def _pad_state(state, width):
    nseq, _, c = state.shape
    return jnp.concatenate([jnp.zeros((nseq, SUBLANES - (width - 1), c), F32), state], axis=1)


def _prep_w_in(w_in):
    w_t = jnp.transpose(w_in, (0, 2, 1)).astype(BF16)
    n_z0 = PROJ_QKV + FOX_HEADS
    n_dt0 = n_z0 + SSD_DIM + XBC_DIM
    pad = jnp.zeros((DEPTH, LANES - FOX_HEADS - SSD_HEADS, D_MODEL), BF16)
    w_fd = jnp.concatenate([w_t[:, PROJ_QKV:n_z0], w_t[:, n_dt0:], pad], axis=1)
    return w_t, w_t[:, n_z0:n_dt0], w_fd


def _layer_params(l, p):
    pad = jnp.zeros((LANES - FOX_HEADS - SSD_HEADS,), F32)
    a_neg = -jnp.exp(p['ssd_A_log'][l])
    return dict(
        fb=jnp.concatenate([p['fox_f_bias'][l], p['ssd_dt_bias'][l], pad])[None, :],
        qg=jnp.tile(p['fox_q_norm'][l], FOX_HEADS)[None, :],
        kg=jnp.tile(p['fox_k_norm'][l], FOX_HEADS)[None, :],
        av=jnp.concatenate([jnp.zeros((FOX_HEADS,), F32), a_neg, pad])[None, :],
        dv=jnp.repeat(p['ssd_D'][l], HEAD_DIM)[None, :],
        xa_qg=jnp.tile(p['xa_q_norm'][l], MEM_HEADS)[None, :],
    )


def _token_mix(l, x, p, lp, nseq, seq, conv_state, ssd_conv_state, ssd_state, paged, finish):
    n = nseq * seq
    scb, cin, qa, kn, ka, v, va, z, xbc, fdt, *lf_t = _inproj(
        x, p['mix_norm'][l][None, :], p['w_in_t'], lp['qg'], lp['kg'], lp['fb'], p['pm'], p['pc'], l, seq)

    if paged is None:
        attn = _fox_prompt(qa, ka, va, nseq, seq)
        k_out = jnp.transpose(kn, (0, 3, 1, 2))
        v_out = jnp.transpose(v, (0, 3, 1, 2))
        logf = jnp.transpose(lf_t[0], (0, 2, 1))
    else:
        logf = fdt[:, F_LANE0:F_LANE0 + FOX_HEADS].reshape(nseq, seq, FOX_HEADS)
        k_out = kn.reshape(nseq, seq, FOX_HEADS, HEAD_DIM)
        v_out = v.reshape(nseq, seq, FOX_HEADS, HEAD_DIM)
        page_table, cache_kt, cache_vt, cache_lf_t = paged
        eye = jnp.eye(FOX_HEADS, dtype=BF16)
        q4 = jnp.transpose(qa.reshape(nseq, seq, FOX_HEADS, LANES)[..., :HEAD_DIM], (0, 2, 1, 3))
        qbd = (q4[:, :, :, None, :] * eye[None, :, None, :, None]).reshape(nseq, FOX_HEADS * seq, ATTN_DIM)
        attn = _fox_paged(l, page_table, qbd, kn.reshape(nseq, seq, ATTN_DIM), v.reshape(nseq, seq, ATTN_DIM),
                          jnp.transpose(logf, (0, 2, 1)), cache_kt, cache_vt, cache_lf_t,
                          nseq, seq).reshape(n, ATTN_DIM)

    yssd, h_end = _ssd(xbc, _pad_state(ssd_conv_state, SSD_CONV), fdt, z, ssd_state, p['ssd_conv_w'][l],
                       p['ssd_conv_b'][l][None, :], lp['av'], lp['dv'], nseq, seq)
    x = finish(x, scb, cin, _pad_state(conv_state, SC_WIDTH), attn, yssd)

    cin3 = cin.reshape(nseq, seq, CONV_DIM)
    xbc3 = xbc.reshape(nseq, seq, XBC_DIM)
    new_conv = jnp.concatenate([conv_state, cin3], axis=1)[:, -(SC_WIDTH - 1):]
    new_ssd_conv = jnp.concatenate([ssd_conv_state, xbc3], axis=1)[:, -(SSD_CONV - 1):]
    state = (k_out, v_out, logf, new_conv, new_ssd_conv, h_end)
    return x, state


def kernel(x_prompt, x_sample, cache_fox_k, cache_fox_v, cache_fox_logf, cache_mem_k, cache_mem_v,
           state_conv, state_ssd_conv, state_ssd, page_table, mem_prompt,
           ffn1_norm, ffn1_wg, ffn1_wu, ffn1_wd, mix_norm, w_in, sc_conv_w, fox_q_norm, fox_k_norm,
           fox_f_bias, ssd_conv_w, ssd_conv_b, ssd_dt_bias, ssd_A_log, ssd_D, mix_out_norm, w_out,
           xa_norm, mem_norm, xa_wq, xa_wk, xa_wv, xa_q_norm, xa_k_norm, xa_wo,
           ffn2_norm, ffn2_wg, ffn2_wu, ffn2_wd, final_norm):
    b_p, s_p, _ = x_prompt.shape
    b_s, s_s, _ = x_sample.shape
    n_phys = cache_fox_k.shape[1]
    pm, pc = _bias_placement()
    p = dict(mix_norm=mix_norm, w_in_t=_prep_w_in(w_in), sc_conv_w=sc_conv_w, fox_q_norm=fox_q_norm,
             fox_k_norm=fox_k_norm, fox_f_bias=fox_f_bias, ssd_conv_w=ssd_conv_w, ssd_conv_b=ssd_conv_b,
             ssd_dt_bias=ssd_dt_bias, ssd_A_log=ssd_A_log, ssd_D=ssd_D, mix_out_norm=mix_out_norm,
             xa_q_norm=xa_q_norm, w_out_bf=w_out.astype(BF16), pm=pm, pc=pc)
    wg1, wu1, wd1 = ffn1_wg.astype(BF16), ffn1_wu.astype(BF16), ffn1_wd.astype(BF16)
    wg2, wu2, wd2 = ffn2_wg.astype(BF16), ffn2_wu.astype(BF16), ffn2_wd.astype(BF16)
    wq, wo = xa_wq.astype(BF16), xa_wo.astype(BF16)
    fg = final_norm[None, :]

    km_p, vm_p = _memkv(mem_prompt, mem_norm[:, None, :], xa_wk.astype(BF16), xa_wv.astype(BF16),
                        jnp.tile(xa_k_norm, (1, MEM_HEADS))[:, None, :])
    km_s, vm_s = cache_mem_k, cache_mem_v
    cache_kt = jnp.transpose(cache_fox_k, (0, 1, 3, 4, 2)).reshape(DEPTH, n_phys, ATTN_DIM, PAGE_SIZE)
    cache_vt = jnp.transpose(cache_fox_v, (0, 1, 3, 4, 2)).reshape(DEPTH, n_phys, ATTN_DIM, PAGE_SIZE)
    cache_lf_t = jnp.transpose(cache_fox_logf, (0, 1, 3, 2))

    zero_conv = jnp.zeros((b_p, SC_WIDTH - 1, CONV_DIM), F32)
    zero_ssd_conv = jnp.zeros((b_p, SSD_CONV - 1, XBC_DIM), F32)
    zero_ssd = jnp.zeros((b_p, SSD_HEADS, HEAD_DIM, SSD_STATE), F32)

    xp = x_prompt.reshape(b_p * s_p, D_MODEL)
    xs = x_sample.reshape(b_s * s_s, D_MODEL)
    st_p, st_s = [], []
    for l in range(DEPTH):
        lp = _layer_params(l, p)
        last = l == DEPTH - 1
        groups = []
        for x, nseq, seq, km, vm, cs, scs, ss, paged in (
                (xp, b_p, s_p, km_p, vm_p, zero_conv, zero_ssd_conv, zero_ssd, None),
                (xs, b_s, s_s, km_s, vm_s, state_conv[l], state_ssd_conv[l], state_ssd[l],
                 (page_table, cache_kt, cache_vt, cache_lf_t))):
            def finish(x, scb, cin, conv_pad, attn, yssd, nseq=nseq, seq=seq, km=km, vm=vm):
                mix = (p['sc_conv_w'][l], p['mix_out_norm'][l][None, :], p['w_out_bf'])
                xa = (xa_norm[l][None, :], wq, lp['xa_qg'], wo)
                f2 = (ffn2_norm[l][None, :], wg2, wu2, wd2, fg)
                if seq % ROW_TILE == 0:
                    x = _merge_xattn(x, scb, cin, conv_pad, attn, yssd, *mix, km, vm, *xa, l, nseq, seq)
                else:
                    x = _merge(x, scb, cin, conv_pad, attn, yssd, *mix, l, nseq, seq)
                    x = _xattn(x, km, vm, l, *xa, nseq, seq)
                return _ffn(x, *f2, l, last)

            x = _ffn(x, ffn1_norm[l][None, :], wg1, wu1, wd1, fg, l, False)
            x, st = _token_mix(l, x, p, lp, nseq, seq, cs, scs, ss, paged, finish)
            groups.append((x, st))
        (xp, sp), (xs, ss_) = groups
        st_p.append(sp)
        st_s.append(ss_)

    def stack(states, idx):
        return jnp.stack([s[idx] for s in states])

    return (xp.reshape(b_p, s_p, D_MODEL), xs.reshape(b_s, s_s, D_MODEL),
            stack(st_p, 0), stack(st_p, 1), stack(st_p, 2),
            km_p.reshape(DEPTH, b_p, MEM_LEN, MEM_HEADS, MEM_HEAD_DIM),
            vm_p.reshape(DEPTH, b_p, MEM_LEN, MEM_HEADS, MEM_HEAD_DIM),
            stack(st_p, 3), stack(st_p, 4), stack(st_p, 5),
            stack(st_s, 0), stack(st_s, 1), stack(st_s, 2), stack(st_s, 3), stack(st_s, 4), stack(st_s, 5))
```

```python
import functools

import numpy as np
import jax
import jax.numpy as jnp
from jax import lax
from jax.experimental import pallas as pl
from jax.experimental.pallas import tpu as pltpu

F32 = jnp.float32
BF16 = jnp.bfloat16

D_MODEL = 1024
DEPTH = 4
PAGE_SIZE = 128
HEAD_DIM = 64
CONV_DIM = 256
ATTN_DIM = 512
SSD_DIM = 256
FOX_HEADS = 8
SC_WIDTH = 3
SSD_HEADS = 4
SSD_STATE = 64
SSD_CONV = 4
SSD_CHUNK = 128
XBC_DIM = 512
FFN_DIM = 2816
MEM_LEN = 256
MEM_HEADS = 4
MEM_HEAD_DIM = 128
MEM_DIM = 512
RMS_EPS = 1e-6
LOG2E = 1.4426950408889634

CUMSUM_BLOCK = 256
LANES = 128
SUBLANES = 8
VMEM_LIMIT = 56 * 1024 * 1024
ROW_TILE = 512
PAGES_PER_STEP = 32
XATTN_SEQS_PER_STEP = 16
FOX_HEADS_PER_STEP = 4
SSD_CHUNKS_PER_STEP = 4
SSD_SEQS_PER_STEP = 8
PROJ_QKV = 3 * CONV_DIM + 3 * ATTN_DIM
F_LANE0 = 0
DT_LANE0 = FOX_HEADS
PAD_DIM = FOX_HEADS * LANES
BIAS_LANE0 = HEAD_DIM
N_SPLIT = 3


def _params(sem):
    return pltpu.CompilerParams(dimension_semantics=sem, vmem_limit_bytes=VMEM_LIMIT)


def _resident(shape):
    nd = len(shape)
    return pl.BlockSpec(shape, lambda *_: (0,) * nd, pipeline_mode=pl.Buffered(1))


def _resident_layer(shape, layer):
    nd = len(shape)
    return pl.BlockSpec((None,) + tuple(shape), lambda *_: (layer,) + (0,) * nd, pipeline_mode=pl.Buffered(1))


def _rms(x, g):
    ms = jnp.mean(x * x, axis=-1, keepdims=True)
    return x * lax.rsqrt(ms + RMS_EPS) * g


def _head_rms(x, g, head_dim):
    width = x.shape[-1]
    pieces = []
    for c in range(width // LANES):
        blk = x[:, c * LANES:(c + 1) * LANES]
        sq = blk * blk
        s_all = jnp.sum(sq, axis=-1, keepdims=True)
        if head_dim == LANES:
            ms = s_all * (1.0 / LANES)
        else:
            lo = lax.broadcasted_iota(jnp.int32, sq.shape, 1) < head_dim
            s_lo = jnp.sum(jnp.where(lo, sq, 0.0), axis=-1, keepdims=True)
            ms = jnp.where(lo, s_lo, s_all - s_lo) * (1.0 / head_dim)
        pieces.append(blk * lax.rsqrt(ms + RMS_EPS))
    return jnp.concatenate(pieces, axis=-1) * g


def _silu(x):
    return x * jax.nn.sigmoid(x)


def _softplus(x):
    return jnp.maximum(x, 0.0) + jnp.log1p(jnp.exp(-jnp.abs(x)))


def _split3(x):
    hi = x.astype(BF16)
    r1 = x - hi.astype(F32)
    mid = r1.astype(BF16)
    lo = (r1 - mid.astype(F32)).astype(BF16)
    return hi, mid, lo


def _dot_exact_rhs(x, m_bf16):
    hi, mid, lo = _split3(x)
    out = jnp.dot(hi, m_bf16, preferred_element_type=F32)
    out = out + jnp.dot(mid, m_bf16, preferred_element_type=F32)
    return out + jnp.dot(lo, m_bf16, preferred_element_type=F32)


def _dot_exact_lhs(m_bf16, x):
    hi, mid, lo = _split3(x)
    out = jnp.dot(m_bf16, hi, preferred_element_type=F32)
    out = out + jnp.dot(m_bf16, mid, preferred_element_type=F32)
    return out + jnp.dot(m_bf16, lo, preferred_element_type=F32)


def _lower_tri(n, seq=None):
    r = lax.broadcasted_iota(jnp.int32, (n, n), 0)
    c = lax.broadcasted_iota(jnp.int32, (n, n), 1)
    keep = c <= r
    if seq is not None and seq < n:
        shift = jnp.int32(seq.bit_length() - 1)
        keep = keep & (lax.shift_right_logical(r, shift) == lax.shift_right_logical(c, shift))
    return keep.astype(BF16)


def _upper_tri(n):
    r = lax.broadcasted_iota(jnp.int32, (n, n), 0)
    c = lax.broadcasted_iota(jnp.int32, (n, n), 1)
    return (r <= c).astype(BF16)


def _nt(a, b):
    return lax.dot_general(a, b, (((1,), (1,)), ((), ())), preferred_element_type=F32)


def _tn(a, b):
    return lax.dot_general(a, b, (((0,), (0,)), ((), ())), preferred_element_type=F32)


def _row_tile(x, reps):
    return x if reps == 1 else jnp.concatenate([x] * reps, axis=0)


def _lane_tile(x, reps):
    return x if reps == 1 else jnp.concatenate([x] * reps, axis=-1)


def _ffn_body(x, g_ref, wg_ref, wu_ref, wd_ref, fg_ref, final):
    h = _rms(x, g_ref[...]).astype(BF16)
    a = jnp.dot(h, wg_ref[...], preferred_element_type=F32)
    b = jnp.dot(h, wu_ref[...], preferred_element_type=F32)
    t = (_silu(a) * b).astype(BF16)
    out = x + 0.5 * jnp.dot(t, wd_ref[...], preferred_element_type=F32)
    return _rms(out, fg_ref[...]) if final else out


def _ffn_kernel(x_ref, g_ref, wg_ref, wu_ref, wd_ref, fg_ref, o_ref, *, final):
    o_ref[...] = _ffn_body(x_ref[...], g_ref, wg_ref, wu_ref, wd_ref, fg_ref, final)


def _ffn(x, g, wg, wu, wd, fg, layer, final):
    n = x.shape[0]
    tm = min(ROW_TILE, n)
    return pl.pallas_call(
        functools.partial(_ffn_kernel, final=final),
        out_shape=jax.ShapeDtypeStruct((n, D_MODEL), F32),
        grid=(n // tm,),
        in_specs=[
            pl.BlockSpec((tm, D_MODEL), lambda i: (i, 0)),
            _resident((1, D_MODEL)),
            _resident_layer((D_MODEL, FFN_DIM), layer),
            _resident_layer((D_MODEL, FFN_DIM), layer),
            _resident_layer((FFN_DIM, D_MODEL), layer),
            _resident((1, D_MODEL)),
        ],
        out_specs=pl.BlockSpec((tm, D_MODEL), lambda i: (i, 0)),
        compiler_params=_params(("arbitrary",)),
        name="ffn",
    )(x, g, wg, wu, wd, fg)


def _expand_heads(x, fill):
    lo = lax.broadcasted_iota(jnp.int32, (x.shape[0], LANES), 1) < HEAD_DIM
    blocks = []
    for c in range(ATTN_DIM // LANES):
        blk = x[:, c * LANES:(c + 1) * LANES]
        for half, src in enumerate((blk, pltpu.roll(blk, HEAD_DIM, axis=1))):
            h = 2 * c + half
            blocks.append(jnp.where(lo, src, fill[:, h * LANES:(h + 1) * LANES]))
    return jnp.concatenate(blocks, axis=-1).astype(BF16)


def _inproj_kernel(x_ref, g_ref, w_ref, wzx_ref, wfd_ref, qg_ref, kg_ref, fb_ref, pm_ref, pc_ref,
                   scb_ref, cin_ref, qa_ref, kn_ref, ka_ref, v_ref, va_ref,
                   z_ref, xbc_ref, fdt_ref, *rest, tm, seq, dim_major):
    carry_ref = rest[-1]
    h = _rms(x_ref[...], g_ref[...]).astype(BF16)

    def store_kv(ref, val):
        ref[...] = val.T.reshape(FOX_HEADS, HEAD_DIM, tm) if dim_major else val

    def proj(a, b, ref=w_ref):
        return _nt(h, ref[a:b, :])

    o = 0
    scb_ref[...] = proj(o, o + CONV_DIM)
    o += CONV_DIM
    cin_ref[...] = proj(o, o + CONV_DIM) * proj(o + CONV_DIM, o + 2 * CONV_DIM)
    o += 2 * CONV_DIM
    q = _head_rms(proj(o, o + ATTN_DIM), qg_ref[...], HEAD_DIM) * (HEAD_DIM ** -0.5 * LOG2E)
    o += ATTN_DIM
    k = _head_rms(proj(o, o + ATTN_DIM), kg_ref[...], HEAD_DIM)
    store_kv(kn_ref, k)
    o += ATTN_DIM
    v = proj(o, o + ATTN_DIM)
    store_kv(v_ref, v)
    z_ref[...] = proj(0, SSD_DIM, wzx_ref)
    xbc_ref[...] = proj(SSD_DIM, SSD_DIM + XBC_DIM, wzx_ref)
    u = proj(0, LANES, wfd_ref) + fb_ref[...]
    lane = lax.broadcasted_iota(jnp.int32, u.shape, 1)
    fdt = jnp.where(lane < DT_LANE0, -_softplus(-u), _softplus(u))
    fdt_ref[...] = fdt
    if dim_major:
        rest[0][...] = fdt.T[0:FOX_HEADS, :]

    @pl.when(pl.program_id(0) % max(seq // tm, 1) == 0)
    def _():
        carry_ref[...] = jnp.zeros_like(carry_ref)

    lf = jnp.where(lane < FOX_HEADS, fdt, 0.0) * LOG2E
    blk = min(tm, CUMSUM_BLOCK)
    tri = _lower_tri(blk, seq)
    carry = carry_ref[0:1, :]
    c_parts = []
    for r in range(tm // blk):
        c_blk = _dot_exact_lhs(tri, lf[r * blk:(r + 1) * blk, :])
        if seq >= blk and (r == 0 or (r * blk) % seq != 0):
            c_blk = c_blk + carry
        carry = c_blk[blk - 1:blk, :]
        c_parts.append(c_blk)
    c = jnp.concatenate(c_parts, axis=0)
    carry_ref[...] = jnp.broadcast_to(carry, carry_ref.shape)
    hi, mid, lo = _split3(c)
    packed = (hi.astype(F32) + pltpu.roll(mid.astype(F32), FOX_HEADS, axis=1)
              + pltpu.roll(lo.astype(F32), 2 * FOX_HEADS, axis=1)).astype(BF16)
    both = jnp.dot(packed, pm_ref[...], preferred_element_type=F32)
    lane_blk = _lane_tile(lax.broadcasted_iota(jnp.int32, (tm, LANES), 1), FOX_HEADS)
    q_side = lane_blk < BIAS_LANE0 + N_SPLIT
    qa_ref[...] = _expand_heads(q, jnp.where(q_side, both, pc_ref[:, 0:PAD_DIM]))
    ka_ref[...] = _expand_heads(k, jnp.where(q_side, pc_ref[:, PAD_DIM:2 * PAD_DIM], both))
    va_ref[...] = _expand_heads(v, pc_ref[:, 2 * PAD_DIM:3 * PAD_DIM])


def _bias_placement():
    pm = np.zeros((LANES, PAD_DIM), np.float32)
    pc = np.zeros((1, 3 * PAD_DIM), np.float32)
    for h in range(FOX_HEADS):
        for part in range(N_SPLIT):
            src = part * FOX_HEADS + h
            pm[src, h * LANES + BIAS_LANE0 + part] = 1.0
            pm[src, h * LANES + BIAS_LANE0 + N_SPLIT + part] = -1.0
            pc[0, h * LANES + BIAS_LANE0 + N_SPLIT + part] = 1.0
            pc[0, PAD_DIM + h * LANES + BIAS_LANE0 + part] = 1.0
        pc[0, 2 * PAD_DIM + h * LANES + HEAD_DIM] = 1.0
    return jnp.asarray(pm, BF16), jnp.asarray(pc, F32)


def _inproj(x, g, w, qg, kg, fb, pm, pc, layer, seq):
    w_t, w_zx, w_fd = w
    n = x.shape[0]
    tm = min(ROW_TILE, n)
    widths = [(CONV_DIM, F32), (CONV_DIM, F32), (PAD_DIM, BF16), (ATTN_DIM, F32), (PAD_DIM, BF16),
              (ATTN_DIM, F32), (PAD_DIM, BF16), (SSD_DIM, F32), (XBC_DIM, F32), (LANES, F32)]
    out_shape = [jax.ShapeDtypeStruct((n, w_), dt) for w_, dt in widths]
    out_specs = [pl.BlockSpec((tm, w_), lambda i: (i, 0)) for w_, _ in widths]
    dim_major = seq >= tm
    if dim_major:
        tps = seq // tm
        kv_shape = jax.ShapeDtypeStruct((n // seq, FOX_HEADS, HEAD_DIM, seq), F32)
        kv_spec = pl.BlockSpec((None, FOX_HEADS, HEAD_DIM, tm), lambda i: (i // tps, 0, 0, i % tps))
        out_shape[3], out_shape[5] = kv_shape, kv_shape
        out_specs[3], out_specs[5] = kv_spec, kv_spec
        out_shape.append(jax.ShapeDtypeStruct((n // seq, FOX_HEADS, seq), F32))
        out_specs.append(pl.BlockSpec((None, FOX_HEADS, tm), lambda i: (i // tps, 0, i % tps)))
    return pl.pallas_call(
        functools.partial(_inproj_kernel, tm=tm, seq=seq, dim_major=dim_major),
        out_shape=out_shape,
        grid=(n // tm,),
        in_specs=[
            pl.BlockSpec((tm, D_MODEL), lambda i: (i, 0)),
            _resident((1, D_MODEL)),
            _resident_layer((PROJ_QKV, D_MODEL), layer),
            _resident_layer((SSD_DIM + XBC_DIM, D_MODEL), layer),
            _resident_layer((LANES, D_MODEL), layer),
            _resident((1, ATTN_DIM)),
            _resident((1, ATTN_DIM)),
            _resident((1, LANES)),
            _resident((LANES, PAD_DIM)),
            _resident((1, 3 * PAD_DIM)),
        ],
        out_specs=out_specs,
        scratch_shapes=[pltpu.VMEM((SUBLANES, LANES), F32)],
        compiler_params=_params(("arbitrary",)),
        name="inproj",
    )(x, g, w_t, w_zx, w_fd, qg, kg, fb, pm, pc)


def _fox_kernel(q_ref, k_ref, v_ref, o_ref, m_ref, acc_ref, *, tq):
    i = pl.program_id(2)
    m_ref[...] = jnp.full(m_ref.shape, -jnp.inf, F32)
    acc_ref[...] = jnp.zeros_like(acc_ref)

    def tile(start, width, masked):
        rows = pl.ds(pl.multiple_of(start, tq), width)
        for hh in range(FOX_HEADS_PER_STEP):
            sl = slice(hh * LANES, (hh + 1) * LANES)
            s = _nt(q_ref[:, sl], k_ref[rows, sl])
            if masked:
                row = lax.broadcasted_iota(jnp.int32, s.shape, 0)
                col = lax.broadcasted_iota(jnp.int32, s.shape, 1)
                s = jnp.where(col <= row + (width - tq), s, -jnp.inf)
            m_old = m_ref[hh]
            m_new = jnp.maximum(m_old, jnp.max(s, axis=-1, keepdims=True))
            p = jnp.exp2(s - _lane_tile(m_new, width // LANES))
            acc_ref[hh] = jnp.exp2(m_old - m_new) * acc_ref[hh] + jnp.dot(
                p.astype(BF16), v_ref[rows, sl], preferred_element_type=F32)
            m_ref[hh] = m_new

    def body(j, carry):
        tile(j * (4 * tq), 4 * tq, False)
        return carry

    odd = lax.bitwise_and(i, 1) == 1
    n_wide = lax.shift_right_logical(i, 1) - jnp.where(odd | (i == 0), 0, 1)
    n_quad = lax.shift_right_logical(n_wide, 1)
    lax.fori_loop(0, n_quad, body, 0)

    @pl.when(lax.bitwise_and(n_wide, 1) == 1)
    def _():
        tile(n_quad * (4 * tq), 2 * tq, False)

    @pl.when(odd)
    def _():
        tile((i - 1) * tq, 2 * tq, True)

    @pl.when(jnp.logical_not(odd) & (i > 0))
    def _():
        tile((i - 2) * tq, 3 * tq, True)

    @pl.when(i == 0)
    def _():
        tile(0, tq, True)
    outs = []
    for hh in range(FOX_HEADS_PER_STEP):
        a = acc_ref[hh]
        outs.append(a[:, 0:HEAD_DIM] / a[:, HEAD_DIM:HEAD_DIM + 1])
    o_ref[...] = jnp.concatenate(outs, axis=-1)


def _fox_prompt(qa, ka, va, nseq, seq):
    tq = min(ROW_TILE, seq)
    nq = seq // tq
    nh = FOX_HEADS_PER_STEP
    return pl.pallas_call(
        functools.partial(_fox_kernel, tq=tq),
        out_shape=jax.ShapeDtypeStruct((nseq * seq, ATTN_DIM), F32),
        grid=(nseq, FOX_HEADS // nh, nq),
        in_specs=[
            pl.BlockSpec((tq, nh * LANES), lambda b, h, i: (b * nq + i, h)),
            pl.BlockSpec((seq, nh * LANES), lambda b, h, i: (b, h)),
            pl.BlockSpec((seq, nh * LANES), lambda b, h, i: (b, h)),
        ],
        out_specs=pl.BlockSpec((tq, nh * HEAD_DIM), lambda b, h, i: (b * nq + i, h)),
        scratch_shapes=[pltpu.VMEM((nh, tq, LANES), F32), pltpu.VMEM((nh, tq, LANES), F32)],
        compiler_params=_params(("arbitrary",) * 3),
        name="fox_prompt",
    )(qa, ka, va)


def _fox_paged_kernel(pt_ref, q_ref, kn_ref, vn_ref, lfn_ref, *rest, tokens):
    npg = PAGES_PER_STEP
    k_refs = rest[:npg]
    v_refs = rest[npg:2 * npg]
    lf_refs = rest[2 * npg:3 * npg]
    o_ref, m_ref, l_ref, acc_ref, carry_ref, c_ref = rest[3 * npg:]
    g = pl.program_id(1)
    rows = FOX_HEADS * tokens

    @pl.when(g == 0)
    def _():
        m_ref[...] = jnp.full(m_ref.shape, -jnp.inf, F32)
        l_ref[...] = jnp.zeros_like(l_ref)
        acc_ref[...] = jnp.zeros_like(acc_ref)
        carry_ref[...] = jnp.zeros_like(carry_ref)

    q = q_ref[...]
    tri = _upper_tri(PAGE_SIZE)

    def update(k_list, v_list, lf_list):
        n = len(k_list)
        nr = n * FOX_HEADS
        parts = _split3(jnp.concatenate(lf_list, axis=0) * LOG2E)

        def times(m_bf16):
            return sum(jnp.dot(x, m_bf16, preferred_element_type=F32) for x in parts)

        c = times(tri) + _row_tile(carry_ref[...], n)
        if n > 1:
            tot = times(jnp.ones((PAGE_SIZE, PAGE_SIZE), BF16))
            r = lax.broadcasted_iota(jnp.int32, (nr, nr), 0)
            col = lax.broadcasted_iota(jnp.int32, (nr, nr), 1)
            same_head = lax.bitwise_and(r, FOX_HEADS - 1) == lax.bitwise_and(col, FOX_HEADS - 1)
            earlier = jnp.where(col < r, jnp.where(same_head, 1.0, 0.0), 0.0).astype(BF16)
            off = _dot_exact_lhs(earlier, tot)
            c = c + off
            carry_ref[...] = carry_ref[...] + off[nr - FOX_HEADS:nr, :] + tot[nr - FOX_HEADS:nr, :]
        c_ref[0:nr, :] = c
        s_list = []
        for idx in range(n):
            c_rows = jnp.concatenate(
                [jnp.broadcast_to(c_ref[idx * FOX_HEADS + h:idx * FOX_HEADS + h + 1, :], (tokens, PAGE_SIZE))
                 for h in range(FOX_HEADS)], axis=0)
            s_list.append(jnp.dot(q, k_list[idx].astype(BF16), preferred_element_type=F32) - c_rows)
        s = jnp.concatenate(s_list, axis=-1)
        m_old = m_ref[...]
        m_new = jnp.maximum(m_old, jnp.max(s, axis=-1, keepdims=True))
        alpha = jnp.exp2(m_old - m_new)
        p = jnp.exp2(s - _lane_tile(m_new, n))
        l_ref[...] = alpha * l_ref[...] + jnp.sum(p, axis=-1, keepdims=True)
        pv = _nt(p[:, 0:PAGE_SIZE].astype(BF16), v_list[0].astype(BF16))
        for idx in range(1, n):
            pv = pv + _nt(p[:, idx * PAGE_SIZE:(idx + 1) * PAGE_SIZE].astype(BF16), v_list[idx].astype(BF16))
        acc_ref[...] = _lane_tile(alpha, ATTN_DIM // LANES) * acc_ref[...] + pv
        m_ref[...] = m_new

    update([r[...] for r in k_refs], [r[...] for r in v_refs], [r[...] for r in lf_refs])

    @pl.when(g == pl.num_programs(1) - 1)
    def _():
        c_new = _dot_exact_rhs(lfn_ref[...] * LOG2E, _upper_tri(tokens)) + carry_ref[:, 0:tokens]
        c_ref[0:FOX_HEADS, 0:tokens] = c_new
        c_rows = jnp.concatenate(
            [jnp.broadcast_to(c_ref[h:h + 1, 0:tokens], (tokens, tokens)) for h in range(FOX_HEADS)], axis=0)
        s = _nt(q, kn_ref[...].astype(BF16)) - c_rows
        t_idx = lax.bitwise_and(lax.broadcasted_iota(jnp.int32, s.shape, 0), tokens - 1)
        s = jnp.where(lax.broadcasted_iota(jnp.int32, s.shape, 1) <= t_idx, s, -jnp.inf)
        m_old = m_ref[...]
        m_new = jnp.maximum(m_old, jnp.max(s, axis=-1, keepdims=True))
        alpha = jnp.exp2(m_old - m_new)
        p = jnp.exp2(s - m_new[:, 0:tokens])
        l_new = alpha * l_ref[...] + jnp.sum(p, axis=-1, keepdims=True)
        acc = _lane_tile(alpha, ATTN_DIM // LANES) * acc_ref[...] + jnp.dot(
            p.astype(BF16), vn_ref[...].astype(BF16), preferred_element_type=F32)
        full = acc / _lane_tile(l_new, ATTN_DIM // LANES)
        lane_head = lax.shift_right_logical(
            lax.broadcasted_iota(jnp.int32, (tokens, ATTN_DIM), 1), jnp.int32(HEAD_DIM.bit_length() - 1))
        out = jnp.zeros((tokens, ATTN_DIM), F32)
        for h in range(FOX_HEADS):
            out = out + jnp.where(lane_head == h, full[h * tokens:(h + 1) * tokens, :], 0.0)
        o_ref[...] = out


def _fox_paged(layer, page_table, qbd, kn_t, vn_t, lfn_t, cache_kt, cache_vt, cache_lf_t, nseq, tokens):
    npg = PAGES_PER_STEP
    n_groups = page_table.shape[1] // npg
    rows = FOX_HEADS * tokens

    def page_spec(width2, n):
        return pl.BlockSpec((None, None, width2, PAGE_SIZE),
                            lambda b, g, pt: (layer, pt[b, g * npg + n], 0, 0))

    in_specs = [
        pl.BlockSpec((None, rows, ATTN_DIM), lambda b, g, pt: (b, 0, 0)),
        pl.BlockSpec((None, tokens, ATTN_DIM), lambda b, g, pt: (b, 0, 0)),
        pl.BlockSpec((None, tokens, ATTN_DIM), lambda b, g, pt: (b, 0, 0)),
        pl.BlockSpec((None, FOX_HEADS, tokens), lambda b, g, pt: (b, 0, 0)),
    ]
    in_specs += [page_spec(ATTN_DIM, n) for n in range(npg)]
    in_specs += [page_spec(ATTN_DIM, n) for n in range(npg)]
    in_specs += [page_spec(FOX_HEADS, n) for n in range(npg)]
    grid_spec = pltpu.PrefetchScalarGridSpec(
        num_scalar_prefetch=1,
        grid=(nseq, n_groups),
        in_specs=in_specs,
        out_specs=pl.BlockSpec((None, tokens, ATTN_DIM), lambda b, g, pt: (b, 0, 0)),
        scratch_shapes=[pltpu.VMEM((rows, LANES), F32), pltpu.VMEM((rows, LANES), F32),
                        pltpu.VMEM((rows, ATTN_DIM), F32), pltpu.VMEM((FOX_HEADS, LANES), F32),
                        pltpu.VMEM((npg * FOX_HEADS, LANES), F32)],
    )
    return pl.pallas_call(
        functools.partial(_fox_paged_kernel, tokens=tokens),
        out_shape=jax.ShapeDtypeStruct((nseq, tokens, ATTN_DIM), F32),
        grid_spec=grid_spec,
        compiler_params=_params(("arbitrary", "arbitrary")),
        name="fox_paged",
    )(page_table, qbd, kn_t, vn_t, lfn_t,
      *([cache_kt] * npg), *([cache_vt] * npg), *([cache_lf_t] * npg))


def _ssd_kernel(xbc_ref, halo_ref, st_ref, fdt_ref, z_ref, h0_ref,
                cw_ref, cb_ref, av_ref, dv_ref,
                y_ref, hout_ref, buf_ref, h_ref, *, lc, rows):
    j = pl.program_id(1)
    bs = st_ref.shape[0]

    @pl.when(j == 0)
    def _():
        h_ref[...] = h0_ref[...]
        buf_ref[:, 0:SUBLANES, :] = st_ref[...]

    @pl.when(j > 0)
    def _():
        buf_ref[0, 0:SUBLANES, :] = halo_ref[...]

    buf_ref[:, SUBLANES:SUBLANES + rows, :] = xbc_ref[...].reshape(bs, rows, XBC_DIM)
    base = SUBLANES - (SSD_CONV - 1)
    u_all = buf_ref[:, base:base + rows, :] * cw_ref[0:1, :]
    for k in range(1, SSD_CONV):
        u_all = u_all + buf_ref[:, base + k:base + k + rows, :] * cw_ref[k:k + 1, :]
    u_all = _silu(u_all + cb_ref[...]).reshape(bs * rows, XBC_DIM)

    tri_lo = _lower_tri(lc)
    tri_up = _upper_tri(lc)
    pick = (lax.broadcasted_iota(jnp.int32, (SUBLANES, LANES), 1)
            == lax.broadcasted_iota(jnp.int32, (SUBLANES, LANES), 0) + DT_LANE0).astype(BF16)

    def rows_of(x):
        return sum(_nt(pick, part) for part in _split3(x))

    row = lax.broadcasted_iota(jnp.int32, (lc, lc), 0)
    col = lax.broadcasted_iota(jnp.int32, (lc, lc), 1)
    causal = col <= row
    dvec = dv_ref[...]

    for s, c in ((s, c) for s in range(bs) for c in range(rows // lc)):
        tok = slice(s * rows + c * lc, s * rows + (c + 1) * lc)
        u = u_all[tok, :]
        z = z_ref[tok, :]
        fdt = fdt_ref[tok, :]
        da = fdt * av_ref[...]
        acum = _dot_exact_lhs(tri_lo, da)
        dtt = rows_of(fdt)
        acum_t = _dot_exact_rhs(rows_of(da), tri_up)
        outs = []
        for h in range(SSD_HEADS):
            grp = h // (SSD_HEADS // 2)
            xh = u[:, h * HEAD_DIM:(h + 1) * HEAD_DIM]
            bh = u[:, SSD_DIM + grp * SSD_STATE:SSD_DIM + (grp + 1) * SSD_STATE]
            ch = u[:, SSD_DIM + 2 * SSD_STATE + grp * SSD_STATE:SSD_DIM + 2 * SSD_STATE + (grp + 1) * SSD_STATE]
            lane = DT_LANE0 + h
            a_col = acum[:, lane:lane + 1]
            dt_col = fdt[:, lane:lane + 1]
            a_row = acum_t[h:h + 1, :]
            dt_row = dtt[h:h + 1, :]
            a_last = acum[lc - 1:lc, lane:lane + 1]
            decay = jnp.exp(jnp.where(causal, a_col - a_row, -jnp.inf))
            xb = xh.astype(BF16)
            bb = bh.astype(BF16)
            cb16 = ch.astype(BF16)
            w = _nt(cb16, bb) * decay * dt_row
            h_prev = h_ref[s, h]
            y = jnp.dot(w.astype(BF16), xb, preferred_element_type=F32)
            y = y + _nt(cb16, h_prev.astype(BF16)) * jnp.exp(a_col)
            w_end = jnp.exp(a_last - a_col) * dt_col
            h_ref[s, h] = h_prev * jnp.exp(a_last) + _tn((xh * w_end).astype(BF16), bb)
            sl = slice(h * HEAD_DIM, (h + 1) * HEAD_DIM)
            outs.append((y + dvec[:, sl] * xh) * _silu(z[:, sl]))
        y_ref[tok, :] = jnp.concatenate(outs, axis=-1)

    @pl.when(j == pl.num_programs(1) - 1)
    def _():
        hout_ref[...] = h_ref[...]


def _ssd(xbc, st_pad, fdt, z, h0, cw, cb, av, dv, nseq, seq):
    lc = min(SSD_CHUNK, seq)
    rows = min(SSD_CHUNKS_PER_STEP * lc, seq)
    nc = seq // rows
    bs = 1 if nc > 1 else min(nseq, SSD_SEQS_PER_STEP)
    nrows = bs * rows
    hb = nrows // SUBLANES
    state_spec = pl.BlockSpec((bs, SSD_HEADS, HEAD_DIM, SSD_STATE), lambda b, j: (b, 0, 0, 0))
    return pl.pallas_call(
        functools.partial(_ssd_kernel, lc=lc, rows=rows),
        out_shape=[jax.ShapeDtypeStruct((nseq * seq, SSD_DIM), F32),
                   jax.ShapeDtypeStruct((nseq, SSD_HEADS, HEAD_DIM, SSD_STATE), F32)],
        grid=(nseq // bs, nc),
        in_specs=[
            pl.BlockSpec((nrows, XBC_DIM), lambda b, j: (b * nc + j, 0)),
            pl.BlockSpec((SUBLANES, XBC_DIM), lambda b, j: (jnp.maximum((b * nc + j) * hb - 1, 0), 0)),
            pl.BlockSpec((bs, SUBLANES, XBC_DIM), lambda b, j: (b, 0, 0)),
            pl.BlockSpec((nrows, LANES), lambda b, j: (b * nc + j, 0)),
            pl.BlockSpec((nrows, SSD_DIM), lambda b, j: (b * nc + j, 0)),
            state_spec,
            _resident((SSD_CONV, XBC_DIM)),
            _resident((1, XBC_DIM)),
            _resident((1, LANES)),
            _resident((1, SSD_DIM)),
        ],
        out_specs=[pl.BlockSpec((nrows, SSD_DIM), lambda b, j: (b * nc + j, 0)), state_spec],
        scratch_shapes=[pltpu.VMEM((bs, SUBLANES + rows, XBC_DIM), F32),
                        pltpu.VMEM((bs, SSD_HEADS, HEAD_DIM, SSD_STATE), F32)],
        compiler_params=_params(("arbitrary", "arbitrary")),
        name="ssd",
    )(xbc, xbc, st_pad, fdt, z, h0, cw, cb, av, dv)


def _merge_body(x_ref, scb_ref, cin_ref, halo_ref, st_ref, attn_ref, yssd_ref,
                cw_ref, g_ref, w_ref, buf_ref, tl):
    j = pl.program_id(1)
    bs = st_ref.shape[0]

    @pl.when(j == 0)
    def _():
        buf_ref[:, 0:SUBLANES, :] = st_ref[...]

    @pl.when(j > 0)
    def _():
        buf_ref[0, 0:SUBLANES, :] = halo_ref[...]

    buf_ref[:, SUBLANES:SUBLANES + tl, :] = cin_ref[...].reshape(bs, tl, CONV_DIM)
    base = SUBLANES - (SC_WIDTH - 1)
    y = buf_ref[:, base:base + tl, :] * cw_ref[0:1, :]
    for k in range(1, SC_WIDTH):
        y = y + buf_ref[:, base + k:base + k + tl, :] * cw_ref[k:k + 1, :]
    y = y.reshape(bs * tl, CONV_DIM)
    cat = jnp.concatenate([scb_ref[...] * y, attn_ref[...], yssd_ref[...]], axis=-1)
    cat = _head_rms(cat, g_ref[...], HEAD_DIM).astype(BF16)
    return x_ref[...] + jnp.dot(cat, w_ref[...], preferred_element_type=F32)


def _merge_kernel(x_ref, scb_ref, cin_ref, halo_ref, st_ref, attn_ref, yssd_ref,
                  cw_ref, g_ref, w_ref, o_ref, buf_ref, *, tl):
    o_ref[...] = _merge_body(x_ref, scb_ref, cin_ref, halo_ref, st_ref, attn_ref, yssd_ref,
                             cw_ref, g_ref, w_ref, buf_ref, tl)


def _merge(x, scb, cin, st_pad, attn, yssd, cw, g, w, layer, nseq, seq):
    tl = min(ROW_TILE, seq)
    nt = seq // tl
    bs = 1 if nt > 1 else min(nseq, ROW_TILE // tl)
    nrows = bs * tl
    hb = nrows // SUBLANES

    def rows(width):
        return pl.BlockSpec((nrows, width), lambda b, j: (b * nt + j, 0))

    return pl.pallas_call(
        functools.partial(_merge_kernel, tl=tl),
        out_shape=jax.ShapeDtypeStruct((nseq * seq, D_MODEL), F32),
        grid=(nseq // bs, nt),
        in_specs=[
            rows(D_MODEL), rows(CONV_DIM), rows(CONV_DIM),
            pl.BlockSpec((SUBLANES, CONV_DIM), lambda b, j: (jnp.maximum((b * nt + j) * hb - 1, 0), 0)),
            pl.BlockSpec((bs, SUBLANES, CONV_DIM), lambda b, j: (b, 0, 0)),
            rows(ATTN_DIM), rows(SSD_DIM),
            _resident((SC_WIDTH, CONV_DIM)),
            _resident((1, D_MODEL)),
            _resident_layer((D_MODEL, D_MODEL), layer),
        ],
        out_specs=rows(D_MODEL),
        scratch_shapes=[pltpu.VMEM((bs, SUBLANES + tl, CONV_DIM), F32)],
        compiler_params=_params(("arbitrary", "arbitrary")),
        name="merge",
    )(x, scb, cin, cin, st_pad, attn, yssd, cw, g, w)


def _memkv_kernel(mem_ref, g_ref, wk_ref, wv_ref, kg_ref, km_ref, vm_ref):
    m = _rms(mem_ref[...], g_ref[...]).astype(BF16)
    km_ref[...] = _head_rms(jnp.dot(m, wk_ref[...], preferred_element_type=F32), kg_ref[...], MEM_HEAD_DIM)
    vm_ref[...] = jnp.dot(m, wv_ref[...], preferred_element_type=F32)


def _memkv(mem, g, wk, wv, kg):
    nb = mem.shape[0]
    out = jax.ShapeDtypeStruct((DEPTH, nb, MEM_LEN, MEM_DIM), F32)
    return pl.pallas_call(
        _memkv_kernel,
        out_shape=[out, out],
        grid=(DEPTH, nb),
        in_specs=[
            pl.BlockSpec((None, MEM_LEN, D_MODEL), lambda l, b: (b, 0, 0)),
            pl.BlockSpec((None, 1, D_MODEL), lambda l, b: (l, 0, 0)),
            pl.BlockSpec((None, D_MODEL, MEM_DIM), lambda l, b: (l, 0, 0)),
            pl.BlockSpec((None, D_MODEL, MEM_DIM), lambda l, b: (l, 0, 0)),
            pl.BlockSpec((None, 1, MEM_DIM), lambda l, b: (l, 0, 0)),
        ],
        out_specs=[pl.BlockSpec((None, None, MEM_LEN, MEM_DIM), lambda l, b: (l, b, 0, 0))] * 2,
        compiler_params=_params(("arbitrary", "arbitrary")),
        name="mem_kv",
    )(mem, g, wk, wv, kg)


def _xattn_body(x, km_ref, vm_ref, g_ref, wq_ref, qg_ref, wo_ref, q_ref, att_ref, bs, tl):
    h = _rms(x, g_ref[...]).astype(BF16)
    q = _head_rms(jnp.dot(h, wq_ref[...], preferred_element_type=F32), qg_ref[...], MEM_HEAD_DIM)
    q = q * (MEM_HEAD_DIM ** -0.5)
    for hh in range(MEM_HEADS):
        q_ref[hh] = q[:, hh * MEM_HEAD_DIM:(hh + 1) * MEM_HEAD_DIM]

    def softmax(sc):
        p = jnp.exp(sc - jnp.max(sc, axis=-1, keepdims=True))
        return (p / jnp.sum(p, axis=-1, keepdims=True)).astype(BF16)

    for s in range(bs):
        rows = slice(s * tl, (s + 1) * tl)
        if len(km_ref.shape) == 4:
            km = km_ref[s].reshape(MEM_LEN * MEM_HEADS, MEM_HEAD_DIM).astype(BF16)
            vm = vm_ref[s].reshape(MEM_LEN * MEM_HEADS, MEM_HEAD_DIM).astype(BF16)
            qs = jnp.concatenate([q_ref[hh, rows, :] for hh in range(MEM_HEADS)], axis=0).astype(BF16)
            sc = _nt(qs, km)
            row_head = lax.shift_right_logical(lax.broadcasted_iota(jnp.int32, sc.shape, 0),
                                               jnp.int32(tl.bit_length() - 1))
            col_head = lax.bitwise_and(lax.broadcasted_iota(jnp.int32, sc.shape, 1), MEM_HEADS - 1)
            out = jnp.dot(softmax(jnp.where(row_head == col_head, sc, -jnp.inf)), vm,
                          preferred_element_type=F32)
            for hh in range(MEM_HEADS):
                att_ref[hh, rows, :] = out[hh * tl:(hh + 1) * tl, :]
        else:
            for hh in range(MEM_HEADS):
                sl = slice(hh * MEM_HEAD_DIM, (hh + 1) * MEM_HEAD_DIM)
                sc = _nt(q_ref[hh, rows, :].astype(BF16), km_ref[s, :, sl].astype(BF16))
                att_ref[hh, rows, :] = jnp.dot(softmax(sc), vm_ref[s, :, sl].astype(BF16),
                                               preferred_element_type=F32)

    att = jnp.concatenate([att_ref[hh] for hh in range(MEM_HEADS)], axis=-1).astype(BF16)
    return x + jnp.dot(att, wo_ref[...], preferred_element_type=F32)


def _xattn_kernel(x_ref, km_ref, vm_ref, g_ref, wq_ref, qg_ref, wo_ref, o_ref, q_ref, att_ref, *, bs, tl):
    o_ref[...] = _xattn_body(x_ref[...], km_ref, vm_ref, g_ref, wq_ref, qg_ref, wo_ref, q_ref, att_ref, bs, tl)


def _xattn(x, km, vm, layer, g, wq, qg, wo, nseq, seq):
    tl = min(ROW_TILE, seq)
    nt = seq // tl
    bs = 1 if nt > 1 else min(nseq, XATTN_SEQS_PER_STEP)
    rows = bs * tl
    mem_zeros = (0,) * (km.ndim - 2)
    mem_spec = pl.BlockSpec((None, bs) + tuple(km.shape[2:]), lambda b, j: (layer, b) + mem_zeros)
    row_spec = pl.BlockSpec((rows, D_MODEL), lambda b, j: (b * nt + j, 0))
    return pl.pallas_call(
        functools.partial(_xattn_kernel, bs=bs, tl=tl),
        out_shape=jax.ShapeDtypeStruct((nseq * seq, D_MODEL), F32),
        grid=(nseq // bs, nt),
        in_specs=[
            row_spec, mem_spec, mem_spec,
            _resident((1, D_MODEL)),
            _resident_layer((D_MODEL, MEM_DIM), layer),
            _resident((1, MEM_DIM)),
            _resident_layer((MEM_DIM, D_MODEL), layer),
        ],
        out_specs=row_spec,
        scratch_shapes=[pltpu.VMEM((MEM_HEADS, rows, MEM_HEAD_DIM), F32),
                        pltpu.VMEM((MEM_HEADS, rows, MEM_HEAD_DIM), F32)],
        compiler_params=_params(("arbitrary", "arbitrary")),
        name="xattn",
    )(x, km, vm, g, wq, qg, wo)


def _merge_xattn_kernel(x_ref, scb_ref, cin_ref, halo_ref, st_ref, attn_ref, yssd_ref, cw_ref, go_ref,
                        wout_ref, km_ref, vm_ref, gx_ref, wq_ref, qg_ref, wo_ref,
                        o_ref, buf_ref, q_ref, att_ref, *, tl):
    x = _merge_body(x_ref, scb_ref, cin_ref, halo_ref, st_ref, attn_ref, yssd_ref,
                    cw_ref, go_ref, wout_ref, buf_ref, tl)
    o_ref[...] = _xattn_body(x, km_ref, vm_ref, gx_ref, wq_ref, qg_ref, wo_ref, q_ref, att_ref, 1, tl)


def _merge_xattn(x, scb, cin, st_pad, attn, yssd, cw, g_out, w_out, km, vm, g_xa, wq, qg, wo,
                 layer, nseq, seq):
    tl = ROW_TILE
    nt = seq // tl
    hb = tl // SUBLANES

    def rows(width):
        return pl.BlockSpec((tl, width), lambda b, j: (b * nt + j, 0))

    mem_spec = pl.BlockSpec((None, 1, MEM_LEN, MEM_DIM), lambda b, j: (layer, b, 0, 0))
    return pl.pallas_call(
        functools.partial(_merge_xattn_kernel, tl=tl),
        out_shape=jax.ShapeDtypeStruct((nseq * seq, D_MODEL), F32),
        grid=(nseq, nt),
        in_specs=[
            rows(D_MODEL), rows(CONV_DIM), rows(CONV_DIM),
            pl.BlockSpec((SUBLANES, CONV_DIM), lambda b, j: (jnp.maximum((b * nt + j) * hb - 1, 0), 0)),
            pl.BlockSpec((1, SUBLANES, CONV_DIM), lambda b, j: (b, 0, 0)),
            rows(ATTN_DIM), rows(SSD_DIM),
            _resident((SC_WIDTH, CONV_DIM)),
            _resident((1, D_MODEL)),
            _resident_layer((D_MODEL, D_MODEL), layer),
            mem_spec, mem_spec,
            _resident((1, D_MODEL)),
            _resident_layer((D_MODEL, MEM_DIM), layer),
            _resident((1, MEM_DIM)),
            _resident_layer((MEM_DIM, D_MODEL), layer),
        ],
        out_specs=rows(D_MODEL),
        scratch_shapes=[pltpu.VMEM((1, SUBLANES + tl, CONV_DIM), F32),
                        pltpu.VMEM((MEM_HEADS, tl, MEM_HEAD_DIM), F32),
                        pltpu.VMEM((MEM_HEADS, tl, MEM_HEAD_DIM), F32)],
        compiler_params=_params(("arbitrary", "arbitrary")),
        name="merge_xattn",
    )(x, scb, cin, cin, st_pad, attn, yssd, cw, g_out, w_out, km, vm, g_xa, wq, qg, wo)


def _pad_state(state, width):
    nseq, _, c = state.shape
    return jnp.concatenate([jnp.zeros((nseq, SUBLANES - (width - 1), c), F32), state], axis=1)


def _prep_w_in(w_in):
    w_t = jnp.transpose(w_in, (0, 2, 1)).astype(BF16)
    n_z0 = PROJ_QKV + FOX_HEADS
    n_dt0 = n_z0 + SSD_DIM + XBC_DIM
    pad = jnp.zeros((DEPTH, LANES - FOX_HEADS - SSD_HEADS, D_MODEL), BF16)
    w_fd = jnp.concatenate([w_t[:, PROJ_QKV:n_z0], w_t[:, n_dt0:], pad], axis=1)
    return w_t, w_t[:, n_z0:n_dt0], w_fd


def _layer_params(l, p):
    pad = jnp.zeros((LANES - FOX_HEADS - SSD_HEADS,), F32)
    a_neg = -jnp.exp(p['ssd_A_log'][l])
    return dict(
        fb=jnp.concatenate([p['fox_f_bias'][l], p['ssd_dt_bias'][l], pad])[None, :],
        qg=jnp.tile(p['fox_q_norm'][l], FOX_HEADS)[None, :],
        kg=jnp.tile(p['fox_k_norm'][l], FOX_HEADS)[None, :],
        av=jnp.concatenate([jnp.zeros((FOX_HEADS,), F32), a_neg, pad])[None, :],
        dv=jnp.repeat(p['ssd_D'][l], HEAD_DIM)[None, :],
        xa_qg=jnp.tile(p['xa_q_norm'][l], MEM_HEADS)[None, :],
    )


def _token_mix(l, x, p, lp, nseq, seq, conv_state, ssd_conv_state, ssd_state, paged, finish):
    n = nseq * seq
    scb, cin, qa, kn, ka, v, va, z, xbc, fdt, *lf_t = _inproj(
        x, p['mix_norm'][l][None, :], p['w_in_t'], lp['qg'], lp['kg'], lp['fb'], p['pm'], p['pc'], l, seq)

    if paged is None:
        attn = _fox_prompt(qa, ka, va, nseq, seq)
        k_out = jnp.transpose(kn, (0, 3, 1, 2))
        v_out = jnp.transpose(v, (0, 3, 1, 2))
        logf = jnp.transpose(lf_t[0], (0, 2, 1))
    else:
        logf = fdt[:, F_LANE0:F_LANE0 + FOX_HEADS].reshape(nseq, seq, FOX_HEADS)
        k_out = kn.reshape(nseq, seq, FOX_HEADS, HEAD_DIM)
        v_out = v.reshape(nseq, seq, FOX_HEADS, HEAD_DIM)
        page_table, cache_kt, cache_vt, cache_lf_t = paged
        eye = jnp.eye(FOX_HEADS, dtype=BF16)
        q4 = jnp.transpose(qa.reshape(nseq, seq, FOX_HEADS, LANES)[..., :HEAD_DIM], (0, 2, 1, 3))
        qbd = (q4[:, :, :, None, :] * eye[None, :, None, :, None]).reshape(nseq, FOX_HEADS * seq, ATTN_DIM)
        attn = _fox_paged(l, page_table, qbd, kn.reshape(nseq, seq, ATTN_DIM), v.reshape(nseq, seq, ATTN_DIM),
                          jnp.transpose(logf, (0, 2, 1)), cache_kt, cache_vt, cache_lf_t,
                          nseq, seq).reshape(n, ATTN_DIM)

    yssd, h_end = _ssd(xbc, _pad_state(ssd_conv_state, SSD_CONV), fdt, z, ssd_state, p['ssd_conv_w'][l],
                       p['ssd_conv_b'][l][None, :], lp['av'], lp['dv'], nseq, seq)
    x = finish(x, scb, cin, _pad_state(conv_state, SC_WIDTH), attn, yssd)

    cin3 = cin.reshape(nseq, seq, CONV_DIM)
    xbc3 = xbc.reshape(nseq, seq, XBC_DIM)
    new_conv = jnp.concatenate([conv_state, cin3], axis=1)[:, -(SC_WIDTH - 1):]
    new_ssd_conv = jnp.concatenate([ssd_conv_state, xbc3], axis=1)[:, -(SSD_CONV - 1):]
    state = (k_out, v_out, logf, new_conv, new_ssd_conv, h_end)
    return x, state


def kernel(x_prompt, x_sample, cache_fox_k, cache_fox_v, cache_fox_logf, cache_mem_k, cache_mem_v,
           state_conv, state_ssd_conv, state_ssd, page_table, mem_prompt,
           ffn1_norm, ffn1_wg, ffn1_wu, ffn1_wd, mix_norm, w_in, sc_conv_w, fox_q_norm, fox_k_norm,
           fox_f_bias, ssd_conv_w, ssd_conv_b, ssd_dt_bias, ssd_A_log, ssd_D, mix_out_norm, w_out,
           xa_norm, mem_norm, xa_wq, xa_wk, xa_wv, xa_q_norm, xa_k_norm, xa_wo,
           ffn2_norm, ffn2_wg, ffn2_wu, ffn2_wd, final_norm):
    b_p, s_p, _ = x_prompt.shape
    b_s, s_s, _ = x_sample.shape
    n_phys = cache_fox_k.shape[1]
    pm, pc = _bias_placement()
    p = dict(mix_norm=mix_norm, w_in_t=_prep_w_in(w_in), sc_conv_w=sc_conv_w, fox_q_norm=fox_q_norm,
             fox_k_norm=fox_k_norm, fox_f_bias=fox_f_bias, ssd_conv_w=ssd_conv_w, ssd_conv_b=ssd_conv_b,
             ssd_dt_bias=ssd_dt_bias, ssd_A_log=ssd_A_log, ssd_D=ssd_D, mix_out_norm=mix_out_norm,
             xa_q_norm=xa_q_norm, w_out_bf=w_out.astype(BF16), pm=pm, pc=pc)
    wg1, wu1, wd1 = ffn1_wg.astype(BF16), ffn1_wu.astype(BF16), ffn1_wd.astype(BF16)
    wg2, wu2, wd2 = ffn2_wg.astype(BF16), ffn2_wu.astype(BF16), ffn2_wd.astype(BF16)
    wq, wo = xa_wq.astype(BF16), xa_wo.astype(BF16)
    fg = final_norm[None, :]

    km_p, vm_p = _memkv(mem_prompt, mem_norm[:, None, :], xa_wk.astype(BF16), xa_wv.astype(BF16),
                        jnp.tile(xa_k_norm, (1, MEM_HEADS))[:, None, :])
    km_s, vm_s = cache_mem_k, cache_mem_v
    cache_kt = jnp.transpose(cache_fox_k, (0, 1, 3, 4, 2)).reshape(DEPTH, n_phys, ATTN_DIM, PAGE_SIZE)
    cache_vt = jnp.transpose(cache_fox_v, (0, 1, 3, 4, 2)).reshape(DEPTH, n_phys, ATTN_DIM, PAGE_SIZE)
    cache_lf_t = jnp.transpose(cache_fox_logf, (0, 1, 3, 2))

    zero_conv = jnp.zeros((b_p, SC_WIDTH - 1, CONV_DIM), F32)
    zero_ssd_conv = jnp.zeros((b_p, SSD_CONV - 1, XBC_DIM), F32)
    zero_ssd = jnp.zeros((b_p, SSD_HEADS, HEAD_DIM, SSD_STATE), F32)

    xp = x_prompt.reshape(b_p * s_p, D_MODEL)
    xs = x_sample.reshape(b_s * s_s, D_MODEL)
    st_p, st_s = [], []
    for l in range(DEPTH):
        lp = _layer_params(l, p)
        last = l == DEPTH - 1
        groups = []
        for x, nseq, seq, km, vm, cs, scs, ss, paged in (
                (xp, b_p, s_p, km_p, vm_p, zero_conv, zero_ssd_conv, zero_ssd, None),
                (xs, b_s, s_s, km_s, vm_s, state_conv[l], state_ssd_conv[l], state_ssd[l],
                 (page_table, cache_kt, cache_vt, cache_lf_t))):
            def finish(x, scb, cin, conv_pad, attn, yssd, nseq=nseq, seq=seq, km=km, vm=vm):
                mix = (p['sc_conv_w'][l], p['mix_out_norm'][l][None, :], p['w_out_bf'])
                xa = (xa_norm[l][None, :], wq, lp['xa_qg'], wo)
                f2 = (ffn2_norm[l][None, :], wg2, wu2, wd2, fg)
                if seq % ROW_TILE == 0:
                    x = _merge_xattn(x, scb, cin, conv_pad, attn, yssd, *mix, km, vm, *xa, l, nseq, seq)
                else:
                    x = _merge(x, scb, cin, conv_pad, attn, yssd, *mix, l, nseq, seq)
                    x = _xattn(x, km, vm, l, *xa, nseq, seq)
                return _ffn(x, *f2, l, last)

            x = _ffn(x, ffn1_norm[l][None, :], wg1, wu1, wd1, fg, l, False)
            x, st = _token_mix(l, x, p, lp, nseq, seq, cs, scs, ss, paged, finish)
            groups.append((x, st))
        (xp, sp), (xs, ss_) = groups
        st_p.append(sp)
        st_s.append(ss_)

    def stack(states, idx):
        return jnp.stack([s[idx] for s in states])

    return (xp.reshape(b_p, s_p, D_MODEL), xs.reshape(b_s, s_s, D_MODEL),
            stack(st_p, 0), stack(st_p, 1), stack(st_p, 2),
            km_p.reshape(DEPTH, b_p, MEM_LEN, MEM_HEADS, MEM_HEAD_DIM),
            vm_p.reshape(DEPTH, b_p, MEM_LEN, MEM_HEADS, MEM_HEAD_DIM),
            stack(st_p, 3), stack(st_p, 4), stack(st_p, 5),
            stack(st_s, 0), stack(st_s, 1), stack(st_s, 2), stack(st_s, 3), stack(st_s, 4), stack(st_s, 5))
```
